```python
import math
import jax, jax.numpy as jnp
from jax import lax
import numpy as np

D_MODEL = 1024
BATCH = 32
SEQ = 256
DEPTH = 1
DEC_BATCH = 8
DEC_SEQ = 2048
PAST_LEN = 256

GRID_W = 64
D_FF = 2816
N_DIR = 2
EPS = 1e-6
M_HEADS = 4
M_DK = 128
M_DV = 128
M_WIDTH = M_HEADS * M_DV
M_CHUNK = 64
M_COLS = 4 * M_WIDTH + 2 * N_DIR * M_HEADS
R_HEADS = 8
R_HEAD = 64
R_WIDTH = R_HEADS * R_HEAD
R_DECAY_LORA = 64
R_AAA_LORA = 64
R_GATE_LORA = 128
R_LN_EPS = 64e-5
R_COLS = 3 * R_WIDTH + N_DIR * R_DECAY_LORA + N_DIR * R_AAA_LORA + R_GATE_LORA
MIX_WIDTH = M_WIDTH + R_WIDTH
P_IN = M_COLS + R_COLS

kernel_name = 'hybrid_mlstm_rwkv7_diffusion_step'


def rmsnorm(x, g):
    xf = x.astype(jnp.float32)
    y = xf * lax.rsqrt(jnp.mean(xf * xf, axis=-1, keepdims=True) + EPS)
    return (y * g).astype(x.dtype)


def modulate(h, shift, scale):
    return h * (1.0 + scale) + shift


def swiglu(h, w_in, w_out):
    gate, up = jnp.split(h @ w_in, 2, axis=-1)
    return (jax.nn.silu(gate) * up) @ w_out


def shift_along(u, axis, offset):
    pad = [(0, 0)] * u.ndim
    n = u.shape[axis]
    if offset > 0:
        pad[axis] = (1, 0)
        return lax.slice_in_dim(jnp.pad(u, pad), 0, n, axis=axis)
    pad[axis] = (0, 1)
    return lax.slice_in_dim(jnp.pad(u, pad), 1, n + 1, axis=axis)


def centred_conv3(u, w, b):
    return w[0] * shift_along(u, 1, 1) + w[1] * u + w[2] * shift_along(u, 1, -1) + b


def token_shift(u, mu, grid):
    if grid:
        B, T, C = u.shape
        rows = T // GRID_W
        g = u.reshape(B, rows, GRID_W, C)
        out = (g + mu[0] * (shift_along(g, 2, 1) - g) + mu[1] * (shift_along(g, 2, -1) - g)
               + mu[2] * (shift_along(g, 1, 1) - g) + mu[3] * (shift_along(g, 1, -1) - g))
        return out.reshape(B, T, C)
    return u + mu[0] * (shift_along(u, 1, 1) - u) + mu[1] * (shift_along(u, 1, -1) - u)


def mlstm_chunkwise(q, k, v, i_pre, logf, C0, n0, m0):
    B, H, T, _ = q.shape
    L = M_CHUNK
    nc = T // L
    k = k * (M_DK ** -0.5)

    def to_chunks(a):
        return jnp.moveaxis(a.reshape((B, H, nc, L) + a.shape[3:]), 2, 0)

    qc, kc, vc, ic = to_chunks(q), to_chunks(k), to_chunks(v), to_chunks(i_pre)
    bc = jnp.cumsum(to_chunks(logf), axis=-1)
    causal = jnp.tril(jnp.ones((L, L), dtype=bool))

    def step(carry, xs):
        C, n, m = carry
        qt, kt, vt, it, bt = xs
        D = bt[..., :, None] - bt[..., None, :] + it[..., None, :]
        D = jnp.where(causal, D, -jnp.inf)
        m_inter = bt + m[..., None]
        m_t = jnp.maximum(m_inter, jnp.max(D, axis=-1))
        S = jnp.einsum('bhtd,bhsd->bhts', qt, kt) * jnp.exp(D - m_t[..., None])
        sc = jnp.exp(m_inter - m_t)
        num = jnp.einsum('bhts,bhse->bhte', S, vt) + sc[..., None] * jnp.einsum('bhtd,bhde->bhte', qt, C)
        den = jnp.sum(S, axis=-1) + sc * jnp.einsum('bhtd,bhd->bht', qt, n)
        h = num / jnp.maximum(jnp.abs(den), jnp.exp(-m_t))[..., None]
        bL = bt[..., -1]
        gs = bL[..., None] - bt + it
        m_new = jnp.maximum(bL + m, jnp.max(gs, axis=-1))
        wk = jnp.exp(gs - m_new[..., None])
        decay = jnp.exp(bL + m - m_new)
        C_new = decay[..., None, None] * C + jnp.einsum('bhs,bhsd,bhse->bhde', wk, kt, vt)
        n_new = decay[..., None] * n + jnp.einsum('bhs,bhsd->bhd', wk, kt)
        return (C_new, n_new, m_new), h

    (C, n, m), h = lax.scan(step, (C0, n0, m0), (qc, kc, vc, ic, bc))
    h = jnp.moveaxis(h, 0, 2).reshape(B, H, T, M_DV)
    return h, C, n, m


def mlstm_group(u, conv_w, conv_b, gate_b, head_g, C0, n0, m0):
    B, T, _ = u.shape
    qk = jax.nn.silu(centred_conv3(u[..., :2 * M_WIDTH], conv_w, conv_b))
    q = qk[..., :M_WIDTH].reshape(B, T, M_HEADS, M_DK).transpose(0, 2, 1, 3)
    k = qk[..., M_WIDTH:].reshape(B, T, M_HEADS, M_DK).transpose(0, 2, 1, 3)
    v = u[..., 2 * M_WIDTH:3 * M_WIDTH].reshape(B, T, M_HEADS, M_DV).transpose(0, 2, 1, 3)
    o = jax.nn.sigmoid(u[..., 3 * M_WIDTH:4 * M_WIDTH]).reshape(B, T, M_HEADS, M_DV)
    gates = u[..., 4 * M_WIDTH:].reshape(B, T, 2, N_DIR, M_HEADS) + gate_b
    gates = gates.transpose(2, 3, 0, 4, 1)
    i_pre = gates[0]
    logf = jax.nn.log_sigmoid(gates[1])
    hs, Cs, ns, ms = [], [], [], []
    for d in range(N_DIR):
        args = (q, k, v, i_pre[d], logf[d])
        if d == 1:
            args = tuple(jnp.flip(a, axis=2) for a in args)
        h, C, n, m = mlstm_chunkwise(*args, C0[:, d], n0[:, d], m0[:, d])
        if d == 1:
            h = jnp.flip(h, axis=2)
        hs.append(h)
        Cs.append(C)
        ns.append(n)
        ms.append(m)
    h = o * (hs[0] + hs[1]).transpose(0, 2, 1, 3)
    h = h * lax.rsqrt(jnp.mean(h * h, axis=-1, keepdims=True) + EPS)
    out = h.reshape(B, T, M_WIDTH) * head_g
    return out, jnp.stack(Cs, axis=1), jnp.stack(ns, axis=1), jnp.stack(ms, axis=1)


def rwkv7_scan(r, w, k, v, kk, a, S0, reverse):
    xs = tuple(jnp.moveaxis(t, 1, 0) for t in (r, w, k, v, kk, a))

    def step(S, x):
        r_t, w_t, k_t, v_t, kk_t, a_t = x
        sa = jnp.einsum('bhij,bhj->bhi', S, -kk_t)
        S = S * w_t[:, :, None, :] + sa[..., None] * (kk_t * a_t)[:, :, None, :] + v_t[..., None] * k_t[:, :, None, :]
        y = jnp.einsum('bhij,bhj->bhi', S, r_t)
        return S, y

    S, y = lax.scan(step, S0, xs, reverse=reverse)
    return jnp.moveaxis(y, 0, 1), S


def rwkv_group(u, mu, w0, w2, a0, a2, g2, k_k, k_a, r_k, ln_g, ln_b, S0, grid):
    B, T, _ = u.shape
    u = token_shift(u, mu, grid)

    def heads(t):
        return t.reshape(B, T, R_HEADS, R_HEAD)

    r = u[..., :R_WIDTH]
    k = u[..., R_WIDTH:2 * R_WIDTH]
    v = u[..., 2 * R_WIDTH:3 * R_WIDTH]
    off = 3 * R_WIDTH
    wd = u[..., off:off + N_DIR * R_DECAY_LORA].reshape(B, T, N_DIR, R_DECAY_LORA)
    off += N_DIR * R_DECAY_LORA
    ad = u[..., off:off + N_DIR * R_AAA_LORA].reshape(B, T, N_DIR, R_AAA_LORA)
    off += N_DIR * R_AAA_LORA
    g = jax.nn.sigmoid(u[..., off:off + R_GATE_LORA]) @ g2
    kk = heads(k * k_k)
    kk = kk / jnp.maximum(jnp.sqrt(jnp.sum(kk * kk, axis=-1, keepdims=True)), 1e-12)
    ys, Ss, ks = [], [], []
    for d in range(N_DIR):
        w = -jax.nn.softplus(-(w0[d] + jnp.tanh(wd[:, :, d]) @ w2[d])) - 0.5
        decay = jnp.exp(-jnp.exp(w))
        a = jax.nn.sigmoid(a0[d] + ad[:, :, d] @ a2[d])
        kd = k * (1.0 + (a - 1.0) * k_a)
        y, S = rwkv7_scan(heads(r), heads(decay), heads(kd), heads(v), kk, heads(a), S0[:, d], d == 1)
        ys.append(y)
        Ss.append(S)
        ks.append(kd)
    y = ys[0] + ys[1]
    mean = jnp.mean(y, axis=-1, keepdims=True)
    var = jnp.mean((y - mean) ** 2, axis=-1, keepdims=True)
    y = ((y - mean) * lax.rsqrt(var + R_LN_EPS)).reshape(B, T, R_WIDTH) * ln_g + ln_b
    k_bar = heads(0.5 * (ks[0] + ks[1]))
    bonus = jnp.sum(heads(r) * k_bar * r_k, axis=-1, keepdims=True) * heads(v)
    out = (y + bonus.reshape(B, T, R_WIDTH)) * g
    return out, jnp.stack(Ss, axis=1)


def token_mix(h, p, init, grid):
    C0, n0, m0, S0 = init
    u = (h @ p['mix_w_in']).astype(jnp.float32)
    m_out, C, n, m = mlstm_group(u[..., :M_COLS], p['m_conv_w'], p['m_conv_b'], p['m_gate_b'],
                                 p['m_head_g'], C0, n0, m0)
    r_out, S = rwkv_group(u[..., M_COLS:], p['r_mu'], p['r_w0'], p['r_w2'], p['r_a0'], p['r_a2'],
                          p['r_g2'], p['r_k_k'], p['r_k_a'], p['r_r_k'], p['r_ln_g'], p['r_ln_b'], S0, grid)
    y = jnp.concatenate([m_out, r_out], axis=-1).astype(h.dtype) @ p['mix_w_out']
    return y, (C, n, m, S)


def trunk_layer(x, cond, grid, init, p):
    mod = (jax.nn.silu(cond) @ p['ada_w'] + p['ada_b']).reshape(-1, 1, 9 * D_MODEL)
    sh1, sc1, ga1, sh2, sc2, ga2, sh3, sc3, ga3 = jnp.split(mod, 9, axis=-1)
    h = modulate(rmsnorm(x, p['norm_ffn1']), sh1, sc1)
    x = x + 0.5 * ga1 * swiglu(h, p['ffn1_w_in'], p['ffn1_w_out'])
    h = modulate(rmsnorm(x, p['norm_mix']), sh2, sc2)
    mix, states = token_mix(h, p, init, grid)
    x = x + ga2 * mix
    h = modulate(rmsnorm(x, p['norm_ffn2']), sh3, sc3)
    x = x + 0.5 * ga3 * swiglu(h, p['ffn2_w_in'], p['ffn2_w_out'])
    return x, states


def setup_inputs(seed: int = 0) -> dict:
    key = jax.random.key(seed)
    keys = iter(jax.random.split(key, 64))

    def nrm(shape, scale):
        return scale * jax.random.normal(next(keys), shape, jnp.float32)

    def gain(shape):
        return 1.0 + 0.01 * jax.random.normal(next(keys), shape, jnp.float32)

    L, D = DEPTH, D_MODEL
    m_gate_b = jnp.stack([nrm((L, N_DIR, M_HEADS), 0.1), 3.0 + nrm((L, N_DIR, M_HEADS), 0.5)], axis=1)
    return {
        'x_prompt': nrm((BATCH, SEQ, D), 1.0),
        'x_sample': nrm((DEC_BATCH, DEC_SEQ, D), 1.0),
        'c': nrm((DEC_BATCH, D), 1.0),
        'state_mlstm_C': nrm((DEC_BATCH, L, N_DIR, M_HEADS, M_DK, M_DV), 0.1),
        'state_mlstm_n': jnp.abs(nrm((DEC_BATCH, L, N_DIR, M_HEADS, M_DK), 0.5)),
        'state_mlstm_m': nrm((DEC_BATCH, L, N_DIR, M_HEADS), 0.5),
        'state_rwkv_S': nrm((DEC_BATCH, L, N_DIR, R_HEADS, R_HEAD, R_HEAD), 0.1),
        'c_ctx': nrm((D,), 1.0),
        'ada_w': nrm((L, D, 9 * D), 0.5 * D ** -0.5),
        'ada_b': nrm((L, 9 * D), 0.01),
        'norm_ffn1': gain((L, D)),
        'ffn1_w_in': nrm((L, D, 2 * D_FF), D ** -0.5),
        'ffn1_w_out': nrm((L, D_FF, D), D_FF ** -0.5),
        'norm_mix': gain((L, D)),
        'mix_w_in': nrm((L, D, P_IN), D ** -0.5),
        'mix_w_out': nrm((L, MIX_WIDTH, D), MIX_WIDTH ** -0.5),
        'm_conv_w': nrm((L, 3, 2 * M_WIDTH), 3 ** -0.5),
        'm_conv_b': nrm((L, 2 * M_WIDTH), 0.01),
        'm_gate_b': m_gate_b,
        'm_head_g': gain((L, M_WIDTH)),
        'r_mu': jax.random.uniform(next(keys), (L, 4, R_COLS), jnp.float32, 0.0, 0.5),
        'r_w0': -2.0 + nrm((L, N_DIR, R_WIDTH), 0.5),
        'r_w2': nrm((L, N_DIR, R_DECAY_LORA, R_WIDTH), 0.5 * R_DECAY_LORA ** -0.5),
        'r_a0': nrm((L, N_DIR, R_WIDTH), 0.5),
        'r_a2': nrm((L, N_DIR, R_AAA_LORA, R_WIDTH), R_AAA_LORA ** -0.5),
        'r_g2': nrm((L, R_GATE_LORA, R_WIDTH), R_GATE_LORA ** -0.5),
        'r_k_k': 0.85 + nrm((L, R_WIDTH), 0.1),
        'r_k_a': 1.0 + nrm((L, R_WIDTH), 0.1),
        'r_r_k': nrm((L, R_HEADS, R_HEAD), 0.1),
        'r_ln_g': gain((L, R_WIDTH)),
        'r_ln_b': nrm((L, R_WIDTH), 0.01),
        'norm_ffn2': gain((L, D)),
        'ffn2_w_in': nrm((L, D, 2 * D_FF), D ** -0.5),
        'ffn2_w_out': nrm((L, D_FF, D), D_FF ** -0.5),
        'norm_final': gain((D,)),
    }


def reference(x_prompt, x_sample, c, state_mlstm_C, state_mlstm_n, state_mlstm_m, state_rwkv_S, c_ctx,
              ada_w, ada_b, norm_ffn1, ffn1_w_in, ffn1_w_out, norm_mix, mix_w_in, mix_w_out,
              m_conv_w, m_conv_b, m_gate_b, m_head_g, r_mu, r_w0, r_w2, r_a0, r_a2, r_g2,
              r_k_k, r_k_a, r_r_k, r_ln_g, r_ln_b, norm_ffn2, ffn2_w_in, ffn2_w_out, norm_final):
    layers = [dict(ada_w=ada_w[l], ada_b=ada_b[l], norm_ffn1=norm_ffn1[l], ffn1_w_in=ffn1_w_in[l],
                   ffn1_w_out=ffn1_w_out[l], norm_mix=norm_mix[l], mix_w_in=mix_w_in[l],
                   mix_w_out=mix_w_out[l], m_conv_w=m_conv_w[l], m_conv_b=m_conv_b[l],
                   m_gate_b=m_gate_b[l], m_head_g=m_head_g[l], r_mu=r_mu[l], r_w0=r_w0[l],
                   r_w2=r_w2[l], r_a0=r_a0[l], r_a2=r_a2[l], r_g2=r_g2[l], r_k_k=r_k_k[l],
                   r_k_a=r_k_a[l], r_r_k=r_r_k[l], r_ln_g=r_ln_g[l], r_ln_b=r_ln_b[l],
                   norm_ffn2=norm_ffn2[l], ffn2_w_in=ffn2_w_in[l], ffn2_w_out=ffn2_w_out[l])
              for l in range(DEPTH)]

    Bp = x_prompt.shape[0]
    f32 = jnp.float32
    h = x_prompt
    Cl, nl, ml, Sl = [], [], [], []
    for l in range(DEPTH):
        init = (jnp.zeros((Bp, N_DIR, M_HEADS, M_DK, M_DV), f32), jnp.zeros((Bp, N_DIR, M_HEADS, M_DK), f32),
                jnp.zeros((Bp, N_DIR, M_HEADS), f32), jnp.zeros((Bp, N_DIR, R_HEADS, R_HEAD, R_HEAD), f32))
        h, (C, n, m, S) = trunk_layer(h, c_ctx, False, init, layers[l])
        Cl.append(C)
        nl.append(n)
        ml.append(m)
        Sl.append(S)
    y_prompt = rmsnorm(h, norm_final)
    new_mlstm_C = jnp.stack(Cl, axis=1)
    new_mlstm_n = jnp.stack(nl, axis=1)
    new_mlstm_m = jnp.stack(ml, axis=1)
    new_rwkv_S = jnp.stack(Sl, axis=1)

    h = x_sample
    for l in range(DEPTH):
        init = (state_mlstm_C[:, l].astype(f32), state_mlstm_n[:, l].astype(f32),
                state_mlstm_m[:, l].astype(f32), state_rwkv_S[:, l].astype(f32))
        h, _ = trunk_layer(h, c, True, init, layers[l])
    y_sample = rmsnorm(h, norm_final)
    return (y_prompt, y_sample, new_mlstm_C, new_mlstm_n, new_mlstm_m, new_rwkv_S)
```

```python
import functools

import jax
import jax.numpy as jnp
from jax import lax
from jax.experimental import pallas as pl
from jax.experimental.pallas import tpu as pltpu

F32 = jnp.float32
BF16 = jnp.bfloat16

EPS = 1e-6
R_LN_EPS = 64e-5
GRID_W = 64
M_HEADS = 4
M_DK = 128
M_DV = 128
M_WIDTH = M_HEADS * M_DV
R_HEADS = 8
R_HEAD = 64
R_WIDTH = R_HEADS * R_HEAD
R_LORA = 128
R_COLS = 3 * R_WIDTH + 3 * R_LORA
GATE_COLS = 16
LANES = 128

FFN_TILE = 512
FFN_CHUNK = 256
MIX_TILE = 512
M_CHUNK = 256
R_CHUNK = 64
VMEM_LIMIT = 56 * 1024 * 1024


def _dot(a, b):
    return jnp.dot(a, b, preferred_element_type=F32)


def _dot_nt(a, b):
    return lax.dot_general(a, b, (((1,), (1,)), ((), ())), preferred_element_type=F32)


def _dot_tn(a, b):
    return lax.dot_general(a, b, (((0,), (0,)), ((), ())), preferred_element_type=F32)


def _split2(x):
    hi = x.astype(BF16)
    lo = (x - hi.astype(F32)).astype(BF16)
    return hi, lo


def _split3(x):
    hi = x.astype(BF16)
    r1 = x - hi.astype(F32)
    mid = r1.astype(BF16)
    lo = (r1 - mid.astype(F32)).astype(BF16)
    return hi, mid, lo


def _sel_dot(mat, x):
    hi, mid, lo = _split3(x)
    return _dot(mat, hi) + _dot(mat, mid) + _dot(mat, lo)


def _dot_sel(x, mat):
    hi, mid, lo = _split3(x)
    return _dot(hi, mat) + _dot(mid, mat) + _dot(lo, mat)


def _mm3(a, b, dot=_dot):
    ah, al = _split2(a)
    bh, bl = _split2(b)
    return dot(ah, bh) + dot(ah, bl) + dot(al, bh)


def _silu(x):
    return x * jax.nn.sigmoid(x)


def _rms_mod(x, g, shift, scale):
    y = x * lax.rsqrt(jnp.mean(x * x, axis=-1, keepdims=True) + EPS) * g
    return y * (1.0 + scale) + shift


def _cparams(sem):
    return pltpu.CompilerParams(dimension_semantics=sem, vmem_limit_bytes=VMEM_LIMIT)


def _const_spec(shape):
    nd = len(shape)
    return pl.BlockSpec(shape, lambda *_: (0,) * nd, pipeline_mode=pl.Buffered(1))


def _ada_body(c_ref, w_ref, b_ref, o_ref):
    s = _silu(c_ref[...])
    o_ref[...] = _dot(s.astype(BF16), w_ref[...].astype(BF16)) + b_ref[...]


def _ada_call(cond, w, b):
    rows, d = cond.shape
    n = w.shape[1]
    tn = d
    return pl.pallas_call(
        _ada_body,
        grid=(n // tn,),
        in_specs=[
            pl.BlockSpec((rows, d), lambda j: (0, 0)),
            pl.BlockSpec((d, tn), lambda j: (0, j)),
            pl.BlockSpec((1, tn), lambda j: (0, j)),
        ],
        out_specs=pl.BlockSpec((rows, tn), lambda j: (0, j)),
        out_shape=jax.ShapeDtypeStruct((rows, n), F32),
        compiler_params=_cparams(("arbitrary",)),
        name="ada",
    )(cond, w, b)


def _ffn_body(mod_base, with_mix, with_final, *refs):
    refs = list(refs)
    x_ref = refs.pop(0)
    if with_mix:
        mo_ref, ro_ref, wmm_ref, wmr_ref = refs[:4]
        refs = refs[4:]
    mod_ref, g_ref, win_ref, wout_ref = refs[:4]
    refs = refs[4:]
    if with_final:
        gfin_ref = refs.pop(0)
    out_ref, a_scr = refs

    x = x_ref[...]
    if with_mix:
        mix = _dot(mo_ref[...].astype(BF16), wmm_ref[...]) + _dot(ro_ref[...].astype(BF16), wmr_ref[...])
        x = x + mod_ref[0, mod_base - 1:mod_base, :] * mix
    shift = mod_ref[0, mod_base:mod_base + 1, :]
    scale = mod_ref[0, mod_base + 1:mod_base + 2, :]
    gate = mod_ref[0, mod_base + 2:mod_base + 3, :]
    hb = _rms_mod(x, g_ref[...], shift, scale).astype(BF16)
    dff = wout_ref.shape[0]
    for j in range(dff // FFN_CHUNK):
        lo = j * FFN_CHUNK
        gt = _dot(hb, win_ref[:, lo:lo + FFN_CHUNK])
        up = _dot(hb, win_ref[:, dff + lo:dff + lo + FFN_CHUNK])
        a_scr[:, lo:lo + FFN_CHUNK] = (_silu(gt) * up).astype(BF16)
    y = x + (0.5 * gate) * _dot(a_scr[...], wout_ref[...])
    if with_final:
        y = y * lax.rsqrt(jnp.mean(y * y, axis=-1, keepdims=True) + EPS) * gfin_ref[...]
    out_ref[...] = y


def _ffn_call(x, mod3, row_of_tile, mod_base, g, w_in, w_out, mix=None, g_final=None):
    n, d = x.shape
    tm = FFN_TILE
    dff = w_out.shape[0]
    tok = lambda i: (i, 0)
    args = [x]
    specs = [pl.BlockSpec((tm, d), tok)]
    if mix is not None:
        mo, ro, wmm, wmr = mix
        args += [mo, ro, wmm, wmr]
        specs += [pl.BlockSpec((tm, mo.shape[1]), tok), pl.BlockSpec((tm, ro.shape[1]), tok),
                  _const_spec(wmm.shape), _const_spec(wmr.shape)]
    args += [mod3, g, w_in, w_out]
    specs += [pl.BlockSpec((1,) + mod3.shape[1:], lambda i: (row_of_tile(i), 0, 0)),
              _const_spec(g.shape), _const_spec(w_in.shape), _const_spec(w_out.shape)]
    if g_final is not None:
        args.append(g_final)
        specs.append(_const_spec(g_final.shape))
    body = functools.partial(_ffn_body, mod_base, mix is not None, g_final is not None)
    return pl.pallas_call(
        body,
        grid=(n // tm,),
        in_specs=specs,
        out_specs=pl.BlockSpec((tm, d), tok),
        out_shape=jax.ShapeDtypeStruct((n, d), F32),
        scratch_shapes=[pltpu.VMEM((tm, dff), BF16)],
        compiler_params=_cparams(("parallel",)),
        name="ffn_mix" if mix is not None else "ffn",
    )(*args)


def _mixin_body(grid_mode, has_halo, tiles_per_seq, *refs):
    refs = list(refs)
    x_ref = refs.pop(0)
    if has_halo:
        xp_ref, xn_ref = refs[:2]
        refs = refs[2:]
    (mod_ref, g_ref, wm_ref, wg_ref, wgt_ref, wr_ref, cw_ref, cb_ref, mu_ref,
     q_ref, k_ref, v_ref, o_ref, gc_ref, gr_ref, xr_ref) = refs

    tm = x_ref.shape[0]
    shift = mod_ref[0, 3:4, :]
    scale = mod_ref[0, 4:5, :]
    g = g_ref[...]
    hb = _rms_mod(x_ref[...], g, shift, scale).astype(BF16)
    row = lax.broadcasted_iota(jnp.int32, (tm, 1), 0)
    nqk = 2 * M_WIDTH

    if has_halo:
        i = pl.program_id(0)
        pos = i % tiles_per_seq
        keep_p = (pos > 0).astype(F32)
        keep_n = (pos < tiles_per_seq - 1).astype(F32)
        hp32 = _rms_mod(xp_ref[...], g, shift, scale)
        hn32 = _rms_mod(xn_ref[...], g, shift, scale)
        hp = hp32.astype(BF16)
        hn = hn32.astype(BF16)
        nh = hp32.shape[0]
        edge = 16
        qk_prev = _dot(hp32[nh - edge:, :].astype(BF16), wm_ref[:, :nqk])[edge - 1:edge, :] * keep_p
        qk_next = _dot(hn32[:edge, :].astype(BF16), wm_ref[:, :nqk])[0:1, :] * keep_n
    else:
        qk_prev = jnp.zeros((1, nqk), F32)
        qk_next = jnp.zeros((1, nqk), F32)

    uqk = _dot(hb, wm_ref[:, :nqk])
    u_dn = jnp.where(row == 0, qk_prev, pltpu.roll(uqk, 1, 0))
    u_up = jnp.where(row == tm - 1, qk_next, pltpu.roll(uqk, tm - 1, 0))
    qk = _silu(cw_ref[0:1, :] * u_dn + cw_ref[1:2, :] * uqk + cw_ref[2:3, :] * u_up + cb_ref[...])
    q_ref[...] = qk[:, :M_WIDTH]
    k_ref[...] = qk[:, M_WIDTH:] * (M_DK ** -0.5)
    v_ref[...] = _dot(hb, wm_ref[:, nqk:nqk + M_WIDTH])
    o_ref[...] = _dot(hb, wm_ref[:, nqk + M_WIDTH:])
    gc_ref[...] = _dot(hb, wg_ref[...])
    gr_ref[...] = _dot_nt(wgt_ref[...], hb)

    ur = _dot(hb, wr_ref[...])
    if grid_mode:
        col = row % GRID_W
        left = jnp.where(col == 0, 0.0, pltpu.roll(ur, 1, 0))
        right = jnp.where(col == GRID_W - 1, 0.0, pltpu.roll(ur, tm - 1, 0))
        ur_p = _dot(hp, wr_ref[...]) * keep_p
        ur_n = _dot(hn, wr_ref[...]) * keep_n
        up = jnp.concatenate([ur_p, ur[:tm - GRID_W, :]], axis=0)
        down = jnp.concatenate([ur[GRID_W:, :], ur_n], axis=0)
        xr = (ur + mu_ref[0:1, :] * (left - ur) + mu_ref[1:2, :] * (right - ur)
              + mu_ref[2:3, :] * (up - ur) + mu_ref[3:4, :] * (down - ur))
    else:
        left = jnp.where(row == 0, 0.0, pltpu.roll(ur, 1, 0))
        right = jnp.where(row == tm - 1, 0.0, pltpu.roll(ur, tm - 1, 0))
        xr = ur + mu_ref[0:1, :] * (left - ur) + mu_ref[1:2, :] * (right - ur)
    xr_ref[...] = xr


def _mixin_call(x, seq_len, grid_mode, mod3, row_of_tile, g, wm, wg, wgt, wr, cw, cb, mu):
    n, d = x.shape
    if grid_mode:
        tm = MIX_TILE
        has_halo = True
    else:
        tm = seq_len
        has_halo = False
    tiles_per_seq = seq_len // tm
    tok = lambda i: (i, 0)
    args = [x]
    specs = [pl.BlockSpec((tm, d), tok)]
    if has_halo:
        hb = R_CHUNK
        per = tm // hb
        last = n // hb - 1
        args += [x, x]
        specs += [pl.BlockSpec((hb, d), lambda i: (jnp.maximum(i * per - 1, 0), 0)),
                  pl.BlockSpec((hb, d), lambda i: (jnp.minimum((i + 1) * per, last), 0))]
    args += [mod3, g, wm, wg, wgt, wr, cw, cb, mu]
    specs += [pl.BlockSpec((1,) + mod3.shape[1:], lambda i: (row_of_tile(i), 0, 0))]
    specs += [_const_spec(a.shape) for a in (g, wm, wg, wgt, wr, cw, cb, mu)]
    outs = [jax.ShapeDtypeStruct((n, M_WIDTH), F32)] * 4 + [
        jax.ShapeDtypeStruct((n, LANES), F32),
        jax.ShapeDtypeStruct((GATE_COLS, n), F32),
        jax.ShapeDtypeStruct((n, R_COLS), F32),
    ]
    out_specs = [pl.BlockSpec((tm, M_WIDTH), tok)] * 4 + [
        pl.BlockSpec((tm, LANES), tok),
        pl.BlockSpec((GATE_COLS, tm), lambda i: (0, i)),
        pl.BlockSpec((tm, R_COLS), tok),
    ]
    body = functools.partial(_mixin_body, grid_mode, has_halo, tiles_per_seq)
    return pl.pallas_call(
        body,
        grid=(n // tm,),
        in_specs=specs,
        out_specs=out_specs,
        out_shape=outs,
        compiler_params=_cparams(("parallel",)),
        name="mix_in_grid" if grid_mode else "mix_in_seq",
    )(*args)


def _mlstm_body(zero_init, nc, *refs):
    refs = list(refs)
    q_ref, k_ref, v_ref, o_ref, gc_ref, gr_ref, gbr_ref, gbc_ref, hg_ref = refs[:9]
    refs = refs[9:]
    if not zero_init:
        c0_ref, n0_ref, m0_ref = refs[:3]
        refs = refs[3:]
    out_ref, cs_ref, ns_ref, ms_ref, c_scr, n_scr, m_scr = refs

    d = pl.program_id(1)
    c = pl.program_id(2)
    L = q_ref.shape[0]
    fwd = d == 0

    @pl.when(c == 0)
    def _():
        if zero_init:
            c_scr[...] = jnp.zeros_like(c_scr)
            n_scr[...] = jnp.zeros_like(n_scr)
            m_scr[...] = jnp.zeros_like(m_scr)
        else:
            c_scr[...] = c0_ref[0, 0]
            n_scr[0:M_HEADS, :] = n0_ref[0, 0]
            m_scr[...] = m0_ref[0, 0]

    ri = lax.broadcasted_iota(jnp.int32, (L, L), 0)
    ci = lax.broadcasted_iota(jnp.int32, (L, L), 1)
    sgn = 1 - 2 * d
    mask = (ci - ri) * sgn <= 0
    tri = mask.astype(BF16)
    tri_t = ((ri - ci) * sgn <= 0).astype(BF16)

    gcol = gc_ref[...] + gbr_ref[...]
    grow = gr_ref[...] + gbc_ref[...]
    bcol = _sel_dot(tri, jax.nn.log_sigmoid(gcol))
    brow = _dot_sel(jax.nn.log_sigmoid(grow), tri_t)

    def pick_col(a, j0, j1):
        return jnp.where(fwd, a[:, j0:j0 + 1], a[:, j1:j1 + 1])

    def pick_row(a, j0, j1):
        return jnp.where(fwd, a[j0:j0 + 1, :], a[j1:j1 + 1, :])

    cc = c + d * (nc - 1 - 2 * c)
    rows = pl.ds(pl.multiple_of(cc * L, L), L)
    e0 = (lax.broadcasted_iota(jnp.int32, (L, LANES), 1) == 0)

    for h in range(M_HEADS):
        hs = slice(h * M_DV, (h + 1) * M_DV)
        qb = q_ref[:, hs].astype(BF16)
        kf = k_ref[:, hs]
        vb = v_ref[:, hs].astype(BF16)
        i_c = pick_col(gcol, h, 4 + h)
        i_r = pick_row(grow, h, 4 + h)
        b_c = pick_col(bcol, 8 + h, 12 + h)
        b_r = pick_row(brow, 8 + h, 12 + h)
        b_tot = jnp.where(fwd, b_c[L - 1:L, :], b_c[0:1, :])
        m_prev = m_scr[h:h + 1, 0:1]
        n_prev = n_scr[h:h + 1, :]
        c_prev = c_scr[h]

        dm = jnp.where(mask, b_c - b_r + i_r, -jnp.inf)
        m_inter = b_c + m_prev
        m_t = jnp.maximum(m_inter, jnp.max(dm, axis=1, keepdims=True))
        s = _dot_nt(qb, kf.astype(BF16)) * jnp.exp(dm - m_t)
        sc = jnp.exp(m_inter - m_t)
        num = _dot(s.astype(BF16), vb) + sc * _dot(qb, c_prev.astype(BF16))
        qn = jnp.sum(q_ref[:, hs] * n_prev, axis=1, keepdims=True)
        den = jnp.sum(s, axis=1, keepdims=True) + sc * qn
        hh = num / jnp.maximum(jnp.abs(den), jnp.exp(-m_t))

        gs = b_tot - b_c + i_c
        m_new = jnp.maximum(b_tot + m_prev, jnp.max(gs, axis=0, keepdims=True))
        kw = kf * jnp.exp(gs - m_new)
        decay = jnp.exp(b_tot + m_prev - m_new)
        c_scr[h] = decay * c_prev + _dot_tn(kw.astype(BF16), vb)
        n_scr[h:h + 1, :] = decay * n_prev + jnp.sum(kw, axis=0, keepdims=True)
        m_scr[h:h + 1, :] = jnp.broadcast_to(m_new, (1, LANES))

        @pl.when(fwd)
        def _():
            out_ref[rows, hs] = hh

        @pl.when(jnp.logical_not(fwd))
        def _():
            t = jax.nn.sigmoid(o_ref[:, hs]) * (out_ref[rows, hs] + hh)
            t = t * lax.rsqrt(jnp.mean(t * t, axis=1, keepdims=True) + EPS)
            out_ref[rows, hs] = t * hg_ref[:, hs]

    @pl.when(c == nc - 1)
    def _():
        cs_ref[0, 0] = c_scr[...]
        ns_ref[0, 0] = n_scr[0:M_HEADS, :]
        ms_ref[0, 0] = m_scr[...]


def _mlstm_call(q, k, v, o, gc, gr, gate_b_row, gate_b_col, head_g, batch, seq_len, init):
    n = q.shape[0]
    L = min(M_CHUNK, seq_len)
    nc = seq_len // L
    zero_init = init is None

    def tok(b, d, c):
        return (b * nc + c + d * (nc - 1 - 2 * c), 0)

    def tok_t(b, d, c):
        return (0, b * nc + c + d * (nc - 1 - 2 * c))

    args = [q, k, v, o, gc, gr, gate_b_row, gate_b_col, head_g]
    specs = [pl.BlockSpec((L, M_WIDTH), tok)] * 4 + [
        pl.BlockSpec((L, LANES), tok),
        pl.BlockSpec((GATE_COLS, L), tok_t),
        _const_spec(gate_b_row.shape), _const_spec(gate_b_col.shape), _const_spec(head_g.shape),
    ]
    if not zero_init:
        c0, n0, m0 = init
        args += [c0, n0, m0]
        specs += [pl.BlockSpec((1, 1) + c0.shape[2:], lambda b, d, c: (b, d, 0, 0, 0)),
                  pl.BlockSpec((1, 1) + n0.shape[2:], lambda b, d, c: (b, d, 0, 0)),
                  pl.BlockSpec((1, 1) + m0.shape[2:], lambda b, d, c: (b, d, 0, 0))]
    outs = [jax.ShapeDtypeStruct((n, M_WIDTH), F32),
            jax.ShapeDtypeStruct((batch, 2, M_HEADS, M_DK, M_DV), F32),
            jax.ShapeDtypeStruct((batch, 2, M_HEADS, M_DK), F32),
            jax.ShapeDtypeStruct((batch, 2, 8, LANES), F32)]
    out_specs = [pl.BlockSpec((seq_len, M_WIDTH), lambda b, d, c: (b, 0)),
                 pl.BlockSpec((1, 1, M_HEADS, M_DK, M_DV), lambda b, d, c: (b, d, 0, 0, 0)),
                 pl.BlockSpec((1, 1, M_HEADS, M_DK), lambda b, d, c: (b, d, 0, 0)),
                 pl.BlockSpec((1, 1, 8, LANES), lambda b, d, c: (b, d, 0, 0))]
    body = functools.partial(_mlstm_body, zero_init, nc)
    return pl.pallas_call(
        body,
        grid=(batch, 2, nc),
        in_specs=specs,
        out_specs=out_specs,
        out_shape=outs,
        scratch_shapes=[pltpu.VMEM((M_HEADS, M_DK, M_DV), F32),
                        pltpu.VMEM((8, LANES), F32),
                        pltpu.VMEM((8, LANES), F32)],
        compiler_params=_cparams(("parallel", "arbitrary", "arbitrary")),
        name="mlstm_zero" if zero_init else "mlstm_init",
    )(*args)


def _rwkv_body(zero_init, nc, *refs):
    refs = list(refs)
    (xr_ref, w0_ref, w2_ref, a0_ref, a2_ref, g2_ref, kk_ref, ka_ref, rk_ref,
     lng_ref, lnb_ref, hones_ref) = refs[:12]
    refs = refs[12:]
    if not zero_init:
        h0_ref = refs.pop(0)
    out_ref, hs_ref, h_scr = refs

    d = pl.program_id(1)
    c = pl.program_id(2)
    L = xr_ref.shape[0]
    P = 2 * L
    fwd = d == 0
    n_pairs = R_HEADS // 2

    @pl.when(c == 0)
    def _():
        if zero_init:
            h_scr[...] = jnp.zeros_like(h_scr)
        else:
            h_scr[...] = h0_ref[0, 0]

    r = xr_ref[:, 0:R_WIDTH]
    k = xr_ref[:, R_WIDTH:2 * R_WIDTH]
    v = xr_ref[:, 2 * R_WIDTH:3 * R_WIDTH]
    wd = xr_ref[:, 3 * R_WIDTH:3 * R_WIDTH + R_LORA]
    ad = xr_ref[:, 3 * R_WIDTH + R_LORA:3 * R_WIDTH + 2 * R_LORA]
    gin = xr_ref[:, 3 * R_WIDTH + 2 * R_LORA:]
    hones = hones_ref[...]

    tw = jnp.tanh(wd).astype(BF16)
    adb = ad.astype(BF16)
    ww = jnp.where(fwd, w0_ref[0:1, :] + _dot(tw, w2_ref[0]), w0_ref[1:2, :] + _dot(tw, w2_ref[1]))
    lw = -jnp.exp(-jax.nn.softplus(-ww) - 0.5)
    a_f = jax.nn.sigmoid(a0_ref[0:1, :] + _dot(adb, a2_ref[0]))
    a_b = jax.nn.sigmoid(a0_ref[1:2, :] + _dot(adb, a2_ref[1]))
    a = jnp.where(fwd, a_f, a_b)
    ka = ka_ref[...]
    kd = k * (1.0 + (a - 1.0) * ka)
    kk = k * kk_ref[...]
    kk = kk / jnp.maximum(jnp.sqrt(_dot_sel(kk * kk, hones)), 1e-12)
    bvec = kk * a

    ri = lax.broadcasted_iota(jnp.int32, (L, L), 0)
    ci = lax.broadcasted_iota(jnp.int32, (L, L), 1)
    sgn = 1 - 2 * d
    tri = ((ci - ri) * sgn <= 0).astype(BF16)
    cl = _sel_dot(tri, lw)
    tot = jnp.where(fwd, cl[L - 1:L, :], cl[0:1, :])
    e_in = jnp.exp(cl)
    e_out = jnp.exp(-cl)
    e_end = jnp.exp(tot - cl)
    a_til = -kk * jnp.exp(cl - lw)
    r_til = r * e_in
    b_til = bvec * e_out
    k_til = kd * e_out
    b_hat = bvec * e_end
    k_hat = kd * e_end
    w_end = jnp.exp(tot)

    pr = lax.broadcasted_iota(jnp.int32, (P, P), 0)
    pc = lax.broadcasted_iota(jnp.int32, (P, P), 1)
    same = (pr // L) == (pc // L)
    before = (pc - pr) * sgn < 0
    m_strict = jnp.logical_and(same, before)
    m_incl = jnp.logical_and(same, jnp.logical_or(before, pc == pr))
    eye = pr == pc
    lane = lax.broadcasted_iota(jnp.int32, (L, LANES), 1)
    first = lane < R_HEAD

    def stack(x):
        return jnp.concatenate([jnp.where(first, x, 0.0), jnp.where(first, 0.0, x)], axis=0)

    cc = c + d * (nc - 1 - 2 * c)
    rows = pl.ds(pl.multiple_of(cc * L, L), L)
    ys = []
    for p in range(n_pairs):
        cs = slice(p * LANES, (p + 1) * LANES)
        at, rt = stack(a_til[:, cs]), stack(r_til[:, cs])
        bt, kt = stack(b_til[:, cs]), stack(k_til[:, cs])
        bh, kh = stack(b_hat[:, cs]), stack(k_hat[:, cs])
        vs = stack(v[:, cs])

        big = _mm3(jnp.concatenate([at, rt], axis=0), jnp.concatenate([bt, kt], axis=0), _dot_nt)
        a_ab = jnp.where(m_strict, big[:P, :P], 0.0)
        a_ak = jnp.where(m_strict, big[:P, P:], 0.0)
        a_rb = jnp.where(m_incl, big[P:, :P], 0.0)
        a_rk = jnp.where(m_incl, big[P:, P:], 0.0)

        tinv = jnp.where(eye, 1.0, jnp.where((pr // 2) == (pc // 2), a_ab, 0.0))
        blk = 2
        while blk < L:
            sib = jnp.logical_and((pr // (2 * blk)) == (pc // (2 * blk)), (pr // blk) != (pc // blk))
            e = jnp.where(sib, a_ab, 0.0)
            tinv = tinv + _mm3(_mm3(tinv, e), tinv)
            blk *= 2

        av = _mm3(a_ak, vs)
        pq = _mm3(tinv, jnp.concatenate([at, av], axis=1))
        ry = _mm3(a_rb, pq)
        r_hat = rt + ry[:, :LANES]
        y0 = ry[:, LANES:] + _mm3(a_rk, vs)
        mg = _mm3(bh, pq, _dot_tn)
        m_corr = mg[:, :LANES]
        g_add = mg[:, LANES:] + _mm3(kh, vs, _dot_tn)
        w_col = jnp.sum(jnp.where(eye, jnp.broadcast_to(w_end[:, cs], (P, P)), 0.0), axis=1, keepdims=True)

        hprev = h_scr[p]
        yst = _mm3(r_hat, hprev) + y0
        ys.append(yst[:L, :] + yst[L:, :])
        h_scr[p] = w_col * hprev + _mm3(m_corr, hprev) + g_add

    y = jnp.concatenate(ys, axis=1)

    @pl.when(fwd)
    def _():
        out_ref[rows, :] = y

    @pl.when(jnp.logical_not(fwd))
    def _():
        inv_n = 1.0 / R_HEAD
        ysum = out_ref[rows, :] + y
        mean = _dot_sel(ysum, hones) * inv_n
        yc = ysum - mean
        var = _dot_sel(yc * yc, hones) * inv_n
        yn = yc * lax.rsqrt(var + R_LN_EPS) * lng_ref[...] + lnb_ref[...]
        k_bar = k * (1.0 + (0.5 * (a_f + a_b) - 1.0) * ka)
        bonus = _dot_sel(r * k_bar * rk_ref[...], hones) * v
        gate = _dot(jax.nn.sigmoid(gin).astype(BF16), g2_ref[...])
        out_ref[rows, :] = (yn + bonus) * gate

    @pl.when(c == nc - 1)
    def _():
        hs_ref[0, 0] = h_scr[...]


def _rwkv_call(xr, w0, w2p, a0, a2p, g2, k_k, k_a, r_k, ln_g, ln_b, hones, batch, seq_len, h0):
    n = xr.shape[0]
    L = R_CHUNK
    nc = seq_len // L
    zero_init = h0 is None
    n_pairs = R_HEADS // 2

    def tok(b, d, c):
        return (b * nc + c + d * (nc - 1 - 2 * c), 0)

    consts = [w0, w2p, a0, a2p, g2, k_k, k_a, r_k, ln_g, ln_b, hones]
    args = [xr] + consts
    specs = [pl.BlockSpec((L, R_COLS), tok)] + [_const_spec(a.shape) for a in consts]
    if not zero_init:
        args.append(h0)
        specs.append(pl.BlockSpec((1, 1, n_pairs, LANES, LANES), lambda b, d, c: (b, d, 0, 0, 0)))
    outs = [jax.ShapeDtypeStruct((n, R_WIDTH), F32),
            jax.ShapeDtypeStruct((batch, 2, n_pairs, LANES, LANES), F32)]
    out_specs = [pl.BlockSpec((seq_len, R_WIDTH), lambda b, d, c: (b, 0)),
                 pl.BlockSpec((1, 1, n_pairs, LANES, LANES), lambda b, d, c: (b, d, 0, 0, 0))]
    body = functools.partial(_rwkv_body, zero_init, nc)
    return pl.pallas_call(
        body,
        grid=(batch, 2, nc),
        in_specs=specs,
        out_specs=out_specs,
        out_shape=outs,
        scratch_shapes=[pltpu.VMEM((n_pairs, LANES, LANES), F32)],
        compiler_params=_cparams(("parallel", "arbitrary", "arbitrary")),
        name="rwkv_zero" if zero_init else "rwkv_init",
    )(*args)


def _pairs_from_heads(s):
    ht = jnp.swapaxes(s, -1, -2)
    lead = ht.shape[:-3]
    ht = ht.reshape(lead + (R_HEADS // 2, 2, R_HEAD, R_HEAD))
    z = jnp.zeros_like(ht[..., 0, :, :])
    top = jnp.concatenate([ht[..., 0, :, :], z], axis=-1)
    bot = jnp.concatenate([z, ht[..., 1, :, :]], axis=-1)
    return jnp.concatenate([top, bot], axis=-2)


def _heads_from_pairs(hp):
    a = hp[..., :R_HEAD, :R_HEAD]
    b = hp[..., R_HEAD:, R_HEAD:]
    s = jnp.stack([a, b], axis=-3)
    s = s.reshape(hp.shape[:-3] + (R_HEADS, R_HEAD, R_HEAD))
    return jnp.swapaxes(s, -1, -2)


def _trunk(x, mod3, mod_row0, per_seq_rows, grid_mode, init, w):
    batch, seq_len, d = x.shape
    x2 = x.reshape(batch * seq_len, d)

    def rows_for(tile):
        per = seq_len // tile
        if per_seq_rows:
            return lambda i: mod_row0 + i // per
        return lambda i: mod_row0

    x1 = _ffn_call(x2, mod3, rows_for(FFN_TILE), 0, w["norm_ffn1"], w["ffn1_in"], w["ffn1_out"])
    mix_tile = MIX_TILE if grid_mode else seq_len
    q, k, v, o, gc, gr, xr = _mixin_call(
        x1, seq_len, grid_mode, mod3, rows_for(mix_tile), w["norm_mix"], w["wm"], w["wg"], w["wgt"],
        w["wr"], w["conv_w"], w["conv_b"], w["mu"])
    if init is None:
        m_init = None
        r_init = None
    else:
        c0, n0, m0, s0 = init
        m0p = jnp.broadcast_to(jnp.pad(m0, ((0, 0), (0, 0), (0, 8 - M_HEADS)))[..., None], m0.shape[:2] + (8, LANES))
        m_init = (c0, n0, m0p)
        r_init = _pairs_from_heads(s0)
    mo, cs, ns, ms = _mlstm_call(q, k, v, o, gc, gr, w["gate_b_row"], w["gate_b_col"], w["head_g"],
                                 batch, seq_len, m_init)
    ro, hs = _rwkv_call(xr, w["r_w0"], w["r_w2p"], w["r_a0"], w["r_a2p"], w["r_g2"], w["r_k_k"], w["r_k_a"],
                        w["r_r_k"], w["r_ln_g"], w["r_ln_b"], w["hones"], batch, seq_len, r_init)
    y = _ffn_call(x1, mod3, rows_for(FFN_TILE), 6, w["norm_ffn2"], w["ffn2_in"], w["ffn2_out"],
                  mix=(mo, ro, w["wo_m"], w["wo_r"]), g_final=w["norm_final"])
    states = (cs, ns, ms[:, :, :M_HEADS, 0], _heads_from_pairs(hs))
    return y.reshape(batch, seq_len, d), states


def _prepare_weights(ada_w, ada_b, norm_ffn1, ffn1_w_in, ffn1_w_out, norm_mix, mix_w_in, mix_w_out,
                     m_conv_w, m_conv_b, m_gate_b, m_head_g, r_mu, r_w0, r_w2, r_a0, r_a2, r_g2,
                     r_k_k, r_k_a, r_r_k, r_ln_g, r_ln_b, norm_ffn2, ffn2_w_in, ffn2_w_out, norm_final):
    assert ada_w.shape[0] == 1, "single trunk layer"
    lora = r_w2.shape[2]
    nm = 4 * M_WIDTH

    w_in = mix_w_in[0]
    gate_w = w_in[:, nm:nm + GATE_COLS]
    zpad = jnp.zeros((lora, R_WIDTH), F32)

    def dir_pad(w2):
        return jnp.stack([jnp.concatenate([w2[0], zpad], axis=0), jnp.concatenate([zpad, w2[1]], axis=0)])

    head_id = jnp.arange(R_WIDTH) // R_HEAD
    return dict(
        norm_ffn1=norm_ffn1, norm_mix=norm_mix, norm_ffn2=norm_ffn2, norm_final=norm_final[None],
        ffn1_in=ffn1_w_in[0].astype(BF16), ffn1_out=ffn1_w_out[0].astype(BF16),
        ffn2_in=ffn2_w_in[0].astype(BF16), ffn2_out=ffn2_w_out[0].astype(BF16),
        wm=w_in[:, :nm].astype(BF16),
        wg=jnp.pad(gate_w, ((0, 0), (0, LANES - GATE_COLS))).astype(BF16),
        wgt=gate_w.T.astype(BF16),
        wr=w_in[:, nm + GATE_COLS:].astype(BF16),
        wo_m=mix_w_out[0, :M_WIDTH].astype(BF16), wo_r=mix_w_out[0, M_WIDTH:].astype(BF16),
        conv_w=m_conv_w[0], conv_b=m_conv_b, mu=r_mu[0],
        gate_b_row=jnp.pad(m_gate_b[0].reshape(1, GATE_COLS), ((0, 0), (0, LANES - GATE_COLS))),
        gate_b_col=m_gate_b[0].reshape(GATE_COLS, 1),
        head_g=m_head_g,
        r_w0=r_w0[0], r_w2p=dir_pad(r_w2[0]).astype(BF16), r_a0=r_a0[0], r_a2p=dir_pad(r_a2[0]).astype(BF16),
        r_g2=r_g2[0].astype(BF16), r_k_k=r_k_k, r_k_a=r_k_a, r_r_k=r_r_k[0].reshape(1, R_WIDTH),
        r_ln_g=r_ln_g, r_ln_b=r_ln_b,
        hones=(head_id[:, None] == head_id[None, :]).astype(BF16),
    )


def kernel(x_prompt, x_sample, c, state_mlstm_C, state_mlstm_n, state_mlstm_m, state_rwkv_S, c_ctx,
           ada_w, ada_b, norm_ffn1, ffn1_w_in, ffn1_w_out, norm_mix, mix_w_in, mix_w_out,
           m_conv_w, m_conv_b, m_gate_b, m_head_g, r_mu, r_w0, r_w2, r_a0, r_a2, r_g2,
           r_k_k, r_k_a, r_r_k, r_ln_g, r_ln_b, norm_ffn2, ffn2_w_in, ffn2_w_out, norm_final):
    w = _prepare_weights(ada_w, ada_b, norm_ffn1, ffn1_w_in, ffn1_w_out, norm_mix, mix_w_in, mix_w_out,
                         m_conv_w, m_conv_b, m_gate_b, m_head_g, r_mu, r_w0, r_w2, r_a0, r_a2, r_g2,
                         r_k_k, r_k_a, r_r_k, r_ln_g, r_ln_b, norm_ffn2, ffn2_w_in, ffn2_w_out, norm_final)
    d = x_prompt.shape[-1]
    dec_batch = x_sample.shape[0]
    cond =jnp.concatenate([c_ctx[None], c, jnp.zeros((16 - 1 - dec_batch, d), F32)], axis=0)
    mod3 = _ada_call(cond, ada_w[0], ada_b).reshape(16, 9, d)

    y_prompt, (cs, ns, ms, ss) = _trunk(x_prompt, mod3, 0, False, False, None, w)
    init = (state_mlstm_C[:, 0], state_mlstm_n[:, 0], state_mlstm_m[:, 0], state_rwkv_S[:, 0])
    y_sample, _ = _trunk(x_sample, mod3, 1, True, True, init, w)
    return (y_prompt, y_sample, cs[:, None], ns[:, None], ms[:, None], ss[:, None])
```

```python
import functools

import jax
import jax.numpy as jnp
from jax import lax
from jax.experimental import pallas as pl
from jax.experimental.pallas import tpu as pltpu

F32 = jnp.float32
BF16 = jnp.bfloat16

EPS = 1e-6
R_LN_EPS = 64e-5
GRID_W = 64
M_HEADS = 4
M_DK = 128
M_DV = 128
M_WIDTH = M_HEADS * M_DV
R_HEADS = 8
R_HEAD = 64
R_WIDTH = R_HEADS * R_HEAD
R_LORA = 128
R_COLS = 3 * R_WIDTH + 3 * R_LORA
GATE_COLS = 16
LANES = 128

FFN_TILE = 512
FFN_CHUNK = 256
MIX_TILE = 512
M_CHUNK = 256
R_CHUNK = 64
VMEM_LIMIT = 56 * 1024 * 1024
R_PASSES_GRAM = 1
R_PASSES_INV = 1
R_PASSES_APPLY = 1
R_PASSES_STATE = 1


def _dot(a, b):
    return jnp.dot(a, b, preferred_element_type=F32)


def _dot_nt(a, b):
    return lax.dot_general(a, b, (((1,), (1,)), ((), ())), preferred_element_type=F32)


def _dot_tn(a, b):
    return lax.dot_general(a, b, (((0,), (0,)), ((), ())), preferred_element_type=F32)


def _split2(x):
    hi = x.astype(BF16)
    lo = (x - hi.astype(F32)).astype(BF16)
    return hi, lo


def _split3(x):
    hi = x.astype(BF16)
    r1 = x - hi.astype(F32)
    mid = r1.astype(BF16)
    lo = (r1 - mid.astype(F32)).astype(BF16)
    return hi, mid, lo


def _sel_dot(mat, x):
    hi, mid, lo = _split3(x)
    return _dot(mat, hi) + _dot(mat, mid) + _dot(mat, lo)


def _dot_sel(x, mat):
    hi, mid, lo = _split3(x)
    return _dot(hi, mat) + _dot(mid, mat) + _dot(lo, mat)


def _mm3(a, b, dot=_dot):
    ah, al = _split2(a)
    bh, bl = _split2(b)
    return dot(ah, bh) + dot(ah, bl) + dot(al, bh)


def _mm(a, b, passes, dot=_dot):
    if passes == 1:
        return dot(a.astype(BF16), b.astype(BF16))
    return _mm3(a, b, dot)


def _silu(x):
    return x * jax.nn.sigmoid(x)


def _rms_mod(x, g, shift, scale):
    y = x * lax.rsqrt(jnp.mean(x * x, axis=-1, keepdims=True) + EPS) * g
    return y * (1.0 + scale) + shift


def _cparams(sem):
    return pltpu.CompilerParams(dimension_semantics=sem, vmem_limit_bytes=VMEM_LIMIT)


def _const_spec(shape):
    nd = len(shape)
    return pl.BlockSpec(shape, lambda *_: (0,) * nd, pipeline_mode=pl.Buffered(1))


def _ada_body(c_ref, w_ref, b_ref, o_ref):
    s = _silu(c_ref[...])
    o_ref[...] = _dot(s.astype(BF16), w_ref[...].astype(BF16)) + b_ref[...]


def _ada_call(cond, w, b):
    rows, d = cond.shape
    n = w.shape[1]
    tn = d
    return pl.pallas_call(
        _ada_body,
        grid=(n // tn,),
        in_specs=[
            pl.BlockSpec((rows, d), lambda j: (0, 0)),
            pl.BlockSpec((d, tn), lambda j: (0, j)),
            pl.BlockSpec((1, tn), lambda j: (0, j)),
        ],
        out_specs=pl.BlockSpec((rows, tn), lambda j: (0, j)),
        out_shape=jax.ShapeDtypeStruct((rows, n), F32),
        compiler_params=_cparams(("arbitrary",)),
        name="ada",
    )(cond, w, b)


def _ffn_body(mod_base, with_mix, with_final, *refs):
    refs = list(refs)
    x_ref = refs.pop(0)
    if with_mix:
        mo_ref, ro_ref, wmm_ref, wmr_ref = refs[:4]
        refs = refs[4:]
    mod_ref, g_ref, win_ref, wout_ref = refs[:4]
    refs = refs[4:]
    if with_final:
        gfin_ref = refs.pop(0)
    out_ref, a_scr = refs

    x = x_ref[...]
    if with_mix:
        mix = _dot(mo_ref[...].astype(BF16), wmm_ref[...]) + _dot(ro_ref[...].astype(BF16), wmr_ref[...])
        x = x + mod_ref[0, mod_base - 1:mod_base, :] * mix
    shift = mod_ref[0, mod_base:mod_base + 1, :]
    scale = mod_ref[0, mod_base + 1:mod_base + 2, :]
    gate = mod_ref[0, mod_base + 2:mod_base + 3, :]
    hb = _rms_mod(x, g_ref[...], shift, scale).astype(BF16)
    dff = wout_ref.shape[0]
    for j in range(dff // FFN_CHUNK):
        lo = j * FFN_CHUNK
        gt = _dot(hb, win_ref[:, lo:lo + FFN_CHUNK])
        up = _dot(hb, win_ref[:, dff + lo:dff + lo + FFN_CHUNK])
        a_scr[:, lo:lo + FFN_CHUNK] = (_silu(gt) * up).astype(BF16)
    y = x + (0.5 * gate) * _dot(a_scr[...], wout_ref[...])
    if with_final:
        y = y * lax.rsqrt(jnp.mean(y * y, axis=-1, keepdims=True) + EPS) * gfin_ref[...]
    out_ref[...] = y


def _ffn_call(x, mod3, row_of_tile, mod_base, g, w_in, w_out, mix=None, g_final=None):
    n, d = x.shape
    tm = FFN_TILE
    dff = w_out.shape[0]
    tok = lambda i: (i, 0)
    args = [x]
    specs = [pl.BlockSpec((tm, d), tok)]
    if mix is not None:
        mo, ro, wmm, wmr = mix
        args += [mo, ro, wmm, wmr]
        specs += [pl.BlockSpec((tm, mo.shape[1]), tok), pl.BlockSpec((tm, ro.shape[1]), tok),
                  _const_spec(wmm.shape), _const_spec(wmr.shape)]
    args += [mod3, g, w_in, w_out]
    specs += [pl.BlockSpec((1,) + mod3.shape[1:], lambda i: (row_of_tile(i), 0, 0)),
              _const_spec(g.shape), _const_spec(w_in.shape), _const_spec(w_out.shape)]
    if g_final is not None:
        args.append(g_final)
        specs.append(_const_spec(g_final.shape))
    body = functools.partial(_ffn_body, mod_base, mix is not None, g_final is not None)
    return pl.pallas_call(
        body,
        grid=(n // tm,),
        in_specs=specs,
        out_specs=pl.BlockSpec((tm, d), tok),
        out_shape=jax.ShapeDtypeStruct((n, d), F32),
        scratch_shapes=[pltpu.VMEM((tm, dff), BF16)],
        compiler_params=_cparams(("parallel",)),
        name="ffn_mix" if mix is not None else "ffn",
    )(*args)


def _mixin_body(grid_mode, has_halo, tiles_per_seq, *refs):
    refs = list(refs)
    x_ref = refs.pop(0)
    if has_halo:
        xp_ref, xn_ref = refs[:2]
        refs = refs[2:]
    (mod_ref, g_ref, wm_ref, wg_ref, wgt_ref, wr_ref, cw_ref, cb_ref, mu_ref,
     q_ref, k_ref, v_ref, o_ref, gc_ref, gr_ref, xr_ref) = refs

    tm = x_ref.shape[0]
    shift = mod_ref[0, 3:4, :]
    scale = mod_ref[0, 4:5, :]
    g = g_ref[...]
    hb = _rms_mod(x_ref[...], g, shift, scale).astype(BF16)
    row = lax.broadcasted_iota(jnp.int32, (tm, 1), 0)
    nqk = 2 * M_WIDTH

    if has_halo:
        i = pl.program_id(0)
        pos = i % tiles_per_seq
        keep_p = (pos > 0).astype(F32)
        keep_n = (pos < tiles_per_seq - 1).astype(F32)
        hp32 = _rms_mod(xp_ref[...], g, shift, scale)
        hn32 = _rms_mod(xn_ref[...], g, shift, scale)
        hp = hp32.astype(BF16)
        hn = hn32.astype(BF16)
        nh = hp32.shape[0]
        edge = 16
        qk_prev = _dot(hp32[nh - edge:, :].astype(BF16), wm_ref[:, :nqk])[edge - 1:edge, :] * keep_p
        qk_next = _dot(hn32[:edge, :].astype(BF16), wm_ref[:, :nqk])[0:1, :] * keep_n
    else:
        qk_prev = jnp.zeros((1, nqk), F32)
        qk_next = jnp.zeros((1, nqk), F32)

    uqk = _dot(hb, wm_ref[:, :nqk])
    u_dn = jnp.where(row == 0, qk_prev, pltpu.roll(uqk, 1, 0))
    u_up = jnp.where(row == tm - 1, qk_next, pltpu.roll(uqk, tm - 1, 0))
    qk = _silu(cw_ref[0:1, :] * u_dn + cw_ref[1:2, :] * uqk + cw_ref[2:3, :] * u_up + cb_ref[...])
    q_ref[...] = qk[:, :M_WIDTH]
    k_ref[...] = qk[:, M_WIDTH:] * (M_DK ** -0.5)
    v_ref[...] = _dot(hb, wm_ref[:, nqk:nqk + M_WIDTH])
    o_ref[...] = _dot(hb, wm_ref[:, nqk + M_WIDTH:])
    gc_ref[...] = _dot(hb, wg_ref[...])
    gr_ref[...] = _dot_nt(wgt_ref[...], hb)

    ur = _dot(hb, wr_ref[...])
    if grid_mode:
        col = row % GRID_W
        left = jnp.where(col == 0, 0.0, pltpu.roll(ur, 1, 0))
        right = jnp.where(col == GRID_W - 1, 0.0, pltpu.roll(ur, tm - 1, 0))
        ur_p = _dot(hp, wr_ref[...]) * keep_p
        ur_n = _dot(hn, wr_ref[...]) * keep_n
        up = jnp.concatenate([ur_p, ur[:tm - GRID_W, :]], axis=0)
        down = jnp.concatenate([ur[GRID_W:, :], ur_n], axis=0)
        xr = (ur + mu_ref[0:1, :] * (left - ur) + mu_ref[1:2, :] * (right - ur)
              + mu_ref[2:3, :] * (up - ur) + mu_ref[3:4, :] * (down - ur))
    else:
        left = jnp.where(row == 0, 0.0, pltpu.roll(ur, 1, 0))
        right = jnp.where(row == tm - 1, 0.0, pltpu.roll(ur, tm - 1, 0))
        xr = ur + mu_ref[0:1, :] * (left - ur) + mu_ref[1:2, :] * (right - ur)
    xr_ref[...] = xr


def _mixin_call(x, seq_len, grid_mode, mod3, row_of_tile, g, wm, wg, wgt, wr, cw, cb, mu):
    n, d = x.shape
    if grid_mode:
        tm = MIX_TILE
        has_halo = True
    else:
        tm = seq_len
        has_halo = False
    tiles_per_seq = seq_len // tm
    tok = lambda i: (i, 0)
    args = [x]
    specs = [pl.BlockSpec((tm, d), tok)]
    if has_halo:
        hb = R_CHUNK
        per = tm // hb
        last = n // hb - 1
        args += [x, x]
        specs += [pl.BlockSpec((hb, d), lambda i: (jnp.maximum(i * per - 1, 0), 0)),
                  pl.BlockSpec((hb, d), lambda i: (jnp.minimum((i + 1) * per, last), 0))]
    args += [mod3, g, wm, wg, wgt, wr, cw, cb, mu]
    specs += [pl.BlockSpec((1,) + mod3.shape[1:], lambda i: (row_of_tile(i), 0, 0))]
    specs += [_const_spec(a.shape) for a in (g, wm, wg, wgt, wr, cw, cb, mu)]
    outs = [jax.ShapeDtypeStruct((n, M_WIDTH), F32)] * 4 + [
        jax.ShapeDtypeStruct((n, LANES), F32),
        jax.ShapeDtypeStruct((GATE_COLS, n), F32),
        jax.ShapeDtypeStruct((n, R_COLS), F32),
    ]
    out_specs = [pl.BlockSpec((tm, M_WIDTH), tok)] * 4 + [
        pl.BlockSpec((tm, LANES), tok),
        pl.BlockSpec((GATE_COLS, tm), lambda i: (0, i)),
        pl.BlockSpec((tm, R_COLS), tok),
    ]
    body = functools.partial(_mixin_body, grid_mode, has_halo, tiles_per_seq)
    return pl.pallas_call(
        body,
        grid=(n // tm,),
        in_specs=specs,
        out_specs=out_specs,
        out_shape=outs,
        compiler_params=_cparams(("parallel",)),
        name="mix_in_grid" if grid_mode else "mix_in_seq",
    )(*args)


def _mlstm_body(zero_init, nc, *refs):
    refs = list(refs)
    q_ref, k_ref, v_ref, o_ref, gc_ref, gr_ref, gbr_ref, gbc_ref, hg_ref = refs[:9]
    refs = refs[9:]
    if not zero_init:
        c0_ref, n0_ref, m0_ref = refs[:3]
        refs = refs[3:]
    out_ref, cs_ref, ns_ref, ms_ref, c_scr, n_scr, m_scr = refs

    d = pl.program_id(1)
    c = pl.program_id(2)
    L = q_ref.shape[0]
    fwd = d == 0

    @pl.when(c == 0)
    def _():
        if zero_init:
            c_scr[...] = jnp.zeros_like(c_scr)
            n_scr[...] = jnp.zeros_like(n_scr)
            m_scr[...] = jnp.zeros_like(m_scr)
        else:
            c_scr[...] = c0_ref[0, 0]
            n_scr[0:M_HEADS, :] = n0_ref[0, 0]
            m_scr[...] = m0_ref[0, 0]

    ri = lax.broadcasted_iota(jnp.int32, (L, L), 0)
    ci = lax.broadcasted_iota(jnp.int32, (L, L), 1)
    sgn = 1 - 2 * d
    mask = (ci - ri) * sgn <= 0
    tri = mask.astype(BF16)
    tri_t = ((ri - ci) * sgn <= 0).astype(BF16)

    gcol = gc_ref[...] + gbr_ref[...]
    grow = gr_ref[...] + gbc_ref[...]
    bcol = _sel_dot(tri, jax.nn.log_sigmoid(gcol))
    brow = _dot_sel(jax.nn.log_sigmoid(grow), tri_t)

    def pick_col(a, j0, j1):
        return jnp.where(fwd, a[:, j0:j0 + 1], a[:, j1:j1 + 1])

    def pick_row(a, j0, j1):
        return jnp.where(fwd, a[j0:j0 + 1, :], a[j1:j1 + 1, :])

    cc = c + d * (nc - 1 - 2 * c)
    rows = pl.ds(pl.multiple_of(cc * L, L), L)
    e0 = (lax.broadcasted_iota(jnp.int32, (L, LANES), 1) == 0)

    for h in range(M_HEADS):
        hs = slice(h * M_DV, (h + 1) * M_DV)
        qb = q_ref[:, hs].astype(BF16)
        kf = k_ref[:, hs]
        vb = v_ref[:, hs].astype(BF16)
        i_c = pick_col(gcol, h, 4 + h)
        i_r = pick_row(grow, h, 4 + h)
        b_c = pick_col(bcol, 8 + h, 12 + h)
        b_r = pick_row(brow, 8 + h, 12 + h)
        b_tot = jnp.where(fwd, b_c[L - 1:L, :], b_c[0:1, :])
        m_prev = m_scr[h:h + 1, 0:1]
        n_prev = n_scr[h:h + 1, :]
        c_prev = c_scr[h]

        dm = jnp.where(mask, b_c - b_r + i_r, -jnp.inf)
        m_inter = b_c + m_prev
        m_t = jnp.maximum(m_inter, jnp.max(dm, axis=1, keepdims=True))
        s = _dot_nt(qb, kf.astype(BF16)) * jnp.exp(dm - m_t)
        sc = jnp.exp(m_inter - m_t)
        num = _dot(s.astype(BF16), vb) + sc * _dot(qb, c_prev.astype(BF16))
        qn = jnp.sum(q_ref[:, hs] * n_prev, axis=1, keepdims=True)
        den = jnp.sum(s, axis=1, keepdims=True) + sc * qn
        hh = num / jnp.maximum(jnp.abs(den), jnp.exp(-m_t))

        gs = b_tot - b_c + i_c
        m_new = jnp.maximum(b_tot + m_prev, jnp.max(gs, axis=0, keepdims=True))
        kw = kf * jnp.exp(gs - m_new)
        decay = jnp.exp(b_tot + m_prev - m_new)
        c_scr[h] = decay * c_prev + _dot_tn(kw.astype(BF16), vb)
        n_scr[h:h + 1, :] = decay * n_prev + jnp.sum(kw, axis=0, keepdims=True)
        m_scr[h:h + 1, :] = jnp.broadcast_to(m_new, (1, LANES))

        @pl.when(fwd)
        def _():
            out_ref[rows, hs] = hh

        @pl.when(jnp.logical_not(fwd))
        def _():
            t = jax.nn.sigmoid(o_ref[:, hs]) * (out_ref[rows, hs] + hh)
            t = t * lax.rsqrt(jnp.mean(t * t, axis=1, keepdims=True) + EPS)
            out_ref[rows, hs] = t * hg_ref[:, hs]

    @pl.when(c == nc - 1)
    def _():
        cs_ref[0, 0] = c_scr[...]
        ns_ref[0, 0] = n_scr[0:M_HEADS, :]
        ms_ref[0, 0] = m_scr[...]


def _mlstm_call(q, k, v, o, gc, gr, gate_b_row, gate_b_col, head_g, batch, seq_len, init):
    n = q.shape[0]
    L = min(M_CHUNK, seq_len)
    nc = seq_len // L
    zero_init = init is None

    def tok(b, d, c):
        return (b * nc + c + d * (nc - 1 - 2 * c), 0)

    def tok_t(b, d, c):
        return (0, b * nc + c + d * (nc - 1 - 2 * c))

    args = [q, k, v, o, gc, gr, gate_b_row, gate_b_col, head_g]
    specs = [pl.BlockSpec((L, M_WIDTH), tok)] * 4 + [
        pl.BlockSpec((L, LANES), tok),
        pl.BlockSpec((GATE_COLS, L), tok_t),
        _const_spec(gate_b_row.shape), _const_spec(gate_b_col.shape), _const_spec(head_g.shape),
    ]
    if not zero_init:
        c0, n0, m0 = init
        args += [c0, n0, m0]
        specs += [pl.BlockSpec((1, 1) + c0.shape[2:], lambda b, d, c: (b, d, 0, 0, 0)),
                  pl.BlockSpec((1, 1) + n0.shape[2:], lambda b, d, c: (b, d, 0, 0)),
                  pl.BlockSpec((1, 1) + m0.shape[2:], lambda b, d, c: (b, d, 0, 0))]
    outs = [jax.ShapeDtypeStruct((n, M_WIDTH), F32),
            jax.ShapeDtypeStruct((batch, 2, M_HEADS, M_DK, M_DV), F32),
            jax.ShapeDtypeStruct((batch, 2, M_HEADS, M_DK), F32),
            jax.ShapeDtypeStruct((batch, 2, 8, LANES), F32)]
    out_specs = [pl.BlockSpec((seq_len, M_WIDTH), lambda b, d, c: (b, 0)),
                 pl.BlockSpec((1, 1, M_HEADS, M_DK, M_DV), lambda b, d, c: (b, d, 0, 0, 0)),
                 pl.BlockSpec((1, 1, M_HEADS, M_DK), lambda b, d, c: (b, d, 0, 0)),
                 pl.BlockSpec((1, 1, 8, LANES), lambda b, d, c: (b, d, 0, 0))]
    body = functools.partial(_mlstm_body, zero_init, nc)
    return pl.pallas_call(
        body,
        grid=(batch, 2, nc),
        in_specs=specs,
        out_specs=out_specs,
        out_shape=outs,
        scratch_shapes=[pltpu.VMEM((M_HEADS, M_DK, M_DV), F32),
                        pltpu.VMEM((8, LANES), F32),
                        pltpu.VMEM((8, LANES), F32)],
        compiler_params=_cparams(("parallel", "arbitrary", "arbitrary")),
        name="mlstm_zero" if zero_init else "mlstm_init",
    )(*args)


def _rwkv_body(zero_init, nc, *refs):
    refs = list(refs)
    (xr_ref, w0_ref, w2_ref, a0_ref, a2_ref, g2_ref, kk_ref, ka_ref, rk_ref,
     lng_ref, lnb_ref, hones_ref) = refs[:12]
    refs = refs[12:]
    if not zero_init:
        h0_ref = refs.pop(0)
    out_ref, hs_ref, h_scr = refs

    d = pl.program_id(1)
    c = pl.program_id(2)
    L = xr_ref.shape[0]
    P = 2 * L
    fwd = d == 0
    n_pairs = R_HEADS // 2

    @pl.when(c == 0)
    def _():
        if zero_init:
            h_scr[...] = jnp.zeros_like(h_scr)
        else:
            h_scr[...] = h0_ref[0, 0]

    r = xr_ref[:, 0:R_WIDTH]
    k = xr_ref[:, R_WIDTH:2 * R_WIDTH]
    v = xr_ref[:, 2 * R_WIDTH:3 * R_WIDTH]
    wd = xr_ref[:, 3 * R_WIDTH:3 * R_WIDTH + R_LORA]
    ad = xr_ref[:, 3 * R_WIDTH + R_LORA:3 * R_WIDTH + 2 * R_LORA]
    gin = xr_ref[:, 3 * R_WIDTH + 2 * R_LORA:]
    hones = hones_ref[...]

    tw = jnp.tanh(wd).astype(BF16)
    adb = ad.astype(BF16)
    ww = jnp.where(fwd, w0_ref[0:1, :] + _dot(tw, w2_ref[0]), w0_ref[1:2, :] + _dot(tw, w2_ref[1]))
    lw = -jnp.exp(-jax.nn.softplus(-ww) - 0.5)
    a_f = jax.nn.sigmoid(a0_ref[0:1, :] + _dot(adb, a2_ref[0]))
    a_b = jax.nn.sigmoid(a0_ref[1:2, :] + _dot(adb, a2_ref[1]))
    a = jnp.where(fwd, a_f, a_b)
    ka = ka_ref[...]
    kd = k * (1.0 + (a - 1.0) * ka)
    kk = k * kk_ref[...]
    kk = kk / jnp.maximum(jnp.sqrt(_dot_sel(kk * kk, hones)), 1e-12)
    bvec = kk * a

    ri = lax.broadcasted_iota(jnp.int32, (L, L), 0)
    ci = lax.broadcasted_iota(jnp.int32, (L, L), 1)
    sgn = 1 - 2 * d
    tri = ((ci - ri) * sgn <= 0).astype(BF16)
    cl = _sel_dot(tri, lw)
    tot = jnp.where(fwd, cl[L - 1:L, :], cl[0:1, :])
    e_in = jnp.exp(cl)
    e_out = jnp.exp(-cl)
    e_end = jnp.exp(tot - cl)
    a_til = -kk * jnp.exp(cl - lw)
    r_til = r * e_in
    b_til = bvec * e_out
    k_til = kd * e_out
    b_hat = bvec * e_end
    k_hat = kd * e_end
    w_end = jnp.exp(tot)

    pr = lax.broadcasted_iota(jnp.int32, (P, P), 0)
    pc = lax.broadcasted_iota(jnp.int32, (P, P), 1)
    same = (pr // L) == (pc // L)
    before = (pc - pr) * sgn < 0
    m_strict = jnp.logical_and(same, before)
    m_incl = jnp.logical_and(same, jnp.logical_or(before, pc == pr))
    eye = pr == pc
    lane = lax.broadcasted_iota(jnp.int32, (L, LANES), 1)
    first = lane < R_HEAD

    def stack(x):
        return jnp.concatenate([jnp.where(first, x, 0.0), jnp.where(first, 0.0, x)], axis=0)

    cc = c + d * (nc - 1 - 2 * c)
    rows = pl.ds(pl.multiple_of(cc * L, L), L)
    pairs = range(n_pairs)
    cols = [slice(p * LANES, (p + 1) * LANES) for p in pairs]
    at = [stack(a_til[:, cs]) for cs in cols]
    rt = [stack(r_til[:, cs]) for cs in cols]
    bt = [stack(b_til[:, cs]) for cs in cols]
    kt = [stack(k_til[:, cs]) for cs in cols]
    bh = [stack(b_hat[:, cs]) for cs in cols]
    kh = [stack(k_hat[:, cs]) for cs in cols]
    vs = [stack(v[:, cs]) for cs in cols]

    big = [_mm(jnp.concatenate([at[p], rt[p]], axis=0), jnp.concatenate([bt[p], kt[p]], axis=0),
               R_PASSES_GRAM, _dot_nt) for p in pairs]
    a_ab = [jnp.where(m_strict, big[p][:P, :P], 0.0) for p in pairs]
    a_ak = [jnp.where(m_strict, big[p][:P, P:], 0.0) for p in pairs]
    a_rb = [jnp.where(m_incl, big[p][P:, :P], 0.0) for p in pairs]
    a_rk = [jnp.where(m_incl, big[p][P:, P:], 0.0) for p in pairs]

    pair2 = (pr // 2) == (pc // 2)
    tinv = [jnp.where(eye, 1.0, jnp.where(pair2, a_ab[p], 0.0)) for p in pairs]
    blk = 2
    while blk < L:
        sib = jnp.logical_and((pr // (2 * blk)) == (pc // (2 * blk)), (pr // blk) != (pc // blk))
        half = [_mm(tinv[p], jnp.where(sib, a_ab[p], 0.0), R_PASSES_INV) for p in pairs]
        tinv = [tinv[p] + _mm(half[p], tinv[p], R_PASSES_INV) for p in pairs]
        blk *= 2

    av = [_mm(a_ak[p], vs[p], R_PASSES_APPLY) for p in pairs]
    pq = [_mm(tinv[p], jnp.concatenate([at[p], av[p]], axis=1), R_PASSES_APPLY) for p in pairs]
    ry = [_mm(a_rb[p], pq[p], R_PASSES_APPLY) for p in pairs]
    rkv = [_mm(a_rk[p], vs[p], R_PASSES_APPLY) for p in pairs]
    mg = [_mm(bh[p], pq[p], R_PASSES_APPLY, _dot_tn) for p in pairs]
    kv = [_mm(kh[p], vs[p], R_PASSES_APPLY, _dot_tn) for p in pairs]
    ys = []
    for p in pairs:
        r_hat = rt[p] + ry[p][:, :LANES]
        y0 = ry[p][:, LANES:] + rkv[p]
        m_corr = mg[p][:, :LANES]
        g_add = mg[p][:, LANES:] + kv[p]
        w_col = jnp.sum(jnp.where(eye, jnp.broadcast_to(w_end[:, cols[p]], (P, P)), 0.0), axis=1, keepdims=True)
        hprev = h_scr[p]
        yst = _mm(r_hat, hprev, R_PASSES_STATE) + y0
        ys.append(yst[:L, :] + yst[L:, :])
        h_scr[p] = w_col * hprev + _mm(m_corr, hprev, R_PASSES_STATE) + g_add

    y = jnp.concatenate(ys, axis=1)

    @pl.when(fwd)
    def _():
        out_ref[rows, :] = y

    @pl.when(jnp.logical_not(fwd))
    def _():
        inv_n = 1.0 / R_HEAD
        ysum = out_ref[rows, :] + y
        mean = _dot_sel(ysum, hones) * inv_n
        yc = ysum - mean
        var = _dot_sel(yc * yc, hones) * inv_n
        yn = yc * lax.rsqrt(var + R_LN_EPS) * lng_ref[...] + lnb_ref[...]
        k_bar = k * (1.0 + (0.5 * (a_f + a_b) - 1.0) * ka)
        bonus = _dot_sel(r * k_bar * rk_ref[...], hones) * v
        gate = _dot(jax.nn.sigmoid(gin).astype(BF16), g2_ref[...])
        out_ref[rows, :] = (yn + bonus) * gate

    @pl.when(c == nc - 1)
    def _():
        hs_ref[0, 0] = h_scr[...]


def _rwkv_call(xr, w0, w2p, a0, a2p, g2, k_k, k_a, r_k, ln_g, ln_b, hones, batch, seq_len, h0):
    n = xr.shape[0]
    L = R_CHUNK
    nc = seq_len // L
    zero_init = h0 is None
    n_pairs = R_HEADS // 2

    def tok(b, d, c):
        return (b * nc + c + d * (nc - 1 - 2 * c), 0)

    consts = [w0, w2p, a0, a2p, g2, k_k, k_a, r_k, ln_g, ln_b, hones]
    args = [xr] + consts
    specs = [pl.BlockSpec((L, R_COLS), tok)] + [_const_spec(a.shape) for a in consts]
    if not zero_init:
        args.append(h0)
        specs.append(pl.BlockSpec((1, 1, n_pairs, LANES, LANES), lambda b, d, c: (b, d, 0, 0, 0)))
    outs = [jax.ShapeDtypeStruct((n, R_WIDTH), F32),
            jax.ShapeDtypeStruct((batch, 2, n_pairs, LANES, LANES), F32)]
    out_specs = [pl.BlockSpec((seq_len, R_WIDTH), lambda b, d, c: (b, 0)),
                 pl.BlockSpec((1, 1, n_pairs, LANES, LANES), lambda b, d, c: (b, d, 0, 0, 0))]
    body = functools.partial(_rwkv_body, zero_init, nc)
    return pl.pallas_call(
        body,
        grid=(batch, 2, nc),
        in_specs=specs,
        out_specs=out_specs,
        out_shape=outs,
        scratch_shapes=[pltpu.VMEM((n_pairs, LANES, LANES), F32)],
        compiler_params=_cparams(("parallel", "arbitrary", "arbitrary")),
        name="rwkv_zero" if zero_init else "rwkv_init",
    )(*args)


def _pairs_from_heads(s):
    ht = jnp.swapaxes(s, -1, -2)
    lead = ht.shape[:-3]
    ht = ht.reshape(lead + (R_HEADS // 2, 2, R_HEAD, R_HEAD))
    z = jnp.zeros_like(ht[..., 0, :, :])
    top = jnp.concatenate([ht[..., 0, :, :], z], axis=-1)
    bot = jnp.concatenate([z, ht[..., 1, :, :]], axis=-1)
    return jnp.concatenate([top, bot], axis=-2)


def _heads_from_pairs(hp):
    a = hp[..., :R_HEAD, :R_HEAD]
    b = hp[..., R_HEAD:, R_HEAD:]
    s = jnp.stack([a, b], axis=-3)
    s = s.reshape(hp.shape[:-3] + (R_HEADS, R_HEAD, R_HEAD))
    return jnp.swapaxes(s, -1, -2)


def _trunk(x, mod3, mod_row0, per_seq_rows, grid_mode, init, w):
    batch, seq_len, d = x.shape
    x2 = x.reshape(batch * seq_len, d)

    def rows_for(tile):
        per = seq_len // tile
        if per_seq_rows:
            return lambda i: mod_row0 + i // per
        return lambda i: mod_row0

    x1 = _ffn_call(x2, mod3, rows_for(FFN_TILE), 0, w["norm_ffn1"], w["ffn1_in"], w["ffn1_out"])
    mix_tile = MIX_TILE if grid_mode else seq_len
    q, k, v, o, gc, gr, xr = _mixin_call(
        x1, seq_len, grid_mode, mod3, rows_for(mix_tile), w["norm_mix"], w["wm"], w["wg"], w["wgt"],
        w["wr"], w["conv_w"], w["conv_b"], w["mu"])
    if init is None:
        m_init = None
        r_init = None
    else:
        c0, n0, m0, s0 = init
        m0p = jnp.broadcast_to(jnp.pad(m0, ((0, 0), (0, 0), (0, 8 - M_HEADS)))[..., None], m0.shape[:2] + (8, LANES))
        m_init = (c0, n0, m0p)
        r_init = _pairs_from_heads(s0)
    mo, cs, ns, ms = _mlstm_call(q, k, v, o, gc, gr, w["gate_b_row"], w["gate_b_col"], w["head_g"],
                                 batch, seq_len, m_init)
    ro, hs = _rwkv_call(xr, w["r_w0"], w["r_w2p"], w["r_a0"], w["r_a2p"], w["r_g2"], w["r_k_k"], w["r_k_a"],
                        w["r_r_k"], w["r_ln_g"], w["r_ln_b"], w["hones"], batch, seq_len, r_init)
    y = _ffn_call(x1, mod3, rows_for(FFN_TILE), 6, w["norm_ffn2"], w["ffn2_in"], w["ffn2_out"],
                  mix=(mo, ro, w["wo_m"], w["wo_r"]), g_final=w["norm_final"])
    states = (cs, ns, ms[:, :, :M_HEADS, 0], _heads_from_pairs(hs))
    return y.reshape(batch, seq_len, d), states


def _prepare_weights(ada_w, ada_b, norm_ffn1, ffn1_w_in, ffn1_w_out, norm_mix, mix_w_in, mix_w_out,
                     m_conv_w, m_conv_b, m_gate_b, m_head_g, r_mu, r_w0, r_w2, r_a0, r_a2, r_g2,
                     r_k_k, r_k_a, r_r_k, r_ln_g, r_ln_b, norm_ffn2, ffn2_w_in, ffn2_w_out, norm_final):
    assert ada_w.shape[0] == 1, "single trunk layer"
    lora = r_w2.shape[2]
    nm = 4 * M_WIDTH

    w_in = mix_w_in[0]
    gate_w = w_in[:, nm:nm + GATE_COLS]
    zpad = jnp.zeros((lora, R_WIDTH), F32)

    def dir_pad(w2):
        return jnp.stack([jnp.concatenate([w2[0], zpad], axis=0), jnp.concatenate([zpad, w2[1]], axis=0)])

    head_id = jnp.arange(R_WIDTH) // R_HEAD
    return dict(
        norm_ffn1=norm_ffn1, norm_mix=norm_mix, norm_ffn2=norm_ffn2, norm_final=norm_final[None],
        ffn1_in=ffn1_w_in[0].astype(BF16), ffn1_out=ffn1_w_out[0].astype(BF16),
        ffn2_in=ffn2_w_in[0].astype(BF16), ffn2_out=ffn2_w_out[0].astype(BF16),
        wm=w_in[:, :nm].astype(BF16),
        wg=jnp.pad(gate_w, ((0, 0), (0, LANES - GATE_COLS))).astype(BF16),
        wgt=gate_w.T.astype(BF16),
        wr=w_in[:, nm + GATE_COLS:].astype(BF16),
        wo_m=mix_w_out[0, :M_WIDTH].astype(BF16), wo_r=mix_w_out[0, M_WIDTH:].astype(BF16),
        conv_w=m_conv_w[0], conv_b=m_conv_b, mu=r_mu[0],
        gate_b_row=jnp.pad(m_gate_b[0].reshape(1, GATE_COLS), ((0, 0), (0, LANES - GATE_COLS))),
        gate_b_col=m_gate_b[0].reshape(GATE_COLS, 1),
        head_g=m_head_g,
        r_w0=r_w0[0], r_w2p=dir_pad(r_w2[0]).astype(BF16), r_a0=r_a0[0], r_a2p=dir_pad(r_a2[0]).astype(BF16),
        r_g2=r_g2[0].astype(BF16), r_k_k=r_k_k, r_k_a=r_k_a, r_r_k=r_r_k[0].reshape(1, R_WIDTH),
        r_ln_g=r_ln_g, r_ln_b=r_ln_b,
        hones=(head_id[:, None] == head_id[None, :]).astype(BF16),
    )


def kernel(x_prompt, x_sample, c, state_mlstm_C, state_mlstm_n, state_mlstm_m, state_rwkv_S, c_ctx,
           ada_w, ada_b, norm_ffn1, ffn1_w_in, ffn1_w_out, norm_mix, mix_w_in, mix_w_out,
           m_conv_w, m_conv_b, m_gate_b, m_head_g, r_mu, r_w0, r_w2, r_a0, r_a2, r_g2,
           r_k_k, r_k_a, r_r_k, r_ln_g, r_ln_b, norm_ffn2, ffn2_w_in, ffn2_w_out, norm_final):
    w = _prepare_weights(ada_w, ada_b, norm_ffn1, ffn1_w_in, ffn1_w_out, norm_mix, mix_w_in, mix_w_out,
                         m_conv_w, m_conv_b, m_gate_b, m_head_g, r_mu, r_w0, r_w2, r_a0, r_a2, r_g2,
                         r_k_k, r_k_a, r_r_k, r_ln_g, r_ln_b, norm_ffn2, ffn2_w_in, ffn2_w_out, norm_final)
    d = x_prompt.shape[-1]
    dec_batch = x_sample.shape[0]
    cond =jnp.concatenate([c_ctx[None], c, jnp.zeros((16 - 1 - dec_batch, d), F32)], axis=0)
    mod3 = _ada_call(cond, ada_w[0], ada_b).reshape(16, 9, d)

    y_prompt, (cs, ns, ms, ss) = _trunk(x_prompt, mod3, 0, False, False, None, w)
    init = (state_mlstm_C[:, 0], state_mlstm_n[:, 0], state_mlstm_m[:, 0], state_rwkv_S[:, 0])
    y_sample, _ = _trunk(x_sample, mod3, 1, True, True, init, w)
    return (y_prompt, y_sample, cs[:, None], ns[:, None], ms[:, None], ss[:, None])
```

```python
import functools

import jax
import jax.numpy as jnp
from jax import lax
from jax.experimental import pallas as pl
from jax.experimental.pallas import tpu as pltpu

F32 = jnp.float32
BF16 = jnp.bfloat16

EPS = 1e-6
R_LN_EPS = 64e-5
GRID_W = 64
M_HEADS = 4
M_DK = 128
M_DV = 128
M_WIDTH = M_HEADS * M_DV
R_HEADS = 8
R_HEAD = 64
R_WIDTH = R_HEADS * R_HEAD
R_LORA = 128
R_COLS = 3 * R_WIDTH + 3 * R_LORA
GATE_COLS = 16
LANES = 128

FFN_TILE = 512
FFN_CHUNK = 256
MIX_TILE = 512
M_CHUNK = 256
R_CHUNK = 64
R_CHUNKS_PER_STEP = 4
VMEM_LIMIT = 56 * 1024 * 1024


def _dot(a, b):
    return jnp.dot(a, b, preferred_element_type=F32)


def _dot_nt(a, b):
    return lax.dot_general(a, b, (((1,), (1,)), ((), ())), preferred_element_type=F32)


def _dot_tn(a, b):
    return lax.dot_general(a, b, (((0,), (0,)), ((), ())), preferred_element_type=F32)


def _split3(x):
    hi = x.astype(BF16)
    r1 = x - hi.astype(F32)
    mid = r1.astype(BF16)
    lo = (r1 - mid.astype(F32)).astype(BF16)
    return hi, mid, lo


def _sel_dot(mat, x):
    hi, mid, lo = _split3(x)
    return _dot(mat, hi) + _dot(mat, mid) + _dot(mat, lo)


def _dot_sel(x, mat):
    hi, mid, lo = _split3(x)
    return _dot(hi, mat) + _dot(mid, mat) + _dot(lo, mat)


def _silu(x):
    return x * jax.nn.sigmoid(x)


def _rms_mod(x, g, shift, scale):
    y = x * lax.rsqrt(jnp.mean(x * x, axis=-1, keepdims=True) + EPS) * g
    return y * (1.0 + scale) + shift


def _cparams(sem):
    return pltpu.CompilerParams(dimension_semantics=sem, vmem_limit_bytes=VMEM_LIMIT)


def _const_spec(shape):
    nd = len(shape)
    return pl.BlockSpec(shape, lambda *_: (0,) * nd, pipeline_mode=pl.Buffered(1))


def _ada_body(c_ref, w_ref, b_ref, o_ref):
    s = _silu(c_ref[...])
    o_ref[...] = _dot(s.astype(BF16), w_ref[...].astype(BF16)) + b_ref[...]


def _ada_call(cond, w, b):
    rows, d = cond.shape
    n = w.shape[1]
    tn = d
    return pl.pallas_call(
        _ada_body,
        grid=(n // tn,),
        in_specs=[
            pl.BlockSpec((rows, d), lambda j: (0, 0)),
            pl.BlockSpec((d, tn), lambda j: (0, j)),
            pl.BlockSpec((1, tn), lambda j: (0, j)),
        ],
        out_specs=pl.BlockSpec((rows, tn), lambda j: (0, j)),
        out_shape=jax.ShapeDtypeStruct((rows, n), F32),
        compiler_params=_cparams(("arbitrary",)),
        name="ada",
    )(cond, w, b)


def _ffn_body(mod_base, with_mix, with_final, *refs):
    refs = list(refs)
    x_ref = refs.pop(0)
    if with_mix:
        mo_ref, ro_ref, wmm_ref, wmr_ref = refs[:4]
        refs = refs[4:]
    mod_ref, g_ref, win_ref, wout_ref = refs[:4]
    refs = refs[4:]
    if with_final:
        gfin_ref = refs.pop(0)
    out_ref, a_scr = refs

    x = x_ref[...]
    if with_mix:
        mix = _dot(mo_ref[...].astype(BF16), wmm_ref[...]) + _dot(ro_ref[...].astype(BF16), wmr_ref[...])
        x = x + mod_ref[0, mod_base - 1:mod_base, :] * mix
    shift = mod_ref[0, mod_base:mod_base + 1, :]
    scale = mod_ref[0, mod_base + 1:mod_base + 2, :]
    gate = mod_ref[0, mod_base + 2:mod_base + 3, :]
    hb = _rms_mod(x, g_ref[...], shift, scale).astype(BF16)
    dff = wout_ref.shape[0]
    for j in range(dff // FFN_CHUNK):
        lo = j * FFN_CHUNK
        gt = _dot(hb, win_ref[:, lo:lo + FFN_CHUNK])
        up = _dot(hb, win_ref[:, dff + lo:dff + lo + FFN_CHUNK])
        a_scr[:, lo:lo + FFN_CHUNK] = (_silu(gt) * up).astype(BF16)
    y = x + (0.5 * gate) * _dot(a_scr[...], wout_ref[...])
    if with_final:
        y = y * lax.rsqrt(jnp.mean(y * y, axis=-1, keepdims=True) + EPS) * gfin_ref[...]
    out_ref[...] = y


def _ffn_call(x, mod3, row_of_tile, mod_base, g, w_in, w_out, mix=None, g_final=None):
    n, d = x.shape
    tm = FFN_TILE
    dff = w_out.shape[0]
    tok = lambda i: (i, 0)
    args = [x]
    specs = [pl.BlockSpec((tm, d), tok)]
    if mix is not None:
        mo, ro, wmm, wmr = mix
        args += [mo, ro, wmm, wmr]
        specs += [pl.BlockSpec((tm, mo.shape[1]), tok), pl.BlockSpec((tm, ro.shape[1]), tok),
                  _const_spec(wmm.shape), _const_spec(wmr.shape)]
    args += [mod3, g, w_in, w_out]
    specs += [pl.BlockSpec((1,) + mod3.shape[1:], lambda i: (row_of_tile(i), 0, 0)),
              _const_spec(g.shape), _const_spec(w_in.shape), _const_spec(w_out.shape)]
    if g_final is not None:
        args.append(g_final)
        specs.append(_const_spec(g_final.shape))
    body = functools.partial(_ffn_body, mod_base, mix is not None, g_final is not None)
    return pl.pallas_call(
        body,
        grid=(n // tm,),
        in_specs=specs,
        out_specs=pl.BlockSpec((tm, d), tok),
        out_shape=jax.ShapeDtypeStruct((n, d), F32),
        scratch_shapes=[pltpu.VMEM((tm, dff), BF16)],
        compiler_params=_cparams(("parallel",)),
        name="ffn_mix" if mix is not None else "ffn",
    )(*args)


def _mixin_body(grid_mode, has_halo, tiles_per_seq, *refs):
    refs = list(refs)
    x_ref = refs.pop(0)
    if has_halo:
        xp_ref, xn_ref = refs[:2]
        refs = refs[2:]
    (mod_ref, g_ref, wm_ref, wg_ref, wgt_ref, wr_ref, cw_ref, cb_ref, mu_ref,
     q_ref, k_ref, v_ref, o_ref, gc_ref, gr_ref, xr_ref) = refs

    tm = x_ref.shape[0]
    shift = mod_ref[0, 3:4, :]
    scale = mod_ref[0, 4:5, :]
    g = g_ref[...]
    hb = _rms_mod(x_ref[...], g, shift, scale).astype(BF16)
    row = lax.broadcasted_iota(jnp.int32, (tm, 1), 0)
    nqk = 2 * M_WIDTH

    if has_halo:
        i = pl.program_id(0)
        pos = i % tiles_per_seq
        keep_p = (pos > 0).astype(F32)
        keep_n = (pos < tiles_per_seq - 1).astype(F32)
        hp32 = _rms_mod(xp_ref[...], g, shift, scale)
        hn32 = _rms_mod(xn_ref[...], g, shift, scale)
        hp = hp32.astype(BF16)
        hn = hn32.astype(BF16)
        nh = hp32.shape[0]
        edge = 16
        qk_prev = _dot(hp32[nh - edge:, :].astype(BF16), wm_ref[:, :nqk])[edge - 1:edge, :] * keep_p
        qk_next = _dot(hn32[:edge, :].astype(BF16), wm_ref[:, :nqk])[0:1, :] * keep_n
    else:
        qk_prev = jnp.zeros((1, nqk), F32)
        qk_next = jnp.zeros((1, nqk), F32)

    uqk = _dot(hb, wm_ref[:, :nqk])
    u_dn = jnp.where(row == 0, qk_prev, pltpu.roll(uqk, 1, 0))
    u_up = jnp.where(row == tm - 1, qk_next, pltpu.roll(uqk, tm - 1, 0))
    qk = _silu(cw_ref[0:1, :] * u_dn + cw_ref[1:2, :] * uqk + cw_ref[2:3, :] * u_up + cb_ref[...])
    q_ref[...] = qk[:, :M_WIDTH]
    k_ref[...] = qk[:, M_WIDTH:] * (M_DK ** -0.5)
    v_ref[...] = _dot(hb, wm_ref[:, nqk:nqk + M_WIDTH])
    o_ref[...] = _dot(hb, wm_ref[:, nqk + M_WIDTH:])
    gc_ref[...] = _dot(hb, wg_ref[...])
    gr_ref[...] = _dot_nt(wgt_ref[...], hb)

    ur = _dot(hb, wr_ref[...])
    if grid_mode:
        col = row % GRID_W
        left = jnp.where(col == 0, 0.0, pltpu.roll(ur, 1, 0))
        right = jnp.where(col == GRID_W - 1, 0.0, pltpu.roll(ur, tm - 1, 0))
        ur_p = _dot(hp, wr_ref[...]) * keep_p
        ur_n = _dot(hn, wr_ref[...]) * keep_n
        up = jnp.concatenate([ur_p, ur[:tm - GRID_W, :]], axis=0)
        down = jnp.concatenate([ur[GRID_W:, :], ur_n], axis=0)
        xr = (ur + mu_ref[0:1, :] * (left - ur) + mu_ref[1:2, :] * (right - ur)
              + mu_ref[2:3, :] * (up - ur) + mu_ref[3:4, :] * (down - ur))
    else:
        left = jnp.where(row == 0, 0.0, pltpu.roll(ur, 1, 0))
        right = jnp.where(row == tm - 1, 0.0, pltpu.roll(ur, tm - 1, 0))
        xr = ur + mu_ref[0:1, :] * (left - ur) + mu_ref[1:2, :] * (right - ur)
    xr_ref[...] = xr


def _mixin_call(x, seq_len, grid_mode, mod3, row_of_tile, g, wm, wg, wgt, wr, cw, cb, mu):
    n, d = x.shape
    if grid_mode:
        tm = MIX_TILE
        has_halo = True
    else:
        tm = seq_len
        has_halo = False
    tiles_per_seq = seq_len // tm
    tok = lambda i: (i, 0)
    args = [x]
    specs = [pl.BlockSpec((tm, d), tok)]
    if has_halo:
        hb = R_CHUNK
        per = tm // hb
        last = n // hb - 1
        args += [x, x]
        specs += [pl.BlockSpec((hb, d), lambda i: (jnp.maximum(i * per - 1, 0), 0)),
                  pl.BlockSpec((hb, d), lambda i: (jnp.minimum((i + 1) * per, last), 0))]
    args += [mod3, g, wm, wg, wgt, wr, cw, cb, mu]
    specs += [pl.BlockSpec((1,) + mod3.shape[1:], lambda i: (row_of_tile(i), 0, 0))]
    specs += [_const_spec(a.shape) for a in (g, wm, wg, wgt, wr, cw, cb, mu)]
    outs = [jax.ShapeDtypeStruct((n, M_WIDTH), F32)] * 4 + [
        jax.ShapeDtypeStruct((n, LANES), F32),
        jax.ShapeDtypeStruct((GATE_COLS, n), F32),
        jax.ShapeDtypeStruct((n, R_COLS), F32),
    ]
    out_specs = [pl.BlockSpec((tm, M_WIDTH), tok)] * 4 + [
        pl.BlockSpec((tm, LANES), tok),
        pl.BlockSpec((GATE_COLS, tm), lambda i: (0, i)),
        pl.BlockSpec((tm, R_COLS), tok),
    ]
    body = functools.partial(_mixin_body, grid_mode, has_halo, tiles_per_seq)
    return pl.pallas_call(
        body,
        grid=(n // tm,),
        in_specs=specs,
        out_specs=out_specs,
        out_shape=outs,
        compiler_params=_cparams(("parallel",)),
        name="mix_in_grid" if grid_mode else "mix_in_seq",
    )(*args)


def _mlstm_body(zero_init, nc, *refs):
    refs = list(refs)
    q_ref, k_ref, v_ref, o_ref, gc_ref, gr_ref, gbr_ref, gbc_ref, hg_ref = refs[:9]
    refs = refs[9:]
    if not zero_init:
        c0_ref, n0_ref, m0_ref = refs[:3]
        refs = refs[3:]
    out_ref, cs_ref, ns_ref, ms_ref, c_scr, n_scr, m_scr = refs

    d = pl.program_id(1)
    c = pl.program_id(2)
    L = q_ref.shape[0]
    fwd = d == 0

    @pl.when(c == 0)
    def _():
        if zero_init:
            c_scr[...] = jnp.zeros_like(c_scr)
            n_scr[...] = jnp.zeros_like(n_scr)
            m_scr[...] = jnp.zeros_like(m_scr)
        else:
            c_scr[...] = c0_ref[0, 0]
            n_scr[0:M_HEADS, :] = n0_ref[0, 0]
            m_scr[...] = m0_ref[0, 0]

    ri = lax.broadcasted_iota(jnp.int32, (L, L), 0)
    ci = lax.broadcasted_iota(jnp.int32, (L, L), 1)
    sgn = 1 - 2 * d
    mask = (ci - ri) * sgn <= 0
    tri = mask.astype(BF16)
    tri_t = ((ri - ci) * sgn <= 0).astype(BF16)

    gcol = gc_ref[...] + gbr_ref[...]
    grow = gr_ref[...] + gbc_ref[...]
    bcol = _sel_dot(tri, jax.nn.log_sigmoid(gcol))
    brow = _dot_sel(jax.nn.log_sigmoid(grow), tri_t)

    def pick_col(a, j0, j1):
        return jnp.where(fwd, a[:, j0:j0 + 1], a[:, j1:j1 + 1])

    def pick_row(a, j0, j1):
        return jnp.where(fwd, a[j0:j0 + 1, :], a[j1:j1 + 1, :])

    cc = c + d * (nc - 1 - 2 * c)
    rows = pl.ds(pl.multiple_of(cc * L, L), L)

    for h in range(M_HEADS):
        hs = slice(h * M_DV, (h + 1) * M_DV)
        qb = q_ref[:, hs].astype(BF16)
        kf = k_ref[:, hs]
        vb = v_ref[:, hs].astype(BF16)
        i_c = pick_col(gcol, h, 4 + h)
        i_r = pick_row(grow, h, 4 + h)
        b_c = pick_col(bcol, 8 + h, 12 + h)
        b_r = pick_row(brow, 8 + h, 12 + h)
        b_tot = jnp.where(fwd, b_c[L - 1:L, :], b_c[0:1, :])
        m_prev = m_scr[h:h + 1, 0:1]
        n_prev = n_scr[h:h + 1, :]
        c_prev = c_scr[h]

        dm = jnp.where(mask, b_c - b_r + i_r, -jnp.inf)
        m_inter = b_c + m_prev
        m_t = jnp.maximum(m_inter, jnp.max(dm, axis=1, keepdims=True))
        s = _dot_nt(qb, kf.astype(BF16)) * jnp.exp(dm - m_t)
        sc = jnp.exp(m_inter - m_t)
        num = _dot(s.astype(BF16), vb) + sc * _dot(qb, c_prev.astype(BF16))
        qn = jnp.sum(q_ref[:, hs] * n_prev, axis=1, keepdims=True)
        den = jnp.sum(s, axis=1, keepdims=True) + sc * qn
        hh = num / jnp.maximum(jnp.abs(den), jnp.exp(-m_t))

        gs = b_tot - b_c + i_c
        m_new = jnp.maximum(b_tot + m_prev, jnp.max(gs, axis=0, keepdims=True))
        kw = kf * jnp.exp(gs - m_new)
        decay = jnp.exp(b_tot + m_prev - m_new)
        c_scr[h] = decay * c_prev + _dot_tn(kw.astype(BF16), vb)
        n_scr[h:h + 1, :] = decay * n_prev + jnp.sum(kw, axis=0, keepdims=True)
        m_scr[h:h + 1, :] = jnp.broadcast_to(m_new, (1, LANES))

        @pl.when(fwd)
        def _():
            out_ref[rows, hs] = hh

        @pl.when(jnp.logical_not(fwd))
        def _():
            t = jax.nn.sigmoid(o_ref[:, hs]) * (out_ref[rows, hs] + hh)
            t = t * lax.rsqrt(jnp.mean(t * t, axis=1, keepdims=True) + EPS)
            out_ref[rows, hs] = t * hg_ref[:, hs]

    @pl.when(c == nc - 1)
    def _():
        cs_ref[0, 0] = c_scr[...]
        ns_ref[0, 0] = n_scr[0:M_HEADS, :]
        ms_ref[0, 0] = m_scr[...]


def _mlstm_call(q, k, v, o, gc, gr, gate_b_row, gate_b_col, head_g, batch, seq_len, init):
    n = q.shape[0]
    L = min(M_CHUNK, seq_len)
    nc = seq_len // L
    zero_init = init is None

    def tok(b, d, c):
        return (b * nc + c + d * (nc - 1 - 2 * c), 0)

    def tok_t(b, d, c):
        return (0, b * nc + c + d * (nc - 1 - 2 * c))

    args = [q, k, v, o, gc, gr, gate_b_row, gate_b_col, head_g]
    specs = [pl.BlockSpec((L, M_WIDTH), tok)] * 4 + [
        pl.BlockSpec((L, LANES), tok),
        pl.BlockSpec((GATE_COLS, L), tok_t),
        _const_spec(gate_b_row.shape), _const_spec(gate_b_col.shape), _const_spec(head_g.shape),
    ]
    if not zero_init:
        c0, n0, m0 = init
        args += [c0, n0, m0]
        specs += [pl.BlockSpec((1, 1) + c0.shape[2:], lambda b, d, c: (b, d, 0, 0, 0)),
                  pl.BlockSpec((1, 1) + n0.shape[2:], lambda b, d, c: (b, d, 0, 0)),
                  pl.BlockSpec((1, 1) + m0.shape[2:], lambda b, d, c: (b, d, 0, 0))]
    outs = [jax.ShapeDtypeStruct((n, M_WIDTH), F32),
            jax.ShapeDtypeStruct((batch, 2, M_HEADS, M_DK, M_DV), F32),
            jax.ShapeDtypeStruct((batch, 2, M_HEADS, M_DK), F32),
            jax.ShapeDtypeStruct((batch, 2, 8, LANES), F32)]
    out_specs = [pl.BlockSpec((seq_len, M_WIDTH), lambda b, d, c: (b, 0)),
                 pl.BlockSpec((1, 1, M_HEADS, M_DK, M_DV), lambda b, d, c: (b, d, 0, 0, 0)),
                 pl.BlockSpec((1, 1, M_HEADS, M_DK), lambda b, d, c: (b, d, 0, 0)),
                 pl.BlockSpec((1, 1, 8, LANES), lambda b, d, c: (b, d, 0, 0))]
    body = functools.partial(_mlstm_body, zero_init, nc)
    return pl.pallas_call(
        body,
        grid=(batch, 2, nc),
        in_specs=specs,
        out_specs=out_specs,
        out_shape=outs,
        scratch_shapes=[pltpu.VMEM((M_HEADS, M_DK, M_DV), F32),
                        pltpu.VMEM((8, LANES), F32),
                        pltpu.VMEM((8, LANES), F32)],
        compiler_params=_cparams(("parallel", "arbitrary", "arbitrary")),
        name="mlstm_zero" if zero_init else "mlstm_init",
    )(*args)


def _rwkv_body(zero_init, nb, *refs):
    refs = list(refs)
    (xr_ref, w0_ref, w2_ref, a0_ref, a2_ref, g2_ref, kk_ref, ka_ref, rk_ref,
     lng_ref, lnb_ref, hones_ref) = refs[:12]
    refs = refs[12:]
    if not zero_init:
        h0_ref = refs.pop(0)
    out_ref, hs_ref, h_scr = refs

    d = pl.program_id(1)
    c = pl.program_id(2)
    LB = xr_ref.shape[0]
    L = R_CHUNK
    n_chunks = LB // L
    P = 2 * L
    fwd = d == 0
    sgn = 1 - 2 * d
    pairs = range(R_HEADS // 2)
    cols = [slice(p * LANES, (p + 1) * LANES) for p in pairs]

    @pl.when(c == 0)
    def _():
        if zero_init:
            h_scr[...] = jnp.zeros_like(h_scr)
        else:
            h_scr[...] = h0_ref[0, 0]

    r = xr_ref[:, 0:R_WIDTH]
    k = xr_ref[:, R_WIDTH:2 * R_WIDTH]
    v = xr_ref[:, 2 * R_WIDTH:3 * R_WIDTH]
    wd = xr_ref[:, 3 * R_WIDTH:3 * R_WIDTH + R_LORA]
    ad = xr_ref[:, 3 * R_WIDTH + R_LORA:3 * R_WIDTH + 2 * R_LORA]
    gin = xr_ref[:, 3 * R_WIDTH + 2 * R_LORA:]
    hones = hones_ref[...]

    def head_sum(x):
        return jnp.concatenate([_dot_sel(x[:, cs], hones) for cs in cols], axis=1)

    tw = jnp.tanh(wd).astype(BF16)
    adb = ad.astype(BF16)
    ww = jnp.where(fwd, w0_ref[0:1, :] + _dot(tw, w2_ref[0]), w0_ref[1:2, :] + _dot(tw, w2_ref[1]))
    lw = -jnp.exp(-jax.nn.softplus(-ww) - 0.5)
    a_f = jax.nn.sigmoid(a0_ref[0:1, :] + _dot(adb, a2_ref[0]))
    a_b = jax.nn.sigmoid(a0_ref[1:2, :] + _dot(adb, a2_ref[1]))
    a = jnp.where(fwd, a_f, a_b)
    ka = ka_ref[...]
    kd = k * (1.0 + (a - 1.0) * ka)
    kk = k * kk_ref[...]
    kk = kk / jnp.maximum(jnp.sqrt(head_sum(kk * kk)), 1e-12)
    bvec = kk * a

    ri = lax.broadcasted_iota(jnp.int32, (LB, LB), 0)
    ci = lax.broadcasted_iota(jnp.int32, (LB, LB), 1)
    same_chunk = (ri // L) == (ci // L)
    tri = jnp.logical_and(same_chunk, (ci - ri) * sgn <= 0).astype(BF16)
    cl = _sel_dot(tri, lw)
    tot = _sel_dot(same_chunk.astype(BF16), lw)
    e_in = jnp.exp(cl)
    e_out = jnp.exp(-cl)
    e_end = jnp.exp(tot - cl)
    a_til = -kk * jnp.exp(cl - lw)
    r_til = r * e_in
    b_til = bvec * e_out
    k_til = kd * e_out
    b_hat = bvec * e_end
    k_hat = kd * e_end
    w_end = jnp.exp(tot)

    pr = lax.broadcasted_iota(jnp.int32, (P, P), 0)
    pc = lax.broadcasted_iota(jnp.int32, (P, P), 1)
    same = (pr // L) == (pc // L)
    before = (pc - pr) * sgn < 0
    m_strict = jnp.logical_and(same, before)
    m_incl = jnp.logical_and(same, jnp.logical_or(before, pc == pr))
    eye = pr == pc
    lane = lax.broadcasted_iota(jnp.int32, (L, LANES), 1)
    first = lane < R_HEAD

    def stack(x, j, p):
        x = x[j * L:(j + 1) * L, cols[p]]
        return jnp.concatenate([jnp.where(first, x, 0.0), jnp.where(first, 0.0, x)], axis=0).astype(BF16)

    chains = [(j, p) for j in range(n_chunks) for p in pairs]
    at = [stack(a_til, j, p) for j, p in chains]
    rt = [stack(r_til, j, p) for j, p in chains]
    bt = [stack(b_til, j, p) for j, p in chains]
    kt = [stack(k_til, j, p) for j, p in chains]
    bh = [stack(b_hat, j, p) for j, p in chains]
    kh = [stack(k_hat, j, p) for j, p in chains]
    vs = [stack(v, j, p) for j, p in chains]
    ids = range(len(chains))

    big = [_dot_nt(jnp.concatenate([at[i], rt[i]], axis=0), jnp.concatenate([bt[i], kt[i]], axis=0)) for i in ids]
    a_ab = [jnp.where(m_strict, big[i][:P, :P], 0.0) for i in ids]
    a_ak = [jnp.where(m_strict, big[i][:P, P:], 0.0).astype(BF16) for i in ids]
    a_rb = [jnp.where(m_incl, big[i][P:, :P], 0.0).astype(BF16) for i in ids]
    a_rk = [jnp.where(m_incl, big[i][P:, P:], 0.0).astype(BF16) for i in ids]

    pair2 = (pr // 2) == (pc // 2)
    tinv = [jnp.where(eye, 1.0, jnp.where(pair2, a_ab[i], 0.0)) for i in ids]
    blk = 2
    while blk < L:
        sib = jnp.logical_and((pr // (2 * blk)) == (pc // (2 * blk)), (pr // blk) != (pc // blk))
        tb = [tinv[i].astype(BF16) for i in ids]
        half = [_dot(tb[i], jnp.where(sib, a_ab[i], 0.0).astype(BF16)) for i in ids]
        tinv = [tinv[i] + _dot(half[i].astype(BF16), tb[i]) for i in ids]
        blk *= 2

    av = [_dot(a_ak[i], vs[i]) for i in ids]
    pq = [_dot(tinv[i].astype(BF16), jnp.concatenate([at[i], av[i].astype(BF16)], axis=1)).astype(BF16) for i in ids]
    ry = [_dot(a_rb[i], pq[i]) for i in ids]
    rkv = [_dot(a_rk[i], vs[i]) for i in ids]
    mg = [_dot_tn(bh[i], pq[i]) for i in ids]
    kv = [_dot_tn(kh[i], vs[i]) for i in ids]
    r_hat = [(rt[i].astype(F32) + ry[i][:, :LANES]).astype(BF16) for i in ids]
    y0 = [ry[i][:, LANES:] + rkv[i] for i in ids]
    m_corr = [mg[i][:, :LANES].astype(BF16) for i in ids]
    g_add = [mg[i][:, LANES:] + kv[i] for i in ids]
    w_col = [jnp.sum(jnp.where(eye, jnp.broadcast_to(w_end[j * L:j * L + 1, cols[p]], (P, P)), 0.0),
                     axis=1, keepdims=True) for j, p in chains]

    def scan_chunks(order):
        y = [None] * n_chunks
        for j in order:
            ys = []
            for p in pairs:
                i = j * len(pairs) + p
                hprev = h_scr[p]
                hb = hprev.astype(BF16)
                yst = _dot(r_hat[i], hb) + y0[i]
                ys.append(yst[:L, :] + yst[L:, :])
                h_scr[p] = w_col[i] * hprev + _dot(m_corr[i], hb) + g_add[i]
            y[j] = jnp.concatenate(ys, axis=1)
        return jnp.concatenate(y, axis=0)

    cc = c + d * (nb - 1 - 2 * c)
    rows = pl.ds(pl.multiple_of(cc * LB, LB), LB)

    @pl.when(fwd)
    def _():
        out_ref[rows, :] = scan_chunks(range(n_chunks))

    @pl.when(jnp.logical_not(fwd))
    def _():
        inv_n = 1.0 / R_HEAD
        ysum = out_ref[rows, :] + scan_chunks(reversed(range(n_chunks)))
        mean = head_sum(ysum) * inv_n
        yc = ysum - mean
        var = head_sum(yc * yc) * inv_n
        yn = yc * lax.rsqrt(var + R_LN_EPS) * lng_ref[...] + lnb_ref[...]
        k_bar = k * (1.0 + (0.5 * (a_f + a_b) - 1.0) * ka)
        bonus = head_sum(r * k_bar * rk_ref[...]) * v
        gate = _dot(jax.nn.sigmoid(gin).astype(BF16), g2_ref[...])
        out_ref[rows, :] = (yn + bonus) * gate

    @pl.when(c == nb - 1)
    def _():
        hs_ref[0, 0] = h_scr[...]


def _rwkv_call(xr, w0, w2p, a0, a2p, g2, k_k, k_a, r_k, ln_g, ln_b, hones, batch, seq_len, h0):
    n = xr.shape[0]
    LB = R_CHUNK * R_CHUNKS_PER_STEP
    nb = seq_len // LB
    zero_init = h0 is None
    n_pairs = R_HEADS // 2

    def tok(b, d, c):
        return (b * nb + c + d * (nb - 1 - 2 * c), 0)

    consts = [w0, w2p, a0, a2p, g2, k_k, k_a, r_k, ln_g, ln_b, hones]
    args = [xr] + consts
    specs = [pl.BlockSpec((LB, R_COLS), tok)] + [_const_spec(a.shape) for a in consts]
    if not zero_init:
        args.append(h0)
        specs.append(pl.BlockSpec((1, 1, n_pairs, LANES, LANES), lambda b, d, c: (b, d, 0, 0, 0)))
    outs = [jax.ShapeDtypeStruct((n, R_WIDTH), F32),
            jax.ShapeDtypeStruct((batch, 2, n_pairs, LANES, LANES), F32)]
    out_specs = [pl.BlockSpec((seq_len, R_WIDTH), lambda b, d, c: (b, 0)),
                 pl.BlockSpec((1, 1, n_pairs, LANES, LANES), lambda b, d, c: (b, d, 0, 0, 0))]
    body = functools.partial(_rwkv_body, zero_init, nb)
    return pl.pallas_call(
        body,
        grid=(batch, 2, nb),
        in_specs=specs,
        out_specs=out_specs,
        out_shape=outs,
        scratch_shapes=[pltpu.VMEM((n_pairs, LANES, LANES), F32)],
        compiler_params=_cparams(("parallel", "arbitrary", "arbitrary")),
        name="rwkv_zero" if zero_init else "rwkv_init",
    )(*args)


def _pairs_from_heads(s):
    ht = jnp.swapaxes(s, -1, -2)
    lead = ht.shape[:-3]
    ht = ht.reshape(lead + (R_HEADS // 2, 2, R_HEAD, R_HEAD))
    z = jnp.zeros_like(ht[..., 0, :, :])
    top = jnp.concatenate([ht[..., 0, :, :], z], axis=-1)
    bot = jnp.concatenate([z, ht[..., 1, :, :]], axis=-1)
    return jnp.concatenate([top, bot], axis=-2)


def _heads_from_pairs(hp):
    a = hp[..., :R_HEAD, :R_HEAD]
    b = hp[..., R_HEAD:, R_HEAD:]
    s = jnp.stack([a, b], axis=-3)
    s = s.reshape(hp.shape[:-3] + (R_HEADS, R_HEAD, R_HEAD))
    return jnp.swapaxes(s, -1, -2)


def _trunk(x, mod3, mod_row0, per_seq_rows, grid_mode, init, w):
    batch, seq_len, d = x.shape
    x2 = x.reshape(batch * seq_len, d)

    def rows_for(tile):
        per = seq_len // tile
        if per_seq_rows:
            return lambda i: mod_row0 + i // per
        return lambda i: mod_row0

    x1 = _ffn_call(x2, mod3, rows_for(FFN_TILE), 0, w["norm_ffn1"], w["ffn1_in"], w["ffn1_out"])
    mix_tile = MIX_TILE if grid_mode else seq_len
    q, k, v, o, gc, gr, xr = _mixin_call(
        x1, seq_len, grid_mode, mod3, rows_for(mix_tile), w["norm_mix"], w["wm"], w["wg"], w["wgt"],
        w["wr"], w["conv_w"], w["conv_b"], w["mu"])
    if init is None:
        m_init = None
        r_init = None
    else:
        c0, n0, m0, s0 = init
        m0p = jnp.broadcast_to(jnp.pad(m0, ((0, 0), (0, 0), (0, 8 - M_HEADS)))[..., None], m0.shape[:2] + (8, LANES))
        m_init = (c0, n0, m0p)
        r_init = _pairs_from_heads(s0)
    mo, cs, ns, ms = _mlstm_call(q, k, v, o, gc, gr, w["gate_b_row"], w["gate_b_col"], w["head_g"],
                                 batch, seq_len, m_init)
    ro, hs = _rwkv_call(xr, w["r_w0"], w["r_w2p"], w["r_a0"], w["r_a2p"], w["r_g2"], w["r_k_k"], w["r_k_a"],
                        w["r_r_k"], w["r_ln_g"], w["r_ln_b"], w["hones"], batch, seq_len, r_init)
    y = _ffn_call(x1, mod3, rows_for(FFN_TILE), 6, w["norm_ffn2"], w["ffn2_in"], w["ffn2_out"],
                  mix=(mo, ro, w["wo_m"], w["wo_r"]), g_final=w["norm_final"])
    states = (cs, ns, ms[:, :, :M_HEADS, 0], _heads_from_pairs(hs))
    return y.reshape(batch, seq_len, d), states


def _prepare_weights(ada_w, ada_b, norm_ffn1, ffn1_w_in, ffn1_w_out, norm_mix, mix_w_in, mix_w_out,
                     m_conv_w, m_conv_b, m_gate_b, m_head_g, r_mu, r_w0, r_w2, r_a0, r_a2, r_g2,
                     r_k_k, r_k_a, r_r_k, r_ln_g, r_ln_b, norm_ffn2, ffn2_w_in, ffn2_w_out, norm_final):
    assert ada_w.shape[0] == 1, "single trunk layer"
    lora = r_w2.shape[2]
    nm = 4 * M_WIDTH

    w_in = mix_w_in[0]
    gate_w = w_in[:, nm:nm + GATE_COLS]
    zpad = jnp.zeros((lora, R_WIDTH), F32)

    def dir_pad(w2):
        return jnp.stack([jnp.concatenate([w2[0], zpad], axis=0), jnp.concatenate([zpad, w2[1]], axis=0)])

    head_id = jnp.arange(LANES) // R_HEAD
    return dict(
        norm_ffn1=norm_ffn1, norm_mix=norm_mix, norm_ffn2=norm_ffn2, norm_final=norm_final[None],
        ffn1_in=ffn1_w_in[0].astype(BF16), ffn1_out=ffn1_w_out[0].astype(BF16),
        ffn2_in=ffn2_w_in[0].astype(BF16), ffn2_out=ffn2_w_out[0].astype(BF16),
        wm=w_in[:, :nm].astype(BF16),
        wg=jnp.pad(gate_w, ((0, 0), (0, LANES - GATE_COLS))).astype(BF16),
        wgt=gate_w.T.astype(BF16),
        wr=w_in[:, nm + GATE_COLS:].astype(BF16),
        wo_m=mix_w_out[0, :M_WIDTH].astype(BF16), wo_r=mix_w_out[0, M_WIDTH:].astype(BF16),
        conv_w=m_conv_w[0], conv_b=m_conv_b, mu=r_mu[0],
        gate_b_row=jnp.pad(m_gate_b[0].reshape(1, GATE_COLS), ((0, 0), (0, LANES - GATE_COLS))),
        gate_b_col=m_gate_b[0].reshape(GATE_COLS, 1),
        head_g=m_head_g,
        r_w0=r_w0[0], r_w2p=dir_pad(r_w2[0]).astype(BF16), r_a0=r_a0[0], r_a2p=dir_pad(r_a2[0]).astype(BF16),
        r_g2=r_g2[0].astype(BF16), r_k_k=r_k_k, r_k_a=r_k_a, r_r_k=r_r_k[0].reshape(1, R_WIDTH),
        r_ln_g=r_ln_g, r_ln_b=r_ln_b,
        hones=(head_id[:, None] == head_id[None, :]).astype(BF16),
    )


def kernel(x_prompt, x_sample, c, state_mlstm_C, state_mlstm_n, state_mlstm_m, state_rwkv_S, c_ctx,
           ada_w, ada_b, norm_ffn1, ffn1_w_in, ffn1_w_out, norm_mix, mix_w_in, mix_w_out,
           m_conv_w, m_conv_b, m_gate_b, m_head_g, r_mu, r_w0, r_w2, r_a0, r_a2, r_g2,
           r_k_k, r_k_a, r_r_k, r_ln_g, r_ln_b, norm_ffn2, ffn2_w_in, ffn2_w_out, norm_final):
    w = _prepare_weights(ada_w, ada_b, norm_ffn1, ffn1_w_in, ffn1_w_out, norm_mix, mix_w_in, mix_w_out,
                         m_conv_w, m_conv_b, m_gate_b, m_head_g, r_mu, r_w0, r_w2, r_a0, r_a2, r_g2,
                         r_k_k, r_k_a, r_r_k, r_ln_g, r_ln_b, norm_ffn2, ffn2_w_in, ffn2_w_out, norm_final)
    d = x_prompt.shape[-1]
    dec_batch = x_sample.shape[0]
    cond = jnp.concatenate([c_ctx[None], c, jnp.zeros((16 - 1 - dec_batch, d), F32)], axis=0)
    mod3 = _ada_call(cond, ada_w[0], ada_b).reshape(16, 9, d)

    y_prompt, (cs, ns, ms, ss) = _trunk(x_prompt, mod3, 0, False, False, None, w)
    init = (state_mlstm_C[:, 0], state_mlstm_n[:, 0], state_mlstm_m[:, 0], state_rwkv_S[:, 0])
    y_sample, _ = _trunk(x_sample, mod3, 1, True, True, init, w)
    return (y_prompt, y_sample, cs[:, None], ns[:, None], ms[:, None], ss[:, None])
```

```python
import functools

import jax
import jax.numpy as jnp
from jax import lax
from jax.experimental import pallas as pl
from jax.experimental.pallas import tpu as pltpu

F32 = jnp.float32
BF16 = jnp.bfloat16

EPS = 1e-6
R_LN_EPS = 64e-5
GRID_W = 64
M_HEADS = 4
M_DK = 128
M_DV = 128
M_WIDTH = M_HEADS * M_DV
R_HEADS = 8
R_HEAD = 64
R_WIDTH = R_HEADS * R_HEAD
R_LORA = 128
R_COLS = 3 * R_WIDTH + 3 * R_LORA
GATE_COLS = 16
LANES = 128

FFN_TILE = 512
FFN_CHUNK = 256
MIX_TILE = 512
M_CHUNK = 256
R_CHUNK = 64
R_CHUNKS_PER_STEP = 4
VMEM_LIMIT = 56 * 1024 * 1024


def _dot(a, b):
    return jnp.dot(a, b, preferred_element_type=F32)


def _dot_nt(a, b):
    return lax.dot_general(a, b, (((1,), (1,)), ((), ())), preferred_element_type=F32)


def _dot_tn(a, b):
    return lax.dot_general(a, b, (((0,), (0,)), ((), ())), preferred_element_type=F32)


def _split3(x):
    hi = x.astype(BF16)
    r1 = x - hi.astype(F32)
    mid = r1.astype(BF16)
    lo = (r1 - mid.astype(F32)).astype(BF16)
    return hi, mid, lo


def _sel_dot(mat, x):
    hi, mid, lo = _split3(x)
    return _dot(mat, hi) + _dot(mat, mid) + _dot(mat, lo)


def _dot_sel(x, mat):
    hi, mid, lo = _split3(x)
    return _dot(hi, mat) + _dot(mid, mat) + _dot(lo, mat)


def _silu(x):
    return x * jax.nn.sigmoid(x)


def _rms_mod(x, g, shift, scale):
    y = x * lax.rsqrt(jnp.mean(x * x, axis=-1, keepdims=True) + EPS) * g
    return y * (1.0 + scale) + shift


def _cparams(sem):
    return pltpu.CompilerParams(dimension_semantics=sem, vmem_limit_bytes=VMEM_LIMIT)


def _const_spec(shape):
    nd = len(shape)
    return pl.BlockSpec(shape, lambda *_: (0,) * nd, pipeline_mode=pl.Buffered(1))


def _ada_body(c_ref, w_ref, b_ref, o_ref):
    s = _silu(c_ref[...])
    o_ref[...] = _dot(s.astype(BF16), w_ref[...].astype(BF16)) + b_ref[...]


def _ada_call(cond, w, b):
    rows, d = cond.shape
    n = w.shape[1]
    tn = d
    return pl.pallas_call(
        _ada_body,
        grid=(n // tn,),
        in_specs=[
            pl.BlockSpec((rows, d), lambda j: (0, 0)),
            pl.BlockSpec((d, tn), lambda j: (0, j)),
            pl.BlockSpec((1, tn), lambda j: (0, j)),
        ],
        out_specs=pl.BlockSpec((rows, tn), lambda j: (0, j)),
        out_shape=jax.ShapeDtypeStruct((rows, n), F32),
        compiler_params=_cparams(("arbitrary",)),
        name="ada",
    )(cond, w, b)


def _ffn_body(mod_base, with_mix, with_final, *refs):
    refs = list(refs)
    x_ref = refs.pop(0)
    if with_mix:
        mo_ref, ro_ref, wmm_ref, wmr_ref = refs[:4]
        refs = refs[4:]
    mod_ref, g_ref, win_ref, wout_ref = refs[:4]
    refs = refs[4:]
    if with_final:
        gfin_ref = refs.pop(0)
    out_ref, a_scr = refs

    x = x_ref[...]
    if with_mix:
        mix = _dot(mo_ref[...].astype(BF16), wmm_ref[...]) + _dot(ro_ref[...].astype(BF16), wmr_ref[...])
        x = x + mod_ref[0, mod_base - 1:mod_base, :] * mix
    shift = mod_ref[0, mod_base:mod_base + 1, :]
    scale = mod_ref[0, mod_base + 1:mod_base + 2, :]
    gate = mod_ref[0, mod_base + 2:mod_base + 3, :]
    hb = _rms_mod(x, g_ref[...], shift, scale).astype(BF16)
    dff = wout_ref.shape[0]
    for j in range(dff // FFN_CHUNK):
        lo = j * FFN_CHUNK
        gt = _dot(hb, win_ref[:, lo:lo + FFN_CHUNK])
        up = _dot(hb, win_ref[:, dff + lo:dff + lo + FFN_CHUNK])
        a_scr[:, lo:lo + FFN_CHUNK] = (_silu(gt) * up).astype(BF16)
    y = x + (0.5 * gate) * _dot(a_scr[...], wout_ref[...])
    if with_final:
        y = y * lax.rsqrt(jnp.mean(y * y, axis=-1, keepdims=True) + EPS) * gfin_ref[...]
    out_ref[...] = y


def _ffn_call(x, mod3, row_of_tile, mod_base, g, w_in, w_out, mix=None, g_final=None):
    n, d = x.shape
    tm = FFN_TILE
    dff = w_out.shape[0]
    tok = lambda i: (i, 0)
    args = [x]
    specs = [pl.BlockSpec((tm, d), tok)]
    if mix is not None:
        mo, ro, wmm, wmr = mix
        args += [mo, ro, wmm, wmr]
        specs += [pl.BlockSpec((tm, mo.shape[1]), tok), pl.BlockSpec((tm, ro.shape[1]), tok),
                  _const_spec(wmm.shape), _const_spec(wmr.shape)]
    args += [mod3, g, w_in, w_out]
    specs += [pl.BlockSpec((1,) + mod3.shape[1:], lambda i: (row_of_tile(i), 0, 0)),
              _const_spec(g.shape), _const_spec(w_in.shape), _const_spec(w_out.shape)]
    if g_final is not None:
        args.append(g_final)
        specs.append(_const_spec(g_final.shape))
    body = functools.partial(_ffn_body, mod_base, mix is not None, g_final is not None)
    return pl.pallas_call(
        body,
        grid=(n // tm,),
        in_specs=specs,
        out_specs=pl.BlockSpec((tm, d), tok),
        out_shape=jax.ShapeDtypeStruct((n, d), F32),
        scratch_shapes=[pltpu.VMEM((tm, dff), BF16)],
        compiler_params=_cparams(("parallel",)),
        name="ffn_mix" if mix is not None else "ffn",
    )(*args)


def _mixin_body(grid_mode, has_halo, tiles_per_seq, *refs):
    refs = list(refs)
    x_ref = refs.pop(0)
    if has_halo:
        xp_ref, xn_ref = refs[:2]
        refs = refs[2:]
    (mod_ref, g_ref, wqk_ref, wvot_ref, wg_ref, wgt_ref, wr_ref, cw_ref, cb_ref, mu_ref,
     q_ref, k_ref, vt_ref, ot_ref, gc_ref, gr_ref, xr_ref) = refs

    tm = x_ref.shape[0]
    shift = mod_ref[0, 3:4, :]
    scale = mod_ref[0, 4:5, :]
    g = g_ref[...]
    hb = _rms_mod(x_ref[...], g, shift, scale).astype(BF16)
    row = lax.broadcasted_iota(jnp.int32, (tm, 1), 0)
    nqk = 2 * M_WIDTH

    if has_halo:
        i = pl.program_id(0)
        pos = i % tiles_per_seq
        keep_p = (pos > 0).astype(F32)
        keep_n = (pos < tiles_per_seq - 1).astype(F32)
        hp32 = _rms_mod(xp_ref[...], g, shift, scale)
        hn32 = _rms_mod(xn_ref[...], g, shift, scale)
        hp = hp32.astype(BF16)
        hn = hn32.astype(BF16)
        nh = hp32.shape[0]
        edge = 16
        qk_prev = _dot(hp32[nh - edge:, :].astype(BF16), wqk_ref[...])[edge - 1:edge, :] * keep_p
        qk_next = _dot(hn32[:edge, :].astype(BF16), wqk_ref[...])[0:1, :] * keep_n
    else:
        qk_prev = jnp.zeros((1, nqk), F32)
        qk_next = jnp.zeros((1, nqk), F32)

    uqk = _dot(hb, wqk_ref[...])
    u_dn = jnp.where(row == 0, qk_prev, pltpu.roll(uqk, 1, 0))
    u_up = jnp.where(row == tm - 1, qk_next, pltpu.roll(uqk, tm - 1, 0))
    qk = _silu(cw_ref[0:1, :] * u_dn + cw_ref[1:2, :] * uqk + cw_ref[2:3, :] * u_up + cb_ref[...])
    q_ref[...] = qk[:, :M_WIDTH].astype(BF16)
    k_ref[...] = (qk[:, M_WIDTH:] * (M_DK ** -0.5)).astype(BF16)
    vo_t = _dot_nt(wvot_ref[...], hb)
    vt_ref[...] = vo_t[:M_WIDTH, :].astype(BF16)
    ot_ref[...] = vo_t[M_WIDTH:, :]
    gc_ref[...] = _dot(hb, wg_ref[...])
    gr_ref[...] = _dot_nt(wgt_ref[...], hb)

    ur = _dot(hb, wr_ref[...])
    if grid_mode:
        col = row % GRID_W
        left = jnp.where(col == 0, 0.0, pltpu.roll(ur, 1, 0))
        right = jnp.where(col == GRID_W - 1, 0.0, pltpu.roll(ur, tm - 1, 0))
        ur_p = _dot(hp, wr_ref[...]) * keep_p
        ur_n = _dot(hn, wr_ref[...]) * keep_n
        up = jnp.concatenate([ur_p, ur[:tm - GRID_W, :]], axis=0)
        down = jnp.concatenate([ur[GRID_W:, :], ur_n], axis=0)
        xr = (ur + mu_ref[0:1, :] * (left - ur) + mu_ref[1:2, :] * (right - ur)
              + mu_ref[2:3, :] * (up - ur) + mu_ref[3:4, :] * (down - ur))
    else:
        left = jnp.where(row == 0, 0.0, pltpu.roll(ur, 1, 0))
        right = jnp.where(row == tm - 1, 0.0, pltpu.roll(ur, tm - 1, 0))
        xr = ur + mu_ref[0:1, :] * (left - ur) + mu_ref[1:2, :] * (right - ur)
    xr_ref[...] = xr


def _mixin_call(x, seq_len, grid_mode, mod3, row_of_tile, g, wqk, wvot, wg, wgt, wr, cw, cb, mu):
    n, d = x.shape
    if grid_mode:
        tm = MIX_TILE
        has_halo = True
    else:
        tm = seq_len
        has_halo = False
    tiles_per_seq = seq_len // tm
    tok = lambda i: (i, 0)
    args = [x]
    specs = [pl.BlockSpec((tm, d), tok)]
    if has_halo:
        hb = R_CHUNK
        per = tm // hb
        last = n // hb - 1
        args += [x, x]
        specs += [pl.BlockSpec((hb, d), lambda i: (jnp.maximum(i * per - 1, 0), 0)),
                  pl.BlockSpec((hb, d), lambda i: (jnp.minimum((i + 1) * per, last), 0))]
    args += [mod3, g, wqk, wvot, wg, wgt, wr, cw, cb, mu]
    specs += [pl.BlockSpec((1,) + mod3.shape[1:], lambda i: (row_of_tile(i), 0, 0))]
    specs += [_const_spec(a.shape) for a in (g, wqk, wvot, wg, wgt, wr, cw, cb, mu)]
    outs = [jax.ShapeDtypeStruct((n, M_WIDTH), BF16)] * 2 + [
        jax.ShapeDtypeStruct((M_WIDTH, n), BF16),
        jax.ShapeDtypeStruct((M_WIDTH, n), F32),
        jax.ShapeDtypeStruct((n, LANES), F32),
        jax.ShapeDtypeStruct((GATE_COLS, n), F32),
        jax.ShapeDtypeStruct((n, R_COLS), F32),
    ]
    out_specs = [pl.BlockSpec((tm, M_WIDTH), tok)] * 2 + [pl.BlockSpec((M_WIDTH, tm), lambda i: (0, i))] * 2 + [
        pl.BlockSpec((tm, LANES), tok),
        pl.BlockSpec((GATE_COLS, tm), lambda i: (0, i)),
        pl.BlockSpec((tm, R_COLS), tok),
    ]
    body = functools.partial(_mixin_body, grid_mode, has_halo, tiles_per_seq)
    return pl.pallas_call(
        body,
        grid=(n // tm,),
        in_specs=specs,
        out_specs=out_specs,
        out_shape=outs,
        compiler_params=_cparams(("parallel",)),
        name="mix_in_grid" if grid_mode else "mix_in_seq",
    )(*args)


def _mlstm_body(zero_init, nc, *refs):
    refs = list(refs)
    q_ref, k_ref, vt_ref, ot_ref, gc_ref, gr_ref, gbr_ref, gbc_ref, hg_ref = refs[:9]
    refs = refs[9:]
    if not zero_init:
        c0_ref, n0_ref, m0_ref = refs[:3]
        refs = refs[3:]
    out_ref, cs_ref, ns_ref, ms_ref, ct_scr, n_scr, m_scr, acc_scr = refs

    d = pl.program_id(1)
    c = pl.program_id(2)
    L = q_ref.shape[0]
    fwd = d == 0
    heads = range(M_HEADS)
    hsl = [slice(h * M_DV, (h + 1) * M_DV) for h in heads]

    @pl.when(c == 0)
    def _():
        if zero_init:
            ct_scr[...] = jnp.zeros_like(ct_scr)
            n_scr[...] = jnp.zeros_like(n_scr)
            m_scr[...] = jnp.zeros_like(m_scr)
        else:
            for h in heads:
                ct_scr[h] = c0_ref[0, 0, h].T
            n_scr[0:M_HEADS, :] = n0_ref[0, 0]
            m_scr[...] = m0_ref[0, 0]

    ri = lax.broadcasted_iota(jnp.int32, (L, L), 0)
    ci = lax.broadcasted_iota(jnp.int32, (L, L), 1)
    sgn = 1 - 2 * d
    tri = ((ci - ri) * sgn <= 0).astype(BF16)
    mask_st = (ri - ci) * sgn <= 0
    tri_t = mask_st.astype(BF16)

    gcol = gc_ref[...] + gbr_ref[...]
    grow = gr_ref[...] + gbc_ref[...]
    bcol = _sel_dot(tri, jax.nn.log_sigmoid(gcol))
    brow = _dot_sel(jax.nn.log_sigmoid(grow), tri_t)

    def pick_col(a, j0, j1):
        return jnp.where(fwd, a[:, j0:j0 + 1], a[:, j1:j1 + 1])

    def pick_row(a, j0, j1):
        return jnp.where(fwd, a[j0:j0 + 1, :], a[j1:j1 + 1, :])

    cc = c + d * (nc - 1 - 2 * c)
    row0 = lax.broadcasted_iota(jnp.int32, (16, LANES), 0) == 0
    row0_l = lax.broadcasted_iota(jnp.int32, (16, L), 0) == 0

    qb = [q_ref[:, hs] for hs in hsl]
    kb = [k_ref[:, hs] for hs in hsl]
    vt = [vt_ref[hs, :] for hs in hsl]
    ra_c = [pick_col(gcol, h, 4 + h) - pick_col(bcol, 8 + h, 12 + h) for h in heads]
    i_r = [pick_row(grow, h, 4 + h) for h in heads]
    b_r = [pick_row(brow, 8 + h, 12 + h) for h in heads]
    b_tot = [jnp.where(fwd, b_r[h][:, L - 1:L], b_r[h][:, 0:1]) for h in heads]
    m_prev = [m_scr[h:h + 1, 0:1] for h in heads]
    n_prev = [n_scr[h:h + 1, :] for h in heads]
    ct_prev = [ct_scr[h] for h in heads]

    kq = [_dot_nt(kb[h], qb[h]) for h in heads]
    qct = [_dot_nt(ct_prev[h].astype(BF16), qb[h]) for h in heads]
    qn = [_dot_nt(jnp.where(row0, n_prev[h], 0.0).astype(BF16), qb[h])[0:1, :] for h in heads]
    dmt = [jnp.where(mask_st, ra_c[h], -jnp.inf) for h in heads]
    mx = [jnp.maximum(m_prev[h], jnp.max(dmt[h], axis=0, keepdims=True)) for h in heads]
    st = [kq[h] * jnp.exp(dmt[h] - mx[h]) for h in heads]
    sc = [jnp.exp(m_prev[h] - mx[h]) for h in heads]
    numt = [_dot(vt[h], st[h].astype(BF16)) + sc[h] * qct[h] for h in heads]
    den = [jnp.sum(st[h], axis=0, keepdims=True) + sc[h] * qn[h] for h in heads]
    ht = [numt[h] / jnp.maximum(jnp.abs(den[h]), jnp.exp(-(b_r[h] + mx[h]))) for h in heads]

    gs = [b_tot[h] - b_r[h] + i_r[h] for h in heads]
    m_new = [jnp.maximum(b_tot[h] + m_prev[h], jnp.max(gs[h], axis=1, keepdims=True)) for h in heads]
    wk = [jnp.exp(gs[h] - m_new[h]) for h in heads]
    decay = [jnp.exp(b_tot[h] + m_prev[h] - m_new[h]) for h in heads]
    ckv = [_dot((vt[h].astype(F32) * wk[h]).astype(BF16), kb[h]) for h in heads]
    nk = [_dot(jnp.where(row0_l, wk[h], 0.0).astype(BF16), kb[h])[0:1, :] for h in heads]
    for h in heads:
        ct_scr[h] = decay[h] * ct_prev[h] + ckv[h]
        n_scr[h:h + 1, :] = decay[h] * n_prev[h] + nk[h]
        m_scr[h:h + 1, :] = jnp.broadcast_to(m_new[h], (1, LANES))

    @pl.when(fwd)
    def _():
        acc_scr[cc] = jnp.concatenate(ht, axis=0)

    @pl.when(jnp.logical_not(fwd))
    def _():
        outs = []
        for h in heads:
            t = jax.nn.sigmoid(ot_ref[hsl[h], :]) * (acc_scr[cc, hsl[h], :] + ht[h])
            t = t * lax.rsqrt(jnp.mean(t * t, axis=0, keepdims=True) + EPS) * hg_ref[hsl[h], :]
            outs.append(t.T)
        out_ref[...] = jnp.concatenate(outs, axis=1).astype(BF16)

    @pl.when(c == nc - 1)
    def _():
        for h in heads:
            cs_ref[0, 0, h] = ct_scr[h].T
        ns_ref[0, 0] = n_scr[0:M_HEADS, :]
        ms_ref[0, 0] = m_scr[...]


def _mlstm_call(q, k, vt, ot, gc, gr, gate_b_row, gate_b_col, head_g_col, batch, seq_len, init):
    n = q.shape[0]
    L = min(M_CHUNK, seq_len)
    nc = seq_len // L
    zero_init = init is None

    def blk(b, d, c):
        return b * nc + c + d * (nc - 1 - 2 * c)

    tok = lambda b, d, c: (blk(b, d, c), 0)
    tok_t = lambda b, d, c: (0, blk(b, d, c))
    tok_out = lambda b, d, c: (b * nc + nc - 1 - c * d, 0)

    args = [q, k, vt, ot, gc, gr, gate_b_row, gate_b_col, head_g_col]
    specs = [pl.BlockSpec((L, M_WIDTH), tok)] * 2 + [pl.BlockSpec((M_WIDTH, L), tok_t)] * 2 + [
        pl.BlockSpec((L, LANES), tok),
        pl.BlockSpec((GATE_COLS, L), tok_t),
        _const_spec(gate_b_row.shape), _const_spec(gate_b_col.shape), _const_spec(head_g_col.shape),
    ]
    if not zero_init:
        c0, n0, m0 = init
        args += [c0, n0, m0]
        specs += [pl.BlockSpec((1, 1) + c0.shape[2:], lambda b, d, c: (b, d, 0, 0, 0)),
                  pl.BlockSpec((1, 1) + n0.shape[2:], lambda b, d, c: (b, d, 0, 0)),
                  pl.BlockSpec((1, 1) + m0.shape[2:], lambda b, d, c: (b, d, 0, 0))]
    outs = [jax.ShapeDtypeStruct((n, M_WIDTH), BF16),
            jax.ShapeDtypeStruct((batch, 2, M_HEADS, M_DK, M_DV), F32),
            jax.ShapeDtypeStruct((batch, 2, M_HEADS, M_DK), F32),
            jax.ShapeDtypeStruct((batch, 2, 8, LANES), F32)]
    out_specs = [pl.BlockSpec((L, M_WIDTH), tok_out),
                 pl.BlockSpec((1, 1, M_HEADS, M_DK, M_DV), lambda b, d, c: (b, d, 0, 0, 0)),
                 pl.BlockSpec((1, 1, M_HEADS, M_DK), lambda b, d, c: (b, d, 0, 0)),
                 pl.BlockSpec((1, 1, 8, LANES), lambda b, d, c: (b, d, 0, 0))]
    body = functools.partial(_mlstm_body, zero_init, nc)
    return pl.pallas_call(
        body,
        grid=(batch, 2, nc),
        in_specs=specs,
        out_specs=out_specs,
        out_shape=outs,
        scratch_shapes=[pltpu.VMEM((M_HEADS, M_DV, M_DK), F32),
                        pltpu.VMEM((8, LANES), F32),
                        pltpu.VMEM((8, LANES), F32),
                        pltpu.VMEM((nc, M_WIDTH, L), F32)],
        compiler_params=_cparams(("parallel", "arbitrary", "arbitrary")),
        name="mlstm_zero" if zero_init else "mlstm_init",
    )(*args)


def _rwkv_body(zero_init, nb, *refs):
    refs = list(refs)
    (xr_ref, w0_ref, w2_ref, a0_ref, a2_ref, g2_ref, kk_ref, ka_ref, rk_ref,
     lng_ref, lnb_ref, hones_ref) = refs[:12]
    refs = refs[12:]
    if not zero_init:
        h0_ref = refs.pop(0)
    out_ref, hs_ref, h_scr = refs

    d = pl.program_id(1)
    c = pl.program_id(2)
    LB = xr_ref.shape[0]
    L = R_CHUNK
    n_chunks = LB // L
    P = 2 * L
    fwd = d == 0
    sgn = 1 - 2 * d
    pairs = range(R_HEADS // 2)
    cols = [slice(p * LANES, (p + 1) * LANES) for p in pairs]

    @pl.when(c == 0)
    def _():
        if zero_init:
            h_scr[...] = jnp.zeros_like(h_scr)
        else:
            h_scr[...] = h0_ref[0, 0]

    r = xr_ref[:, 0:R_WIDTH]
    k = xr_ref[:, R_WIDTH:2 * R_WIDTH]
    v = xr_ref[:, 2 * R_WIDTH:3 * R_WIDTH]
    wd = xr_ref[:, 3 * R_WIDTH:3 * R_WIDTH + R_LORA]
    ad = xr_ref[:, 3 * R_WIDTH + R_LORA:3 * R_WIDTH + 2 * R_LORA]
    gin = xr_ref[:, 3 * R_WIDTH + 2 * R_LORA:]
    hones = hones_ref[...]

    def head_sum(x):
        return jnp.concatenate([_dot_sel(x[:, cs], hones) for cs in cols], axis=1)

    tw = jnp.tanh(wd).astype(BF16)
    adb = ad.astype(BF16)
    ww = jnp.where(fwd, w0_ref[0:1, :] + _dot(tw, w2_ref[0]), w0_ref[1:2, :] + _dot(tw, w2_ref[1]))
    lw = -jnp.exp(-jax.nn.softplus(-ww) - 0.5)
    a_f = jax.nn.sigmoid(a0_ref[0:1, :] + _dot(adb, a2_ref[0]))
    a_b = jax.nn.sigmoid(a0_ref[1:2, :] + _dot(adb, a2_ref[1]))
    a = jnp.where(fwd, a_f, a_b)
    ka = ka_ref[...]
    kd = k * (1.0 + (a - 1.0) * ka)
    kk = k * kk_ref[...]
    kk = kk / jnp.maximum(jnp.sqrt(head_sum(kk * kk)), 1e-12)
    bvec = kk * a

    ri = lax.broadcasted_iota(jnp.int32, (LB, LB), 0)
    ci = lax.broadcasted_iota(jnp.int32, (LB, LB), 1)
    same_chunk = (ri // L) == (ci // L)
    tri = jnp.logical_and(same_chunk, (ci - ri) * sgn <= 0).astype(BF16)
    cl = _sel_dot(tri, lw)
    tot = _sel_dot(same_chunk.astype(BF16), lw)
    e_in = jnp.exp(cl)
    e_out = jnp.exp(-cl)
    e_end = jnp.exp(tot - cl)
    a_til = -kk * jnp.exp(cl - lw)
    r_til = r * e_in
    b_til = bvec * e_out
    k_til = kd * e_out
    b_hat = bvec * e_end
    k_hat = kd * e_end
    w_end = jnp.exp(tot)

    pr = lax.broadcasted_iota(jnp.int32, (P, P), 0)
    pc = lax.broadcasted_iota(jnp.int32, (P, P), 1)
    same = (pr // L) == (pc // L)
    before = (pc - pr) * sgn < 0
    m_strict = jnp.logical_and(same, before)
    m_incl = jnp.logical_and(same, jnp.logical_or(before, pc == pr))
    eye = pr == pc
    lane = lax.broadcasted_iota(jnp.int32, (L, LANES), 1)
    first = lane < R_HEAD

    def stack(x, j, p):
        x = x[j * L:(j + 1) * L, cols[p]]
        return jnp.concatenate([jnp.where(first, x, 0.0), jnp.where(first, 0.0, x)], axis=0).astype(BF16)

    chains = [(j, p) for j in range(n_chunks) for p in pairs]
    at = [stack(a_til, j, p) for j, p in chains]
    rt = [stack(r_til, j, p) for j, p in chains]
    bt = [stack(b_til, j, p) for j, p in chains]
    kt = [stack(k_til, j, p) for j, p in chains]
    bh = [stack(b_hat, j, p) for j, p in chains]
    kh = [stack(k_hat, j, p) for j, p in chains]
    vs = [stack(v, j, p) for j, p in chains]
    ids = range(len(chains))

    big = [_dot_nt(jnp.concatenate([at[i], rt[i]], axis=0), jnp.concatenate([bt[i], kt[i]], axis=0)) for i in ids]
    a_ab = [jnp.where(m_strict, big[i][:P, :P], 0.0) for i in ids]
    a_ak = [jnp.where(m_strict, big[i][:P, P:], 0.0).astype(BF16) for i in ids]
    a_rb = [jnp.where(m_incl, big[i][P:, :P], 0.0).astype(BF16) for i in ids]
    a_rk = [jnp.where(m_incl, big[i][P:, P:], 0.0).astype(BF16) for i in ids]

    pair2 = (pr // 2) == (pc // 2)
    tinv = [jnp.where(eye, 1.0, jnp.where(pair2, a_ab[i], 0.0)) for i in ids]
    blk = 2
    while blk < L:
        sib = jnp.logical_and((pr // (2 * blk)) == (pc // (2 * blk)), (pr // blk) != (pc // blk))
        tb = [tinv[i].astype(BF16) for i in ids]
        half = [_dot(tb[i], jnp.where(sib, a_ab[i], 0.0).astype(BF16)) for i in ids]
        tinv = [tinv[i] + _dot(half[i].astype(BF16), tb[i]) for i in ids]
        blk *= 2

    av = [_dot(a_ak[i], vs[i]) for i in ids]
    pq = [_dot(tinv[i].astype(BF16), jnp.concatenate([at[i], av[i].astype(BF16)], axis=1)).astype(BF16) for i in ids]
    ry = [_dot(a_rb[i], pq[i]) for i in ids]
    rkv = [_dot(a_rk[i], vs[i]) for i in ids]
    mg = [_dot_tn(bh[i], pq[i]) for i in ids]
    kv = [_dot_tn(kh[i], vs[i]) for i in ids]
    r_hat = [(rt[i].astype(F32) + ry[i][:, :LANES]).astype(BF16) for i in ids]
    y0 = [ry[i][:, LANES:] + rkv[i] for i in ids]
    m_corr = [mg[i][:, :LANES].astype(BF16) for i in ids]
    g_add = [mg[i][:, LANES:] + kv[i] for i in ids]
    w_col = [jnp.sum(jnp.where(eye, jnp.broadcast_to(w_end[j * L:j * L + 1, cols[p]], (P, P)), 0.0),
                     axis=1, keepdims=True) for j, p in chains]

    def scan_chunks(order):
        y = [None] * n_chunks
        for j in order:
            ys = []
            for p in pairs:
                i = j * len(pairs) + p
                hprev = h_scr[p]
                hb = hprev.astype(BF16)
                yst = _dot(r_hat[i], hb) + y0[i]
                ys.append(yst[:L, :] + yst[L:, :])
                h_scr[p] = w_col[i] * hprev + _dot(m_corr[i], hb) + g_add[i]
            y[j] = jnp.concatenate(ys, axis=1)
        return jnp.concatenate(y, axis=0)

    cc = c + d * (nb - 1 - 2 * c)
    rows = pl.ds(pl.multiple_of(cc * LB, LB), LB)

    @pl.when(fwd)
    def _():
        out_ref[rows, :] = scan_chunks(range(n_chunks))

    @pl.when(jnp.logical_not(fwd))
    def _():
        inv_n = 1.0 / R_HEAD
        ysum = out_ref[rows, :] + scan_chunks(reversed(range(n_chunks)))
        mean = head_sum(ysum) * inv_n
        yc = ysum - mean
        var = head_sum(yc * yc) * inv_n
        yn = yc * lax.rsqrt(var + R_LN_EPS) * lng_ref[...] + lnb_ref[...]
        k_bar = k * (1.0 + (0.5 * (a_f + a_b) - 1.0) * ka)
        bonus = head_sum(r * k_bar * rk_ref[...]) * v
        gate = _dot(jax.nn.sigmoid(gin).astype(BF16), g2_ref[...])
        out_ref[rows, :] = (yn + bonus) * gate

    @pl.when(c == nb - 1)
    def _():
        hs_ref[0, 0] = h_scr[...]


def _rwkv_call(xr, w0, w2p, a0, a2p, g2, k_k, k_a, r_k, ln_g, ln_b, hones, batch, seq_len, h0):
    n = xr.shape[0]
    LB = R_CHUNK * R_CHUNKS_PER_STEP
    nb = seq_len // LB
    zero_init = h0 is None
    n_pairs = R_HEADS // 2

    def tok(b, d, c):
        return (b * nb + c + d * (nb - 1 - 2 * c), 0)

    consts = [w0, w2p, a0, a2p, g2, k_k, k_a, r_k, ln_g, ln_b, hones]
    args = [xr] + consts
    specs = [pl.BlockSpec((LB, R_COLS), tok)] + [_const_spec(a.shape) for a in consts]
    if not zero_init:
        args.append(h0)
        specs.append(pl.BlockSpec((1, 1, n_pairs, LANES, LANES), lambda b, d, c: (b, d, 0, 0, 0)))
    outs = [jax.ShapeDtypeStruct((n, R_WIDTH), F32),
            jax.ShapeDtypeStruct((batch, 2, n_pairs, LANES, LANES), F32)]
    out_specs = [pl.BlockSpec((seq_len, R_WIDTH), lambda b, d, c: (b, 0)),
                 pl.BlockSpec((1, 1, n_pairs, LANES, LANES), lambda b, d, c: (b, d, 0, 0, 0))]
    body = functools.partial(_rwkv_body, zero_init, nb)
    return pl.pallas_call(
        body,
        grid=(batch, 2, nb),
        in_specs=specs,
        out_specs=out_specs,
        out_shape=outs,
        scratch_shapes=[pltpu.VMEM((n_pairs, LANES, LANES), F32)],
        compiler_params=_cparams(("parallel", "arbitrary", "arbitrary")),
        name="rwkv_zero" if zero_init else "rwkv_init",
    )(*args)


def _pairs_from_heads(s):
    ht = jnp.swapaxes(s, -1, -2)
    lead = ht.shape[:-3]
    ht = ht.reshape(lead + (R_HEADS // 2, 2, R_HEAD, R_HEAD))
    z = jnp.zeros_like(ht[..., 0, :, :])
    top = jnp.concatenate([ht[..., 0, :, :], z], axis=-1)
    bot = jnp.concatenate([z, ht[..., 1, :, :]], axis=-1)
    return jnp.concatenate([top, bot], axis=-2)


def _heads_from_pairs(hp):
    a = hp[..., :R_HEAD, :R_HEAD]
    b = hp[..., R_HEAD:, R_HEAD:]
    s = jnp.stack([a, b], axis=-3)
    s = s.reshape(hp.shape[:-3] + (R_HEADS, R_HEAD, R_HEAD))
    return jnp.swapaxes(s, -1, -2)


def _trunk(x, mod3, mod_row0, per_seq_rows, grid_mode, init, w):
    batch, seq_len, d = x.shape
    x2 = x.reshape(batch * seq_len, d)

    def rows_for(tile):
        per = seq_len // tile
        if per_seq_rows:
            return lambda i: mod_row0 + i // per
        return lambda i: mod_row0

    x1 = _ffn_call(x2, mod3, rows_for(FFN_TILE), 0, w["norm_ffn1"], w["ffn1_in"], w["ffn1_out"])
    mix_tile = MIX_TILE if grid_mode else seq_len
    q, k, vt, ot, gc, gr, xr = _mixin_call(
        x1, seq_len, grid_mode, mod3, rows_for(mix_tile), w["norm_mix"], w["wqk"], w["wvot"], w["wg"], w["wgt"],
        w["wr"], w["conv_w"], w["conv_b"], w["mu"])
    if init is None:
        m_init = None
        r_init = None
    else:
        c0, n0, m0, s0 = init
        m0p = jnp.broadcast_to(jnp.pad(m0, ((0, 0), (0, 0), (0, 8 - M_HEADS)))[..., None], m0.shape[:2] + (8, LANES))
        m_init = (c0, n0, m0p)
        r_init = _pairs_from_heads(s0)
    mo, cs, ns, ms = _mlstm_call(q, k, vt, ot, gc, gr, w["gate_b_row"], w["gate_b_col"], w["head_g_col"],
                                 batch, seq_len, m_init)
    ro, hs = _rwkv_call(xr, w["r_w0"], w["r_w2p"], w["r_a0"], w["r_a2p"], w["r_g2"], w["r_k_k"], w["r_k_a"],
                        w["r_r_k"], w["r_ln_g"], w["r_ln_b"], w["hones"], batch, seq_len, r_init)
    y = _ffn_call(x1, mod3, rows_for(FFN_TILE), 6, w["norm_ffn2"], w["ffn2_in"], w["ffn2_out"],
                  mix=(mo, ro, w["wo_m"], w["wo_r"]), g_final=w["norm_final"])
    states = (cs, ns, ms[:, :, :M_HEADS, 0], _heads_from_pairs(hs))
    return y.reshape(batch, seq_len, d), states


def _prepare_weights(ada_w, ada_b, norm_ffn1, ffn1_w_in, ffn1_w_out, norm_mix, mix_w_in, mix_w_out,
                     m_conv_w, m_conv_b, m_gate_b, m_head_g, r_mu, r_w0, r_w2, r_a0, r_a2, r_g2,
                     r_k_k, r_k_a, r_r_k, r_ln_g, r_ln_b, norm_ffn2, ffn2_w_in, ffn2_w_out, norm_final):
    assert ada_w.shape[0] == 1, "single trunk layer"
    lora = r_w2.shape[2]
    nm = 4 * M_WIDTH

    w_in = mix_w_in[0]
    gate_w = w_in[:, nm:nm + GATE_COLS]
    zpad = jnp.zeros((lora, R_WIDTH), F32)

    def dir_pad(w2):
        return jnp.stack([jnp.concatenate([w2[0], zpad], axis=0), jnp.concatenate([zpad, w2[1]], axis=0)])

    head_id = jnp.arange(LANES) // R_HEAD
    return dict(
        norm_ffn1=norm_ffn1, norm_mix=norm_mix, norm_ffn2=norm_ffn2, norm_final=norm_final[None],
        ffn1_in=ffn1_w_in[0].astype(BF16), ffn1_out=ffn1_w_out[0].astype(BF16),
        ffn2_in=ffn2_w_in[0].astype(BF16), ffn2_out=ffn2_w_out[0].astype(BF16),
        wqk=w_in[:, :2 * M_WIDTH].astype(BF16), wvot=w_in[:, 2 * M_WIDTH:nm].T.astype(BF16),
        wg=jnp.pad(gate_w, ((0, 0), (0, LANES - GATE_COLS))).astype(BF16),
        wgt=gate_w.T.astype(BF16),
        wr=w_in[:, nm + GATE_COLS:].astype(BF16),
        wo_m=mix_w_out[0, :M_WIDTH].astype(BF16), wo_r=mix_w_out[0, M_WIDTH:].astype(BF16),
        conv_w=m_conv_w[0], conv_b=m_conv_b, mu=r_mu[0],
        gate_b_row=jnp.pad(m_gate_b[0].reshape(1, GATE_COLS), ((0, 0), (0, LANES - GATE_COLS))),
        gate_b_col=m_gate_b[0].reshape(GATE_COLS, 1),
        head_g_col=m_head_g.reshape(M_WIDTH, 1),
        r_w0=r_w0[0], r_w2p=dir_pad(r_w2[0]).astype(BF16), r_a0=r_a0[0], r_a2p=dir_pad(r_a2[0]).astype(BF16),
        r_g2=r_g2[0].astype(BF16), r_k_k=r_k_k, r_k_a=r_k_a, r_r_k=r_r_k[0].reshape(1, R_WIDTH),
        r_ln_g=r_ln_g, r_ln_b=r_ln_b,
        hones=(head_id[:, None] == head_id[None, :]).astype(BF16),
    )


def kernel(x_prompt, x_sample, c, state_mlstm_C, state_mlstm_n, state_mlstm_m, state_rwkv_S, c_ctx,
           ada_w, ada_b, norm_ffn1, ffn1_w_in, ffn1_w_out, norm_mix, mix_w_in, mix_w_out,
           m_conv_w, m_conv_b, m_gate_b, m_head_g, r_mu, r_w0, r_w2, r_a0, r_a2, r_g2,
           r_k_k, r_k_a, r_r_k, r_ln_g, r_ln_b, norm_ffn2, ffn2_w_in, ffn2_w_out, norm_final):
    w = _prepare_weights(ada_w, ada_b, norm_ffn1, ffn1_w_in, ffn1_w_out, norm_mix, mix_w_in, mix_w_out,
                         m_conv_w, m_conv_b, m_gate_b, m_head_g, r_mu, r_w0, r_w2, r_a0, r_a2, r_g2,
                         r_k_k, r_k_a, r_r_k, r_ln_g, r_ln_b, norm_ffn2, ffn2_w_in, ffn2_w_out, norm_final)
    d = x_prompt.shape[-1]
    dec_batch = x_sample.shape[0]
    cond = jnp.concatenate([c_ctx[None], c, jnp.zeros((16 - 1 - dec_batch, d), F32)], axis=0)
    mod3 = _ada_call(cond, ada_w[0], ada_b).reshape(16, 9, d)

    y_prompt, (cs, ns, ms, ss) = _trunk(x_prompt, mod3, 0, False, False, None, w)
    init = (state_mlstm_C[:, 0], state_mlstm_n[:, 0], state_mlstm_m[:, 0], state_rwkv_S[:, 0])
    y_sample, _ = _trunk(x_sample, mod3, 1, True, True, init, w)
    return (y_prompt, y_sample, cs[:, None], ns[:, None], ms[:, None], ss[:, None])
```

```python
import functools

import jax
import jax.numpy as jnp
from jax import lax
from jax.experimental import pallas as pl
from jax.experimental.pallas import tpu as pltpu

F32 = jnp.float32
BF16 = jnp.bfloat16

EPS = 1e-6
R_LN_EPS = 64e-5
GRID_W = 64
M_HEADS = 4
M_DK = 128
M_DV = 128
M_WIDTH = M_HEADS * M_DV
R_HEADS = 8
R_HEAD = 64
R_WIDTH = R_HEADS * R_HEAD
R_LORA = 128
R_COLS = 3 * R_WIDTH + 3 * R_LORA
GATE_COLS = 16
LANES = 128

FFN_TILE = 512
FFN_CHUNK = 256
MIX_TILE = 512
M_CHUNK = 256
R_CHUNK = 64
R_CHUNKS_PER_STEP = 4
VMEM_LIMIT = 56 * 1024 * 1024


def _dot(a, b):
    return jnp.dot(a, b, preferred_element_type=F32)


def _dot_nt(a, b):
    return lax.dot_general(a, b, (((1,), (1,)), ((), ())), preferred_element_type=F32)


def _dot_tn(a, b):
    return lax.dot_general(a, b, (((0,), (0,)), ((), ())), preferred_element_type=F32)


def _split(x, parts):
    pieces = []
    for _ in range(parts - 1):
        p = x.astype(BF16)
        pieces.append(p)
        x = x - p.astype(F32)
    pieces.append(x.astype(BF16))
    return pieces


def _sel_dot(mat, x, parts=3):
    return sum(_dot(mat, p) for p in _split(x, parts))


def _dot_sel(x, mat, parts=3):
    return sum(_dot(p, mat) for p in _split(x, parts))


def _silu(x):
    return x * jax.nn.sigmoid(x)


def _rms_mod(x, g, shift, scale):
    y = x * lax.rsqrt(jnp.mean(x * x, axis=-1, keepdims=True) + EPS) * g
    return y * (1.0 + scale) + shift


def _cparams(sem):
    return pltpu.CompilerParams(dimension_semantics=sem, vmem_limit_bytes=VMEM_LIMIT)


def _const_spec(shape):
    nd = len(shape)
    return pl.BlockSpec(shape, lambda *_: (0,) * nd, pipeline_mode=pl.Buffered(1))


def _ada_body(c_ref, w_ref, b_ref, o_ref):
    s = _silu(c_ref[...])
    o_ref[...] = _dot(s.astype(BF16), w_ref[...].astype(BF16)) + b_ref[...]


def _ada_call(cond, w, b):
    rows, d = cond.shape
    n = w.shape[1]
    tn = d
    return pl.pallas_call(
        _ada_body,
        grid=(n // tn,),
        in_specs=[
            pl.BlockSpec((rows, d), lambda j: (0, 0)),
            pl.BlockSpec((d, tn), lambda j: (0, j)),
            pl.BlockSpec((1, tn), lambda j: (0, j)),
        ],
        out_specs=pl.BlockSpec((rows, tn), lambda j: (0, j)),
        out_shape=jax.ShapeDtypeStruct((rows, n), F32),
        compiler_params=_cparams(("arbitrary",)),
        name="ada",
    )(cond, w, b)


def _ffn_body(mod_base, with_mix, with_final, *refs):
    refs = list(refs)
    x_ref = refs.pop(0)
    if with_mix:
        mo_ref, ro_ref, wmm_ref, wmr_ref = refs[:4]
        refs = refs[4:]
    mod_ref, g_ref, win_ref, wout_ref = refs[:4]
    refs = refs[4:]
    if with_final:
        gfin_ref = refs.pop(0)
    out_ref, a_scr = refs

    x = x_ref[...]
    if with_mix:
        mix = _dot(mo_ref[...].astype(BF16), wmm_ref[...]) + _dot(ro_ref[...].astype(BF16), wmr_ref[...])
        x = x + mod_ref[0, mod_base - 1:mod_base, :] * mix
    shift = mod_ref[0, mod_base:mod_base + 1, :]
    scale = mod_ref[0, mod_base + 1:mod_base + 2, :]
    gate = mod_ref[0, mod_base + 2:mod_base + 3, :]
    hb = _rms_mod(x, g_ref[...], shift, scale).astype(BF16)
    dff = wout_ref.shape[0]
    for j in range(dff // FFN_CHUNK):
        lo = j * FFN_CHUNK
        gt = _dot(hb, win_ref[:, lo:lo + FFN_CHUNK])
        up = _dot(hb, win_ref[:, dff + lo:dff + lo + FFN_CHUNK])
        a_scr[:, lo:lo + FFN_CHUNK] = (_silu(gt) * up).astype(BF16)
    y = x + (0.5 * gate) * _dot(a_scr[...], wout_ref[...])
    if with_final:
        y = y * lax.rsqrt(jnp.mean(y * y, axis=-1, keepdims=True) + EPS) * gfin_ref[...]
    out_ref[...] = y


def _ffn_call(x, mod3, row_of_tile, mod_base, g, w_in, w_out, mix=None, g_final=None):
    n, d = x.shape
    tm = FFN_TILE
    dff = w_out.shape[0]
    tok = lambda i: (i, 0)
    args = [x]
    specs = [pl.BlockSpec((tm, d), tok)]
    if mix is not None:
        mo, ro, wmm, wmr = mix
        args += [mo, ro, wmm, wmr]
        specs += [pl.BlockSpec((tm, mo.shape[1]), tok), pl.BlockSpec((tm, ro.shape[1]), tok),
                  _const_spec(wmm.shape), _const_spec(wmr.shape)]
    args += [mod3, g, w_in, w_out]
    specs += [pl.BlockSpec((1,) + mod3.shape[1:], lambda i: (row_of_tile(i), 0, 0)),
              _const_spec(g.shape), _const_spec(w_in.shape), _const_spec(w_out.shape)]
    if g_final is not None:
        args.append(g_final)
        specs.append(_const_spec(g_final.shape))
    body = functools.partial(_ffn_body, mod_base, mix is not None, g_final is not None)
    return pl.pallas_call(
        body,
        grid=(n // tm,),
        in_specs=specs,
        out_specs=pl.BlockSpec((tm, d), tok),
        out_shape=jax.ShapeDtypeStruct((n, d), F32),
        scratch_shapes=[pltpu.VMEM((tm, dff), BF16)],
        compiler_params=_cparams(("parallel",)),
        name="ffn_mix" if mix is not None else "ffn",
    )(*args)


def _mixin_body(grid_mode, has_halo, tiles_per_seq, *refs):
    refs = list(refs)
    x_ref = refs.pop(0)
    if has_halo:
        xp_ref, xn_ref = refs[:2]
        refs = refs[2:]
    (mod_ref, g_ref, wqk_ref, wvot_ref, wg_ref, wgt_ref, wr_ref, cw_ref, cb_ref, mu_ref,
     q_ref, k_ref, vt_ref, ot_ref, gc_ref, gr_ref, xr_ref) = refs

    tm = x_ref.shape[0]
    shift = mod_ref[0, 3:4, :]
    scale = mod_ref[0, 4:5, :]
    g = g_ref[...]
    hb = _rms_mod(x_ref[...], g, shift, scale).astype(BF16)
    row = lax.broadcasted_iota(jnp.int32, (tm, 1), 0)
    nqk = 2 * M_WIDTH

    if has_halo:
        i = pl.program_id(0)
        pos = i % tiles_per_seq
        keep_p = (pos > 0).astype(F32)
        keep_n = (pos < tiles_per_seq - 1).astype(F32)
        hp32 = _rms_mod(xp_ref[...], g, shift, scale)
        hn32 = _rms_mod(xn_ref[...], g, shift, scale)
        hp = hp32.astype(BF16)
        hn = hn32.astype(BF16)
        nh = hp32.shape[0]
        edge = 16
        qk_prev = _dot(hp32[nh - edge:, :].astype(BF16), wqk_ref[...])[edge - 1:edge, :] * keep_p
        qk_next = _dot(hn32[:edge, :].astype(BF16), wqk_ref[...])[0:1, :] * keep_n
    else:
        qk_prev = jnp.zeros((1, nqk), F32)
        qk_next = jnp.zeros((1, nqk), F32)

    uqk = _dot(hb, wqk_ref[...])
    ur = _dot(hb, wr_ref[...])
    u_dn = jnp.where(row == 0, qk_prev, pltpu.roll(uqk, 1, 0))
    u_up = jnp.where(row == tm - 1, qk_next, pltpu.roll(uqk, tm - 1, 0))
    qk = _silu(cw_ref[0:1, :] * u_dn + cw_ref[1:2, :] * uqk + cw_ref[2:3, :] * u_up + cb_ref[...])
    q_ref[...] = qk[:, :M_WIDTH].astype(BF16)
    k_ref[...] = (qk[:, M_WIDTH:] * (M_DK ** -0.5)).astype(BF16)
    vo_t = _dot_nt(wvot_ref[...], hb)
    gc_ref[...] = _dot(hb, wg_ref[...])
    gr_ref[...] = _dot_nt(wgt_ref[...], hb)

    if grid_mode:
        col = row % GRID_W
        left = jnp.where(col == 0, 0.0, pltpu.roll(ur, 1, 0))
        right = jnp.where(col == GRID_W - 1, 0.0, pltpu.roll(ur, tm - 1, 0))
        ur_p = _dot(hp, wr_ref[...]) * keep_p
        ur_n = _dot(hn, wr_ref[...]) * keep_n
        up = jnp.concatenate([ur_p, ur[:tm - GRID_W, :]], axis=0)
        down = jnp.concatenate([ur[GRID_W:, :], ur_n], axis=0)
        mu_self = 1.0 - (mu_ref[0:1, :] + mu_ref[1:2, :] + mu_ref[2:3, :] + mu_ref[3:4, :])
        xr = (mu_self * ur + mu_ref[0:1, :] * left + mu_ref[1:2, :] * right
              + mu_ref[2:3, :] * up + mu_ref[3:4, :] * down)
    else:
        left = jnp.where(row == 0, 0.0, pltpu.roll(ur, 1, 0))
        right = jnp.where(row == tm - 1, 0.0, pltpu.roll(ur, tm - 1, 0))
        mu_self = 1.0 - (mu_ref[0:1, :] + mu_ref[1:2, :])
        xr = mu_self * ur + mu_ref[0:1, :] * left + mu_ref[1:2, :] * right
    xr_ref[...] = xr
    vt_ref[...] = vo_t[:M_WIDTH, :].astype(BF16)
    ot_ref[...] = vo_t[M_WIDTH:, :]


def _mixin_call(x, seq_len, grid_mode, mod3, row_of_tile, g, wqk, wvot, wg, wgt, wr, cw, cb, mu):
    n, d = x.shape
    if grid_mode:
        tm = MIX_TILE
        has_halo = True
    else:
        tm = seq_len
        has_halo = False
    tiles_per_seq = seq_len // tm
    tok = lambda i: (i, 0)
    args = [x]
    specs = [pl.BlockSpec((tm, d), tok)]
    if has_halo:
        hb = R_CHUNK
        per = tm // hb
        last = n // hb - 1
        args += [x, x]
        specs += [pl.BlockSpec((hb, d), lambda i: (jnp.maximum(i * per - 1, 0), 0)),
                  pl.BlockSpec((hb, d), lambda i: (jnp.minimum((i + 1) * per, last), 0))]
    args += [mod3, g, wqk, wvot, wg, wgt, wr, cw, cb, mu]
    specs += [pl.BlockSpec((1,) + mod3.shape[1:], lambda i: (row_of_tile(i), 0, 0))]
    specs += [_const_spec(a.shape) for a in (g, wqk, wvot, wg, wgt, wr, cw, cb, mu)]
    outs = [jax.ShapeDtypeStruct((n, M_WIDTH), BF16)] * 2 + [
        jax.ShapeDtypeStruct((M_WIDTH, n), BF16),
        jax.ShapeDtypeStruct((M_WIDTH, n), F32),
        jax.ShapeDtypeStruct((n, LANES), F32),
        jax.ShapeDtypeStruct((GATE_COLS, n), F32),
        jax.ShapeDtypeStruct((n, R_COLS), F32),
    ]
    out_specs = [pl.BlockSpec((tm, M_WIDTH), tok)] * 2 + [pl.BlockSpec((M_WIDTH, tm), lambda i: (0, i))] * 2 + [
        pl.BlockSpec((tm, LANES), tok),
        pl.BlockSpec((GATE_COLS, tm), lambda i: (0, i)),
        pl.BlockSpec((tm, R_COLS), tok),
    ]
    body = functools.partial(_mixin_body, grid_mode, has_halo, tiles_per_seq)
    return pl.pallas_call(
        body,
        grid=(n // tm,),
        in_specs=specs,
        out_specs=out_specs,
        out_shape=outs,
        compiler_params=_cparams(("parallel",)),
        name="mix_in_grid" if grid_mode else "mix_in_seq",
    )(*args)


def _mlstm_body(zero_init, nc, *refs):
    refs = list(refs)
    q_ref, k_ref, vt_ref, ot_ref, gc_ref, gr_ref, gbr_ref, gbc_ref, hg_ref = refs[:9]
    refs = refs[9:]
    if not zero_init:
        c0_ref, n0_ref, m0_ref = refs[:3]
        refs = refs[3:]
    out_ref, cs_ref, ns_ref, ms_ref, ct_scr, n_scr, m_scr, acc_scr = refs

    d = pl.program_id(1)
    c = pl.program_id(2)
    L = q_ref.shape[0]
    fwd = d == 0
    heads = range(M_HEADS)
    hsl = [slice(h * M_DV, (h + 1) * M_DV) for h in heads]

    @pl.when(c == 0)
    def _():
        if zero_init:
            ct_scr[...] = jnp.zeros_like(ct_scr)
            n_scr[...] = jnp.zeros_like(n_scr)
            m_scr[...] = jnp.zeros_like(m_scr)
        else:
            for h in heads:
                ct_scr[h] = c0_ref[0, 0, h].T
            n_scr[0:M_HEADS, :] = n0_ref[0, 0]
            m_scr[...] = m0_ref[0, 0]

    ri = lax.broadcasted_iota(jnp.int32, (L, L), 0)
    ci = lax.broadcasted_iota(jnp.int32, (L, L), 1)
    sgn = 1 - 2 * d
    tri = ((ci - ri) * sgn <= 0).astype(BF16)
    mask_st = (ri - ci) * sgn <= 0
    tri_t = mask_st.astype(BF16)

    gcol = gc_ref[...] + gbr_ref[...]
    grow = gr_ref[...] + gbc_ref[...]
    bcol = _sel_dot(tri, jax.nn.log_sigmoid(gcol))
    brow = _dot_sel(jax.nn.log_sigmoid(grow), tri_t)

    def pick_col(a, j0, j1):
        return jnp.where(fwd, a[:, j0:j0 + 1], a[:, j1:j1 + 1])

    def pick_row(a, j0, j1):
        return jnp.where(fwd, a[j0:j0 + 1, :], a[j1:j1 + 1, :])

    cc = c + d * (nc - 1 - 2 * c)
    row0 = lax.broadcasted_iota(jnp.int32, (16, LANES), 0) == 0
    row0_l = lax.broadcasted_iota(jnp.int32, (16, L), 0) == 0

    qb = [q_ref[:, hs] for hs in hsl]
    kb = [k_ref[:, hs] for hs in hsl]
    vt = [vt_ref[hs, :] for hs in hsl]
    ra_c = [pick_col(gcol, h, 4 + h) - pick_col(bcol, 8 + h, 12 + h) for h in heads]
    i_r = [pick_row(grow, h, 4 + h) for h in heads]
    b_r = [pick_row(brow, 8 + h, 12 + h) for h in heads]
    b_tot = [jnp.where(fwd, b_r[h][:, L - 1:L], b_r[h][:, 0:1]) for h in heads]
    m_prev = [m_scr[h:h + 1, 0:1] for h in heads]
    n_prev = [n_scr[h:h + 1, :] for h in heads]
    ct_prev = [ct_scr[h] for h in heads]

    kq = [_dot_nt(kb[h], qb[h]) for h in heads]
    qct = [_dot_nt(ct_prev[h].astype(BF16), qb[h]) for h in heads]
    qn = [_dot_nt(jnp.where(row0, n_prev[h], 0.0).astype(BF16), qb[h])[0:1, :] for h in heads]
    dmt = [jnp.where(mask_st, ra_c[h], -jnp.inf) for h in heads]
    mx = [jnp.maximum(m_prev[h], jnp.max(dmt[h], axis=0, keepdims=True)) for h in heads]
    st = [kq[h] * jnp.exp(dmt[h] - mx[h]) for h in heads]
    sc = [jnp.exp(m_prev[h] - mx[h]) for h in heads]
    numt = [_dot(vt[h], st[h].astype(BF16)) + sc[h] * qct[h] for h in heads]
    den = [jnp.sum(st[h], axis=0, keepdims=True) + sc[h] * qn[h] for h in heads]
    ht = [numt[h] / jnp.maximum(jnp.abs(den[h]), jnp.exp(-(b_r[h] + mx[h]))) for h in heads]

    gs = [b_tot[h] - b_r[h] + i_r[h] for h in heads]
    m_new = [jnp.maximum(b_tot[h] + m_prev[h], jnp.max(gs[h], axis=1, keepdims=True)) for h in heads]
    wk = [jnp.exp(gs[h] - m_new[h]) for h in heads]
    decay = [jnp.exp(b_tot[h] + m_prev[h] - m_new[h]) for h in heads]
    ckv = [_dot((vt[h].astype(F32) * wk[h]).astype(BF16), kb[h]) for h in heads]
    nk = [_dot(jnp.where(row0_l, wk[h], 0.0).astype(BF16), kb[h])[0:1, :] for h in heads]
    for h in heads:
        ct_scr[h] = decay[h] * ct_prev[h] + ckv[h]
        n_scr[h:h + 1, :] = decay[h] * n_prev[h] + nk[h]
        m_scr[h:h + 1, :] = jnp.broadcast_to(m_new[h], (1, LANES))

    @pl.when(fwd)
    def _():
        acc_scr[cc] = jnp.concatenate(ht, axis=0)

    @pl.when(jnp.logical_not(fwd))
    def _():
        outs = []
        for h in heads:
            t = jax.nn.sigmoid(ot_ref[hsl[h], :]) * (acc_scr[cc, hsl[h], :] + ht[h])
            t = t * lax.rsqrt(jnp.mean(t * t, axis=0, keepdims=True) + EPS) * hg_ref[hsl[h], :]
            outs.append(t.T)
        out_ref[...] = jnp.concatenate(outs, axis=1).astype(BF16)

    @pl.when(c == nc - 1)
    def _():
        for h in heads:
            cs_ref[0, 0, h] = ct_scr[h].T
        ns_ref[0, 0] = n_scr[0:M_HEADS, :]
        ms_ref[0, 0] = m_scr[...]


def _mlstm_call(q, k, vt, ot, gc, gr, gate_b_row, gate_b_col, head_g_col, batch, seq_len, init):
    n = q.shape[0]
    L = min(M_CHUNK, seq_len)
    nc = seq_len // L
    zero_init = init is None

    def blk(b, d, c):
        return b * nc + c + d * (nc - 1 - 2 * c)

    tok = lambda b, d, c: (blk(b, d, c), 0)
    tok_t = lambda b, d, c: (0, blk(b, d, c))
    tok_out = lambda b, d, c: (b * nc + nc - 1 - c * d, 0)

    args = [q, k, vt, ot, gc, gr, gate_b_row, gate_b_col, head_g_col]
    specs = [pl.BlockSpec((L, M_WIDTH), tok)] * 2 + [pl.BlockSpec((M_WIDTH, L), tok_t)] * 2 + [
        pl.BlockSpec((L, LANES), tok),
        pl.BlockSpec((GATE_COLS, L), tok_t),
        _const_spec(gate_b_row.shape), _const_spec(gate_b_col.shape), _const_spec(head_g_col.shape),
    ]
    if not zero_init:
        c0, n0, m0 = init
        args += [c0, n0, m0]
        specs += [pl.BlockSpec((1, 1) + c0.shape[2:], lambda b, d, c: (b, d, 0, 0, 0)),
                  pl.BlockSpec((1, 1) + n0.shape[2:], lambda b, d, c: (b, d, 0, 0)),
                  pl.BlockSpec((1, 1) + m0.shape[2:], lambda b, d, c: (b, d, 0, 0))]
    outs = [jax.ShapeDtypeStruct((n, M_WIDTH), BF16),
            jax.ShapeDtypeStruct((batch, 2, M_HEADS, M_DK, M_DV), F32),
            jax.ShapeDtypeStruct((batch, 2, M_HEADS, M_DK), F32),
            jax.ShapeDtypeStruct((batch, 2, 8, LANES), F32)]
    out_specs = [pl.BlockSpec((L, M_WIDTH), tok_out),
                 pl.BlockSpec((1, 1, M_HEADS, M_DK, M_DV), lambda b, d, c: (b, d, 0, 0, 0)),
                 pl.BlockSpec((1, 1, M_HEADS, M_DK), lambda b, d, c: (b, d, 0, 0)),
                 pl.BlockSpec((1, 1, 8, LANES), lambda b, d, c: (b, d, 0, 0))]
    body = functools.partial(_mlstm_body, zero_init, nc)
    return pl.pallas_call(
        body,
        grid=(batch, 2, nc),
        in_specs=specs,
        out_specs=out_specs,
        out_shape=outs,
        scratch_shapes=[pltpu.VMEM((M_HEADS, M_DV, M_DK), F32),
                        pltpu.VMEM((8, LANES), F32),
                        pltpu.VMEM((8, LANES), F32),
                        pltpu.VMEM((nc, M_WIDTH, L), F32)],
        compiler_params=_cparams(("parallel", "arbitrary", "arbitrary")),
        name="mlstm_zero" if zero_init else "mlstm_init",
    )(*args)


def _rwkv_body(zero_init, nb, *refs):
    refs = list(refs)
    (xr_ref, w0_ref, w2_ref, a0_ref, a2_ref, g2_ref, kk_ref, ka_ref, rk_ref,
     lng_ref, lnb_ref, hones_ref) = refs[:12]
    refs = refs[12:]
    if not zero_init:
        h0_ref = refs.pop(0)
    out_ref, hs_ref, h_scr = refs

    d = pl.program_id(1)
    c = pl.program_id(2)
    LB = xr_ref.shape[0]
    L = R_CHUNK
    n_chunks = LB // L
    P = 2 * L
    fwd = d == 0
    sgn = 1 - 2 * d
    pairs = range(R_HEADS // 2)
    cols = [slice(p * LANES, (p + 1) * LANES) for p in pairs]

    @pl.when(c == 0)
    def _():
        if zero_init:
            h_scr[...] = jnp.zeros_like(h_scr)
        else:
            h_scr[...] = h0_ref[0, 0]

    r = xr_ref[:, 0:R_WIDTH]
    k = xr_ref[:, R_WIDTH:2 * R_WIDTH]
    v = xr_ref[:, 2 * R_WIDTH:3 * R_WIDTH]
    wd = xr_ref[:, 3 * R_WIDTH:3 * R_WIDTH + R_LORA]
    ad = xr_ref[:, 3 * R_WIDTH + R_LORA:3 * R_WIDTH + 2 * R_LORA]
    gin = xr_ref[:, 3 * R_WIDTH + 2 * R_LORA:]
    hones = hones_ref[...]

    def head_sum(x):
        return jnp.concatenate([_dot_sel(x[:, cs], hones, 2) for cs in cols], axis=1)

    tw = jnp.tanh(wd).astype(BF16)
    adb = ad.astype(BF16)
    ww = jnp.where(fwd, w0_ref[0:1, :] + _dot(tw, w2_ref[0]), w0_ref[1:2, :] + _dot(tw, w2_ref[1]))
    lw = -jnp.exp(-jax.nn.softplus(-ww) - 0.5)
    a_f = jax.nn.sigmoid(a0_ref[0:1, :] + _dot(adb, a2_ref[0]))
    a_b = jax.nn.sigmoid(a0_ref[1:2, :] + _dot(adb, a2_ref[1]))
    a = jnp.where(fwd, a_f, a_b)
    ka = ka_ref[...]
    kd = k * (1.0 + (a - 1.0) * ka)
    kk = k * kk_ref[...]
    kk = kk / jnp.maximum(jnp.sqrt(head_sum(kk * kk)), 1e-12)
    bvec = kk * a

    ri = lax.broadcasted_iota(jnp.int32, (L, L), 0)
    ci = lax.broadcasted_iota(jnp.int32, (L, L), 1)
    tri = ((ci - ri) * sgn <= 0).astype(BF16)
    cl_j = [_sel_dot(tri, lw[j * L:(j + 1) * L, :], 2) for j in range(n_chunks)]
    tot_j = [jnp.where(fwd, x[L - 1:L, :], x[0:1, :]) for x in cl_j]
    cl = jnp.concatenate(cl_j, axis=0)
    tot = jnp.concatenate([jnp.broadcast_to(x, (L, R_WIDTH)) for x in tot_j], axis=0)
    e_in = jnp.exp(cl)
    e_out = jnp.exp(-cl)
    e_end = jnp.exp(tot - cl)
    a_til = -kk * jnp.exp(cl - lw)
    r_til = r * e_in
    b_til = bvec * e_out
    k_til = kd * e_out
    b_hat = bvec * e_end
    k_hat = kd * e_end
    w_end = jnp.exp(tot)

    pr = lax.broadcasted_iota(jnp.int32, (P, P), 0)
    pc = lax.broadcasted_iota(jnp.int32, (P, P), 1)
    same = (pr // L) == (pc // L)
    before = (pc - pr) * sgn < 0
    m_strict = jnp.logical_and(same, before)
    m_incl = jnp.logical_and(same, jnp.logical_or(before, pc == pr))
    eye = pr == pc
    lane = lax.broadcasted_iota(jnp.int32, (L, LANES), 1)
    first = lane < R_HEAD

    def stack(x, j, p):
        x = x[j * L:(j + 1) * L, cols[p]]
        return jnp.concatenate([jnp.where(first, x, 0.0), jnp.where(first, 0.0, x)], axis=0).astype(BF16)

    chains = [(j, p) for j in range(n_chunks) for p in pairs]
    at = [stack(a_til, j, p) for j, p in chains]
    rt = [stack(r_til, j, p) for j, p in chains]
    bt = [stack(b_til, j, p) for j, p in chains]
    kt = [stack(k_til, j, p) for j, p in chains]
    bh = [stack(b_hat, j, p) for j, p in chains]
    kh = [stack(k_hat, j, p) for j, p in chains]
    vs = [stack(v, j, p) for j, p in chains]
    ids = range(len(chains))

    big = [_dot_nt(jnp.concatenate([at[i], rt[i]], axis=0), jnp.concatenate([bt[i], kt[i]], axis=0)) for i in ids]
    a_ab = [jnp.where(m_strict, big[i][:P, :P], 0.0) for i in ids]
    a_ak = [jnp.where(m_strict, big[i][:P, P:], 0.0).astype(BF16) for i in ids]
    a_rb = [jnp.where(m_incl, big[i][P:, :P], 0.0).astype(BF16) for i in ids]
    a_rk = [jnp.where(m_incl, big[i][P:, P:], 0.0).astype(BF16) for i in ids]

    def sib_mask(blk):
        return jnp.logical_and((pr // (2 * blk)) == (pc // (2 * blk)), (pr // blk) != (pc // blk))

    def to_wide(x, w):
        lane_blk = lax.broadcasted_iota(jnp.int32, (w, P), 1) // w
        return sum(jnp.where(lane_blk == k, x[k * w:(k + 1) * w, :], 0.0) for k in range(P // w))

    def to_diag(x, w):
        lane_blk = lax.broadcasted_iota(jnp.int32, (w, P), 1) // w
        return jnp.concatenate([jnp.where(lane_blk == k, x, 0.0) for k in range(P // w)], axis=0)

    pair2 = (pr // 2) == (pc // 2)
    tinv = [jnp.where(eye, 1.0, jnp.where(pair2, a_ab[i], 0.0)) for i in ids]
    blk = 2
    for wide in (L // 2, L):
        tw = [to_wide(tinv[i], wide) for i in ids]
        while 2 * blk <= wide:
            sib = sib_mask(blk)
            tb = [tw[i].astype(BF16) for i in ids]
            half = [_dot(tb[i], jnp.where(sib, a_ab[i], 0.0).astype(BF16)) for i in ids]
            tw = [tw[i] + _dot(half[i].astype(BF16), to_diag(tw[i], wide).astype(BF16)) for i in ids]
            blk *= 2
        tinv = [to_diag(tw[i], wide) for i in ids]

    av = [_dot(a_ak[i], vs[i]) for i in ids]
    pq = [_dot(tinv[i].astype(BF16), jnp.concatenate([at[i], av[i].astype(BF16)], axis=1)).astype(BF16) for i in ids]
    ry = [_dot(a_rb[i], pq[i]) for i in ids]
    rkv = [_dot(a_rk[i], vs[i]) for i in ids]
    mg = [_dot_tn(bh[i], pq[i]) for i in ids]
    kv = [_dot_tn(kh[i], vs[i]) for i in ids]
    r_hat = [(rt[i].astype(F32) + ry[i][:, :LANES]).astype(BF16) for i in ids]
    y0 = [ry[i][:, LANES:] + rkv[i] for i in ids]
    m_corr = [mg[i][:, :LANES].astype(BF16) for i in ids]
    g_add = [mg[i][:, LANES:] + kv[i] for i in ids]
    w_col = [jnp.sum(jnp.where(eye, jnp.broadcast_to(w_end[j * L:j * L + 1, cols[p]], (P, P)), 0.0),
                     axis=1, keepdims=True) for j, p in chains]

    def scan_chunks(order):
        y = [None] * n_chunks
        for j in order:
            ys = []
            for p in pairs:
                i = j * len(pairs) + p
                hprev = h_scr[p]
                hb = hprev.astype(BF16)
                yst = _dot(r_hat[i], hb) + y0[i]
                ys.append(yst[:L, :] + yst[L:, :])
                h_scr[p] = w_col[i] * hprev + _dot(m_corr[i], hb) + g_add[i]
            y[j] = jnp.concatenate(ys, axis=1)
        return jnp.concatenate(y, axis=0)

    cc = c + d * (nb - 1 - 2 * c)
    rows = pl.ds(pl.multiple_of(cc * LB, LB), LB)

    @pl.when(fwd)
    def _():
        out_ref[rows, :] = scan_chunks(range(n_chunks))

    @pl.when(jnp.logical_not(fwd))
    def _():
        inv_n = 1.0 / R_HEAD
        ysum = out_ref[rows, :] + scan_chunks(reversed(range(n_chunks)))
        mean = head_sum(ysum) * inv_n
        yc = ysum - mean
        var = head_sum(yc * yc) * inv_n
        yn = yc * lax.rsqrt(var + R_LN_EPS) * lng_ref[...] + lnb_ref[...]
        k_bar = k * (1.0 + (0.5 * (a_f + a_b) - 1.0) * ka)
        bonus = head_sum(r * k_bar * rk_ref[...]) * v
        gate = _dot(jax.nn.sigmoid(gin).astype(BF16), g2_ref[...])
        out_ref[rows, :] = (yn + bonus) * gate

    @pl.when(c == nb - 1)
    def _():
        hs_ref[0, 0] = h_scr[...]


def _rwkv_call(xr, w0, w2p, a0, a2p, g2, k_k, k_a, r_k, ln_g, ln_b, hones, batch, seq_len, h0):
    n = xr.shape[0]
    LB = R_CHUNK * R_CHUNKS_PER_STEP
    nb = seq_len // LB
    zero_init = h0 is None
    n_pairs = R_HEADS // 2

    def tok(b, d, c):
        return (b * nb + c + d * (nb - 1 - 2 * c), 0)

    consts = [w0, w2p, a0, a2p, g2, k_k, k_a, r_k, ln_g, ln_b, hones]
    args = [xr] + consts
    specs = [pl.BlockSpec((LB, R_COLS), tok)] + [_const_spec(a.shape) for a in consts]
    if not zero_init:
        args.append(h0)
        specs.append(pl.BlockSpec((1, 1, n_pairs, LANES, LANES), lambda b, d, c: (b, d, 0, 0, 0)))
    outs = [jax.ShapeDtypeStruct((n, R_WIDTH), F32),
            jax.ShapeDtypeStruct((batch, 2, n_pairs, LANES, LANES), F32)]
    out_specs = [pl.BlockSpec((seq_len, R_WIDTH), lambda b, d, c: (b, 0)),
                 pl.BlockSpec((1, 1, n_pairs, LANES, LANES), lambda b, d, c: (b, d, 0, 0, 0))]
    body = functools.partial(_rwkv_body, zero_init, nb)
    return pl.pallas_call(
        body,
        grid=(batch, 2, nb),
        in_specs=specs,
        out_specs=out_specs,
        out_shape=outs,
        scratch_shapes=[pltpu.VMEM((n_pairs, LANES, LANES), F32)],
        compiler_params=_cparams(("parallel", "arbitrary", "arbitrary")),
        name="rwkv_zero" if zero_init else "rwkv_init",
    )(*args)


def _pairs_from_heads(s):
    ht = jnp.swapaxes(s, -1, -2)
    lead = ht.shape[:-3]
    ht = ht.reshape(lead + (R_HEADS // 2, 2, R_HEAD, R_HEAD))
    z = jnp.zeros_like(ht[..., 0, :, :])
    top = jnp.concatenate([ht[..., 0, :, :], z], axis=-1)
    bot = jnp.concatenate([z, ht[..., 1, :, :]], axis=-1)
    return jnp.concatenate([top, bot], axis=-2)


def _heads_from_pairs(hp):
    a = hp[..., :R_HEAD, :R_HEAD]
    b = hp[..., R_HEAD:, R_HEAD:]
    s = jnp.stack([a, b], axis=-3)
    s = s.reshape(hp.shape[:-3] + (R_HEADS, R_HEAD, R_HEAD))
    return jnp.swapaxes(s, -1, -2)


def _trunk(x, mod3, mod_row0, per_seq_rows, grid_mode, init, w):
    batch, seq_len, d = x.shape
    x2 = x.reshape(batch * seq_len, d)

    def rows_for(tile):
        per = seq_len // tile
        if per_seq_rows:
            return lambda i: mod_row0 + i // per
        return lambda i: mod_row0

    x1 = _ffn_call(x2, mod3, rows_for(FFN_TILE), 0, w["norm_ffn1"], w["ffn1_in"], w["ffn1_out"])
    mix_tile = MIX_TILE if grid_mode else seq_len
    q, k, vt, ot, gc, gr, xr = _mixin_call(
        x1, seq_len, grid_mode, mod3, rows_for(mix_tile), w["norm_mix"], w["wqk"], w["wvot"], w["wg"], w["wgt"],
        w["wr"], w["conv_w"], w["conv_b"], w["mu"])
    if init is None:
        m_init = None
        r_init = None
    else:
        c0, n0, m0, s0 = init
        m0p = jnp.broadcast_to(jnp.pad(m0, ((0, 0), (0, 0), (0, 8 - M_HEADS)))[..., None], m0.shape[:2] + (8, LANES))
        m_init = (c0, n0, m0p)
        r_init = _pairs_from_heads(s0)
    mo, cs, ns, ms = _mlstm_call(q, k, vt, ot, gc, gr, w["gate_b_row"], w["gate_b_col"], w["head_g_col"],
                                 batch, seq_len, m_init)
    ro, hs = _rwkv_call(xr, w["r_w0"], w["r_w2p"], w["r_a0"], w["r_a2p"], w["r_g2"], w["r_k_k"], w["r_k_a"],
                        w["r_r_k"], w["r_ln_g"], w["r_ln_b"], w["hones"], batch, seq_len, r_init)
    y = _ffn_call(x1, mod3, rows_for(FFN_TILE), 6, w["norm_ffn2"], w["ffn2_in"], w["ffn2_out"],
                  mix=(mo, ro, w["wo_m"], w["wo_r"]), g_final=w["norm_final"])
    states = (cs, ns, ms[:, :, :M_HEADS, 0], _heads_from_pairs(hs))
    return y.reshape(batch, seq_len, d), states


def _prepare_weights(ada_w, ada_b, norm_ffn1, ffn1_w_in, ffn1_w_out, norm_mix, mix_w_in, mix_w_out,
                     m_conv_w, m_conv_b, m_gate_b, m_head_g, r_mu, r_w0, r_w2, r_a0, r_a2, r_g2,
                     r_k_k, r_k_a, r_r_k, r_ln_g, r_ln_b, norm_ffn2, ffn2_w_in, ffn2_w_out, norm_final):
    assert ada_w.shape[0] == 1, "single trunk layer"
    lora = r_w2.shape[2]
    nm = 4 * M_WIDTH

    w_in = mix_w_in[0]
    gate_w = w_in[:, nm:nm + GATE_COLS]
    zpad = jnp.zeros((lora, R_WIDTH), F32)

    def dir_pad(w2):
        return jnp.stack([jnp.concatenate([w2[0], zpad], axis=0), jnp.concatenate([zpad, w2[1]], axis=0)])

    head_id = jnp.arange(LANES) // R_HEAD
    return dict(
        norm_ffn1=norm_ffn1, norm_mix=norm_mix, norm_ffn2=norm_ffn2, norm_final=norm_final[None],
        ffn1_in=ffn1_w_in[0].astype(BF16), ffn1_out=ffn1_w_out[0].astype(BF16),
        ffn2_in=ffn2_w_in[0].astype(BF16), ffn2_out=ffn2_w_out[0].astype(BF16),
        wqk=w_in[:, :2 * M_WIDTH].astype(BF16), wvot=w_in[:, 2 * M_WIDTH:nm].T.astype(BF16),
        wg=jnp.pad(gate_w, ((0, 0), (0, LANES - GATE_COLS))).astype(BF16),
        wgt=gate_w.T.astype(BF16),
        wr=w_in[:, nm + GATE_COLS:].astype(BF16),
        wo_m=mix_w_out[0, :M_WIDTH].astype(BF16), wo_r=mix_w_out[0, M_WIDTH:].astype(BF16),
        conv_w=m_conv_w[0], conv_b=m_conv_b, mu=r_mu[0],
        gate_b_row=jnp.pad(m_gate_b[0].reshape(1, GATE_COLS), ((0, 0), (0, LANES - GATE_COLS))),
        gate_b_col=m_gate_b[0].reshape(GATE_COLS, 1),
        head_g_col=m_head_g.reshape(M_WIDTH, 1),
        r_w0=r_w0[0], r_w2p=dir_pad(r_w2[0]).astype(BF16), r_a0=r_a0[0], r_a2p=dir_pad(r_a2[0]).astype(BF16),
        r_g2=r_g2[0].astype(BF16), r_k_k=r_k_k, r_k_a=r_k_a, r_r_k=r_r_k[0].reshape(1, R_WIDTH),
        r_ln_g=r_ln_g, r_ln_b=r_ln_b,
        hones=(head_id[:, None] == head_id[None, :]).astype(BF16),
    )


def kernel(x_prompt, x_sample, c, state_mlstm_C, state_mlstm_n, state_mlstm_m, state_rwkv_S, c_ctx,
           ada_w, ada_b, norm_ffn1, ffn1_w_in, ffn1_w_out, norm_mix, mix_w_in, mix_w_out,
           m_conv_w, m_conv_b, m_gate_b, m_head_g, r_mu, r_w0, r_w2, r_a0, r_a2, r_g2,
           r_k_k, r_k_a, r_r_k, r_ln_g, r_ln_b, norm_ffn2, ffn2_w_in, ffn2_w_out, norm_final):
    w = _prepare_weights(ada_w, ada_b, norm_ffn1, ffn1_w_in, ffn1_w_out, norm_mix, mix_w_in, mix_w_out,
                         m_conv_w, m_conv_b, m_gate_b, m_head_g, r_mu, r_w0, r_w2, r_a0, r_a2, r_g2,
                         r_k_k, r_k_a, r_r_k, r_ln_g, r_ln_b, norm_ffn2, ffn2_w_in, ffn2_w_out, norm_final)
    d = x_prompt.shape[-1]
    dec_batch = x_sample.shape[0]
    cond = jnp.concatenate([c_ctx[None], c, jnp.zeros((16 - 1 - dec_batch, d), F32)], axis=0)
    mod3 = _ada_call(cond, ada_w[0], ada_b).reshape(16, 9, d)

    y_prompt, (cs, ns, ms, ss) = _trunk(x_prompt, mod3, 0, False, False, None, w)
    init = (state_mlstm_C[:, 0], state_mlstm_n[:, 0], state_mlstm_m[:, 0], state_rwkv_S[:, 0])
    y_sample, _ = _trunk(x_sample, mod3, 1, True, True, init, w)
    return (y_prompt, y_sample, cs[:, None], ns[:, None], ms[:, None], ss[:, None])
```

```python
import functools

import jax
import jax.numpy as jnp
from jax import lax
from jax.experimental import pallas as pl
from jax.experimental.pallas import tpu as pltpu

F32 = jnp.float32
BF16 = jnp.bfloat16

EPS = 1e-6
R_LN_EPS = 64e-5
GRID_W = 64
M_HEADS = 4
M_DK = 128
M_DV = 128
M_WIDTH = M_HEADS * M_DV
R_HEADS = 8
R_HEAD = 64
R_WIDTH = R_HEADS * R_HEAD
R_LORA = 128
R_COLS = 3 * R_WIDTH + 3 * R_LORA
GATE_COLS = 16
LANES = 128

FFN_TILE = 1024
FFN_CHUNK = 256
MIX_TILE = 512
M_CHUNK = 256
R_CHUNK = 64
R_CHUNKS_PER_STEP = 8
VMEM_LIMIT = 56 * 1024 * 1024


def _dot(a, b):
    return jnp.dot(a, b, preferred_element_type=F32)


def _dot_nt(a, b):
    return lax.dot_general(a, b, (((1,), (1,)), ((), ())), preferred_element_type=F32)


def _dot_tn(a, b):
    return lax.dot_general(a, b, (((0,), (0,)), ((), ())), preferred_element_type=F32)


def _split(x, parts):
    pieces = []
    for _ in range(parts - 1):
        p = x.astype(BF16)
        pieces.append(p)
        x = x - p.astype(F32)
    pieces.append(x.astype(BF16))
    return pieces


def _sel_dot(mat, x, parts=3):
    return sum(_dot(mat, p) for p in _split(x, parts))


def _dot_sel(x, mat, parts=3):
    return sum(_dot(p, mat) for p in _split(x, parts))


def _silu(x):
    return x * jax.nn.sigmoid(x)


def _rms_mod(x, g, shift, scale):
    y = x * lax.rsqrt(jnp.mean(x * x, axis=-1, keepdims=True) + EPS) * g
    return y * (1.0 + scale) + shift


def _cparams(sem):
    return pltpu.CompilerParams(dimension_semantics=sem, vmem_limit_bytes=VMEM_LIMIT)


def _const_spec(shape):
    nd = len(shape)
    return pl.BlockSpec(shape, lambda *_: (0,) * nd, pipeline_mode=pl.Buffered(1))


def _ada_body(c_ref, w_ref, b_ref, o_ref):
    s = _silu(c_ref[...])
    o_ref[...] = _dot(s.astype(BF16), w_ref[...].astype(BF16)) + b_ref[...]


def _ada_call(cond, w, b):
    rows, d = cond.shape
    n = w.shape[1]
    tn = d
    return pl.pallas_call(
        _ada_body,
        grid=(n // tn,),
        in_specs=[
            pl.BlockSpec((rows, d), lambda j: (0, 0)),
            pl.BlockSpec((d, tn), lambda j: (0, j)),
            pl.BlockSpec((1, tn), lambda j: (0, j)),
        ],
        out_specs=pl.BlockSpec((rows, tn), lambda j: (0, j)),
        out_shape=jax.ShapeDtypeStruct((rows, n), F32),
        compiler_params=_cparams(("arbitrary",)),
        name="ada",
    )(cond, w, b)


def _ffn_body(mod_base, with_mix, with_final, *refs):
    refs = list(refs)
    x_ref = refs.pop(0)
    if with_mix:
        mo_ref, ro_ref, wmm_ref, wmr_ref = refs[:4]
        refs = refs[4:]
    mod_ref, g_ref, win_ref, wout_ref = refs[:4]
    refs = refs[4:]
    if with_final:
        gfin_ref = refs.pop(0)
    out_ref, a_scr = refs

    x = x_ref[...]
    if with_mix:
        mix = _dot(mo_ref[...].astype(BF16), wmm_ref[...]) + _dot(ro_ref[...].astype(BF16), wmr_ref[...])
        x = x + mod_ref[0, mod_base - 1:mod_base, :] * mix
    shift = mod_ref[0, mod_base:mod_base + 1, :]
    scale = mod_ref[0, mod_base + 1:mod_base + 2, :]
    gate = mod_ref[0, mod_base + 2:mod_base + 3, :]
    hb = _rms_mod(x, g_ref[...], shift, scale).astype(BF16)
    dff = wout_ref.shape[0]
    for j in range(dff // FFN_CHUNK):
        lo = j * FFN_CHUNK
        gt = _dot(hb, win_ref[:, lo:lo + FFN_CHUNK])
        up = _dot(hb, win_ref[:, dff + lo:dff + lo + FFN_CHUNK])
        a_scr[:, lo:lo + FFN_CHUNK] = (_silu(gt) * up).astype(BF16)
    y = x + (0.5 * gate) * _dot(a_scr[...], wout_ref[...])
    if with_final:
        y = y * lax.rsqrt(jnp.mean(y * y, axis=-1, keepdims=True) + EPS) * gfin_ref[...]
    out_ref[...] = y


def _ffn_call(x, mod3, row_of_tile, mod_base, g, w_in, w_out, mix=None, g_final=None):
    n, d = x.shape
    tm = FFN_TILE
    dff = w_out.shape[0]
    tok = lambda i: (i, 0)
    args = [x]
    specs = [pl.BlockSpec((tm, d), tok)]
    if mix is not None:
        mo, ro, wmm, wmr = mix
        args += [mo, ro, wmm, wmr]
        specs += [pl.BlockSpec((tm, mo.shape[1]), tok), pl.BlockSpec((tm, ro.shape[1]), tok),
                  _const_spec(wmm.shape), _const_spec(wmr.shape)]
    args += [mod3, g, w_in, w_out]
    specs += [pl.BlockSpec((1,) + mod3.shape[1:], lambda i: (row_of_tile(i), 0, 0)),
              _const_spec(g.shape), _const_spec(w_in.shape), _const_spec(w_out.shape)]
    if g_final is not None:
        args.append(g_final)
        specs.append(_const_spec(g_final.shape))
    body = functools.partial(_ffn_body, mod_base, mix is not None, g_final is not None)
    return pl.pallas_call(
        body,
        grid=(n // tm,),
        in_specs=specs,
        out_specs=pl.BlockSpec((tm, d), tok),
        out_shape=jax.ShapeDtypeStruct((n, d), F32),
        scratch_shapes=[pltpu.VMEM((tm, dff), BF16)],
        compiler_params=_cparams(("parallel",)),
        name="ffn_mix" if mix is not None else "ffn",
    )(*args)


def _mixin_body(grid_mode, has_halo, tiles_per_seq, *refs):
    refs = list(refs)
    x_ref = refs.pop(0)
    if has_halo:
        xp_ref, xn_ref = refs[:2]
        refs = refs[2:]
    (mod_ref, g_ref, wqk_ref, wvot_ref, wg_ref, wgt_ref, wr_ref, cw_ref, cb_ref, mu_ref,
     q_ref, k_ref, vt_ref, ot_ref, gc_ref, gr_ref, xr_ref) = refs

    tm = x_ref.shape[0]
    shift = mod_ref[0, 3:4, :]
    scale = mod_ref[0, 4:5, :]
    g = g_ref[...]
    hb = _rms_mod(x_ref[...], g, shift, scale).astype(BF16)
    row = lax.broadcasted_iota(jnp.int32, (tm, 1), 0)
    nqk = 2 * M_WIDTH

    if has_halo:
        i = pl.program_id(0)
        pos = i % tiles_per_seq
        keep_p = (pos > 0).astype(F32)
        keep_n = (pos < tiles_per_seq - 1).astype(F32)
        hp32 = _rms_mod(xp_ref[...], g, shift, scale)
        hn32 = _rms_mod(xn_ref[...], g, shift, scale)
        hp = hp32.astype(BF16)
        hn = hn32.astype(BF16)
        nh = hp32.shape[0]
        edge = 16
        qk_prev = _dot(hp32[nh - edge:, :].astype(BF16), wqk_ref[...])[edge - 1:edge, :] * keep_p
        qk_next = _dot(hn32[:edge, :].astype(BF16), wqk_ref[...])[0:1, :] * keep_n
    else:
        qk_prev = jnp.zeros((1, nqk), F32)
        qk_next = jnp.zeros((1, nqk), F32)

    uqk = _dot(hb, wqk_ref[...])
    ur = _dot(hb, wr_ref[...])
    u_dn = jnp.where(row == 0, qk_prev, pltpu.roll(uqk, 1, 0))
    u_up = jnp.where(row == tm - 1, qk_next, pltpu.roll(uqk, tm - 1, 0))
    qk = _silu(cw_ref[0:1, :] * u_dn + cw_ref[1:2, :] * uqk + cw_ref[2:3, :] * u_up + cb_ref[...])
    q_ref[...] = qk[:, :M_WIDTH].astype(BF16)
    k_ref[...] = (qk[:, M_WIDTH:] * (M_DK ** -0.5)).astype(BF16)
    vo_t = _dot_nt(wvot_ref[...], hb)
    gc_ref[...] = _dot(hb, wg_ref[...])
    gr_ref[...] = _dot_nt(wgt_ref[...], hb)

    if grid_mode:
        col = row % GRID_W
        left = jnp.where(col == 0, 0.0, pltpu.roll(ur, 1, 0))
        right = jnp.where(col == GRID_W - 1, 0.0, pltpu.roll(ur, tm - 1, 0))
        ur_p = _dot(hp, wr_ref[...]) * keep_p
        ur_n = _dot(hn, wr_ref[...]) * keep_n
        up = jnp.concatenate([ur_p, ur[:tm - GRID_W, :]], axis=0)
        down = jnp.concatenate([ur[GRID_W:, :], ur_n], axis=0)
        mu_self = 1.0 - (mu_ref[0:1, :] + mu_ref[1:2, :] + mu_ref[2:3, :] + mu_ref[3:4, :])
        xr = (mu_self * ur + mu_ref[0:1, :] * left + mu_ref[1:2, :] * right
              + mu_ref[2:3, :] * up + mu_ref[3:4, :] * down)
    else:
        left = jnp.where(row == 0, 0.0, pltpu.roll(ur, 1, 0))
        right = jnp.where(row == tm - 1, 0.0, pltpu.roll(ur, tm - 1, 0))
        mu_self = 1.0 - (mu_ref[0:1, :] + mu_ref[1:2, :])
        xr = mu_self * ur + mu_ref[0:1, :] * left + mu_ref[1:2, :] * right
    xr_ref[...] = xr
    vt_ref[...] = vo_t[:M_WIDTH, :].astype(BF16)
    ot_ref[...] = vo_t[M_WIDTH:, :]


def _mixin_call(x, seq_len, grid_mode, mod3, row_of_tile, g, wqk, wvot, wg, wgt, wr, cw, cb, mu):
    n, d = x.shape
    if grid_mode:
        tm = MIX_TILE
        has_halo = True
    else:
        tm = seq_len
        has_halo = False
    tiles_per_seq = seq_len // tm
    tok = lambda i: (i, 0)
    args = [x]
    specs = [pl.BlockSpec((tm, d), tok)]
    if has_halo:
        hb = R_CHUNK
        per = tm // hb
        last = n // hb - 1
        args += [x, x]
        specs += [pl.BlockSpec((hb, d), lambda i: (jnp.maximum(i * per - 1, 0), 0)),
                  pl.BlockSpec((hb, d), lambda i: (jnp.minimum((i + 1) * per, last), 0))]
    args += [mod3, g, wqk, wvot, wg, wgt, wr, cw, cb, mu]
    specs += [pl.BlockSpec((1,) + mod3.shape[1:], lambda i: (row_of_tile(i), 0, 0))]
    specs += [_const_spec(a.shape) for a in (g, wqk, wvot, wg, wgt, wr, cw, cb, mu)]
    outs = [jax.ShapeDtypeStruct((n, M_WIDTH), BF16)] * 2 + [
        jax.ShapeDtypeStruct((M_WIDTH, n), BF16),
        jax.ShapeDtypeStruct((M_WIDTH, n), F32),
        jax.ShapeDtypeStruct((n, LANES), F32),
        jax.ShapeDtypeStruct((GATE_COLS, n), F32),
        jax.ShapeDtypeStruct((n, R_COLS), F32),
    ]
    out_specs = [pl.BlockSpec((tm, M_WIDTH), tok)] * 2 + [pl.BlockSpec((M_WIDTH, tm), lambda i: (0, i))] * 2 + [
        pl.BlockSpec((tm, LANES), tok),
        pl.BlockSpec((GATE_COLS, tm), lambda i: (0, i)),
        pl.BlockSpec((tm, R_COLS), tok),
    ]
    body = functools.partial(_mixin_body, grid_mode, has_halo, tiles_per_seq)
    return pl.pallas_call(
        body,
        grid=(n // tm,),
        in_specs=specs,
        out_specs=out_specs,
        out_shape=outs,
        compiler_params=_cparams(("parallel",)),
        name="mix_in_grid" if grid_mode else "mix_in_seq",
    )(*args)


def _mlstm_body(zero_init, nc, *refs):
    refs = list(refs)
    q_ref, k_ref, vt_ref, ot_ref, gc_ref, gr_ref, gbr_ref, gbc_ref, hg_ref = refs[:9]
    refs = refs[9:]
    if not zero_init:
        c0_ref, n0_ref, m0_ref = refs[:3]
        refs = refs[3:]
    out_ref, cs_ref, ns_ref, ms_ref, ct_scr, n_scr, m_scr, acc_scr = refs

    d = pl.program_id(1)
    c = pl.program_id(2)
    L = q_ref.shape[0]
    fwd = d == 0
    heads = range(M_HEADS)
    hsl = [slice(h * M_DV, (h + 1) * M_DV) for h in heads]

    @pl.when(c == 0)
    def _():
        if zero_init:
            ct_scr[...] = jnp.zeros_like(ct_scr)
            n_scr[...] = jnp.zeros_like(n_scr)
            m_scr[...] = jnp.zeros_like(m_scr)
        else:
            for h in heads:
                ct_scr[h] = c0_ref[0, 0, h].T
            n_scr[0:M_HEADS, :] = n0_ref[0, 0]
            m_scr[...] = m0_ref[0, 0]

    ri = lax.broadcasted_iota(jnp.int32, (L, L), 0)
    ci = lax.broadcasted_iota(jnp.int32, (L, L), 1)
    sgn = 1 - 2 * d
    tri = ((ci - ri) * sgn <= 0).astype(BF16)
    mask_st = (ri - ci) * sgn <= 0
    tri_t = mask_st.astype(BF16)

    gcol = gc_ref[...] + gbr_ref[...]
    grow = gr_ref[...] + gbc_ref[...]
    bcol = _sel_dot(tri, jax.nn.log_sigmoid(gcol))
    brow = _dot_sel(jax.nn.log_sigmoid(grow), tri_t)

    def pick_col(a, j0, j1):
        return jnp.where(fwd, a[:, j0:j0 + 1], a[:, j1:j1 + 1])

    def pick_row(a, j0, j1):
        return jnp.where(fwd, a[j0:j0 + 1, :], a[j1:j1 + 1, :])

    cc = c + d * (nc - 1 - 2 * c)
    row0 = lax.broadcasted_iota(jnp.int32, (16, LANES), 0) == 0
    row0_l = lax.broadcasted_iota(jnp.int32, (16, L), 0) == 0

    qb = [q_ref[:, hs] for hs in hsl]
    kb = [k_ref[:, hs] for hs in hsl]
    vt = [vt_ref[hs, :] for hs in hsl]
    ra_c = [pick_col(gcol, h, 4 + h) - pick_col(bcol, 8 + h, 12 + h) for h in heads]
    i_r = [pick_row(grow, h, 4 + h) for h in heads]
    b_r = [pick_row(brow, 8 + h, 12 + h) for h in heads]
    b_tot = [jnp.where(fwd, b_r[h][:, L - 1:L], b_r[h][:, 0:1]) for h in heads]
    m_prev = [m_scr[h:h + 1, 0:1] for h in heads]
    n_prev = [n_scr[h:h + 1, :] for h in heads]
    ct_prev = [ct_scr[h] for h in heads]

    kq = [_dot_nt(kb[h], qb[h]) for h in heads]
    qct = [_dot_nt(ct_prev[h].astype(BF16), qb[h]) for h in heads]
    qn = [_dot_nt(jnp.where(row0, n_prev[h], 0.0).astype(BF16), qb[h])[0:1, :] for h in heads]
    dmt = [jnp.where(mask_st, ra_c[h], -jnp.inf) for h in heads]
    mx = [jnp.maximum(m_prev[h], jnp.max(dmt[h], axis=0, keepdims=True)) for h in heads]
    st = [kq[h] * jnp.exp(dmt[h] - mx[h]) for h in heads]
    sc = [jnp.exp(m_prev[h] - mx[h]) for h in heads]
    numt = [_dot(vt[h], st[h].astype(BF16)) + sc[h] * qct[h] for h in heads]
    den = [jnp.sum(st[h], axis=0, keepdims=True) + sc[h] * qn[h] for h in heads]
    ht = [numt[h] / jnp.maximum(jnp.abs(den[h]), jnp.exp(-(b_r[h] + mx[h]))) for h in heads]

    gs = [b_tot[h] - b_r[h] + i_r[h] for h in heads]
    m_new = [jnp.maximum(b_tot[h] + m_prev[h], jnp.max(gs[h], axis=1, keepdims=True)) for h in heads]
    wk = [jnp.exp(gs[h] - m_new[h]) for h in heads]
    decay = [jnp.exp(b_tot[h] + m_prev[h] - m_new[h]) for h in heads]
    ckv = [_dot((vt[h].astype(F32) * wk[h]).astype(BF16), kb[h]) for h in heads]
    nk = [_dot(jnp.where(row0_l, wk[h], 0.0).astype(BF16), kb[h])[0:1, :] for h in heads]
    for h in heads:
        ct_scr[h] = decay[h] * ct_prev[h] + ckv[h]
        n_scr[h:h + 1, :] = decay[h] * n_prev[h] + nk[h]
        m_scr[h:h + 1, :] = jnp.broadcast_to(m_new[h], (1, LANES))

    @pl.when(fwd)
    def _():
        acc_scr[cc] = jnp.concatenate(ht, axis=0)

    @pl.when(jnp.logical_not(fwd))
    def _():
        outs = []
        for h in heads:
            t = jax.nn.sigmoid(ot_ref[hsl[h], :]) * (acc_scr[cc, hsl[h], :] + ht[h])
            t = t * lax.rsqrt(jnp.mean(t * t, axis=0, keepdims=True) + EPS) * hg_ref[hsl[h], :]
            outs.append(t.T)
        out_ref[...] = jnp.concatenate(outs, axis=1).astype(BF16)

    @pl.when(c == nc - 1)
    def _():
        for h in heads:
            cs_ref[0, 0, h] = ct_scr[h].T
        ns_ref[0, 0] = n_scr[0:M_HEADS, :]
        ms_ref[0, 0] = m_scr[...]


def _mlstm_call(q, k, vt, ot, gc, gr, gate_b_row, gate_b_col, head_g_col, batch, seq_len, init):
    n = q.shape[0]
    L = min(M_CHUNK, seq_len)
    nc = seq_len // L
    zero_init = init is None

    def blk(b, d, c):
        return b * nc + c + d * (nc - 1 - 2 * c)

    tok = lambda b, d, c: (blk(b, d, c), 0)
    tok_t = lambda b, d, c: (0, blk(b, d, c))
    tok_out = lambda b, d, c: (b * nc + nc - 1 - c * d, 0)

    args = [q, k, vt, ot, gc, gr, gate_b_row, gate_b_col, head_g_col]
    specs = [pl.BlockSpec((L, M_WIDTH), tok)] * 2 + [pl.BlockSpec((M_WIDTH, L), tok_t)] * 2 + [
        pl.BlockSpec((L, LANES), tok),
        pl.BlockSpec((GATE_COLS, L), tok_t),
        _const_spec(gate_b_row.shape), _const_spec(gate_b_col.shape), _const_spec(head_g_col.shape),
    ]
    if not zero_init:
        c0, n0, m0 = init
        args += [c0, n0, m0]
        specs += [pl.BlockSpec((1, 1) + c0.shape[2:], lambda b, d, c: (b, d, 0, 0, 0)),
                  pl.BlockSpec((1, 1) + n0.shape[2:], lambda b, d, c: (b, d, 0, 0)),
                  pl.BlockSpec((1, 1) + m0.shape[2:], lambda b, d, c: (b, d, 0, 0))]
    outs = [jax.ShapeDtypeStruct((n, M_WIDTH), BF16),
            jax.ShapeDtypeStruct((batch, 2, M_HEADS, M_DK, M_DV), F32),
            jax.ShapeDtypeStruct((batch, 2, M_HEADS, M_DK), F32),
            jax.ShapeDtypeStruct((batch, 2, 8, LANES), F32)]
    out_specs = [pl.BlockSpec((L, M_WIDTH), tok_out),
                 pl.BlockSpec((1, 1, M_HEADS, M_DK, M_DV), lambda b, d, c: (b, d, 0, 0, 0)),
                 pl.BlockSpec((1, 1, M_HEADS, M_DK), lambda b, d, c: (b, d, 0, 0)),
                 pl.BlockSpec((1, 1, 8, LANES), lambda b, d, c: (b, d, 0, 0))]
    body = functools.partial(_mlstm_body, zero_init, nc)
    return pl.pallas_call(
        body,
        grid=(batch, 2, nc),
        in_specs=specs,
        out_specs=out_specs,
        out_shape=outs,
        scratch_shapes=[pltpu.VMEM((M_HEADS, M_DV, M_DK), F32),
                        pltpu.VMEM((8, LANES), F32),
                        pltpu.VMEM((8, LANES), F32),
                        pltpu.VMEM((nc, M_WIDTH, L), F32)],
        compiler_params=_cparams(("parallel", "arbitrary", "arbitrary")),
        name="mlstm_zero" if zero_init else "mlstm_init",
    )(*args)


def _rwkv_body(zero_init, nb, *refs):
    refs = list(refs)
    (xr_ref, w0_ref, w2_ref, a0_ref, a2_ref, g2_ref, kk_ref, ka_ref, rk_ref,
     lng_ref, lnb_ref, hones_ref) = refs[:12]
    refs = refs[12:]
    if not zero_init:
        h0_ref = refs.pop(0)
    out_ref, hs_ref, h_scr = refs

    d = pl.program_id(1)
    c = pl.program_id(2)
    LB = xr_ref.shape[0]
    L = R_CHUNK
    n_chunks = LB // L
    P = 2 * L
    fwd = d == 0
    sgn = 1 - 2 * d
    pairs = range(R_HEADS // 2)
    cols = [slice(p * LANES, (p + 1) * LANES) for p in pairs]

    @pl.when(c == 0)
    def _():
        if zero_init:
            h_scr[...] = jnp.zeros_like(h_scr)
        else:
            h_scr[...] = h0_ref[0, 0]

    r = xr_ref[:, 0:R_WIDTH]
    k = xr_ref[:, R_WIDTH:2 * R_WIDTH]
    v = xr_ref[:, 2 * R_WIDTH:3 * R_WIDTH]
    wd = xr_ref[:, 3 * R_WIDTH:3 * R_WIDTH + R_LORA]
    ad = xr_ref[:, 3 * R_WIDTH + R_LORA:3 * R_WIDTH + 2 * R_LORA]
    gin = xr_ref[:, 3 * R_WIDTH + 2 * R_LORA:]
    hones = hones_ref[...]

    def head_sum(x):
        return jnp.concatenate([_dot_sel(x[:, cs], hones, 2) for cs in cols], axis=1)

    tw = jnp.tanh(wd).astype(BF16)
    adb = ad.astype(BF16)
    ww = jnp.where(fwd, w0_ref[0:1, :] + _dot(tw, w2_ref[0]), w0_ref[1:2, :] + _dot(tw, w2_ref[1]))
    lw = -jnp.exp(-jax.nn.softplus(-ww) - 0.5)
    a_f = jax.nn.sigmoid(a0_ref[0:1, :] + _dot(adb, a2_ref[0]))
    a_b = jax.nn.sigmoid(a0_ref[1:2, :] + _dot(adb, a2_ref[1]))
    a = jnp.where(fwd, a_f, a_b)
    ka = ka_ref[...]
    kd = k * (1.0 + (a - 1.0) * ka)
    kk = k * kk_ref[...]
    kk = kk / jnp.maximum(jnp.sqrt(head_sum(kk * kk)), 1e-12)
    bvec = kk * a

    ri = lax.broadcasted_iota(jnp.int32, (L, L), 0)
    ci = lax.broadcasted_iota(jnp.int32, (L, L), 1)
    tri = ((ci - ri) * sgn <= 0).astype(BF16)
    cl_j = [_sel_dot(tri, lw[j * L:(j + 1) * L, :], 2) for j in range(n_chunks)]
    tot_j = [jnp.where(fwd, x[L - 1:L, :], x[0:1, :]) for x in cl_j]
    cl = jnp.concatenate(cl_j, axis=0)
    tot = jnp.concatenate([jnp.broadcast_to(x, (L, R_WIDTH)) for x in tot_j], axis=0)
    e_in = jnp.exp(cl)
    e_out = jnp.exp(-cl)
    e_end = jnp.exp(tot - cl)
    a_til = -kk * jnp.exp(cl - lw)
    r_til = r * e_in
    b_til = bvec * e_out
    k_til = kd * e_out
    b_hat = bvec * e_end
    k_hat = kd * e_end
    w_end = jnp.exp(tot)

    pr = lax.broadcasted_iota(jnp.int32, (P, P), 0)
    pc = lax.broadcasted_iota(jnp.int32, (P, P), 1)
    same = (pr // L) == (pc // L)
    before = (pc - pr) * sgn < 0
    m_strict = jnp.logical_and(same, before)
    m_incl = jnp.logical_and(same, jnp.logical_or(before, pc == pr))
    eye = pr == pc
    lane = lax.broadcasted_iota(jnp.int32, (L, LANES), 1)
    first = lane < R_HEAD

    def stack(x, j, p):
        x = x[j * L:(j + 1) * L, cols[p]]
        return jnp.concatenate([jnp.where(first, x, 0.0), jnp.where(first, 0.0, x)], axis=0).astype(BF16)

    chains = [(j, p) for j in range(n_chunks) for p in pairs]
    at = [stack(a_til, j, p) for j, p in chains]
    rt = [stack(r_til, j, p) for j, p in chains]
    bt = [stack(b_til, j, p) for j, p in chains]
    kt = [stack(k_til, j, p) for j, p in chains]
    bh = [stack(b_hat, j, p) for j, p in chains]
    kh = [stack(k_hat, j, p) for j, p in chains]
    vs = [stack(v, j, p) for j, p in chains]
    ids = range(len(chains))

    big = [_dot_nt(jnp.concatenate([at[i], rt[i]], axis=0), jnp.concatenate([bt[i], kt[i]], axis=0)) for i in ids]
    a_ab = [jnp.where(m_strict, big[i][:P, :P], 0.0) for i in ids]
    a_ak = [jnp.where(m_strict, big[i][:P, P:], 0.0).astype(BF16) for i in ids]
    a_rb = [jnp.where(m_incl, big[i][P:, :P], 0.0).astype(BF16) for i in ids]
    a_rk = [jnp.where(m_incl, big[i][P:, P:], 0.0).astype(BF16) for i in ids]

    def sib_mask(blk):
        return jnp.logical_and((pr // (2 * blk)) == (pc // (2 * blk)), (pr // blk) != (pc // blk))

    def to_wide(x, w):
        lane_blk = lax.broadcasted_iota(jnp.int32, (w, P), 1) // w
        return sum(jnp.where(lane_blk == k, x[k * w:(k + 1) * w, :], 0.0) for k in range(P // w))

    def to_diag(x, w):
        lane_blk = lax.broadcasted_iota(jnp.int32, (w, P), 1) // w
        return jnp.concatenate([jnp.where(lane_blk == k, x, 0.0) for k in range(P // w)], axis=0)

    pair2 = (pr // 2) == (pc // 2)
    tinv = [jnp.where(eye, 1.0, jnp.where(pair2, a_ab[i], 0.0)) for i in ids]
    blk = 2
    for wide in (L // 2, L):
        tw = [to_wide(tinv[i], wide) for i in ids]
        while 2 * blk <= wide:
            sib = sib_mask(blk)
            tb = [tw[i].astype(BF16) for i in ids]
            half = [_dot(tb[i], jnp.where(sib, a_ab[i], 0.0).astype(BF16)) for i in ids]
            tw = [tw[i] + _dot(half[i].astype(BF16), to_diag(tw[i], wide).astype(BF16)) for i in ids]
            blk *= 2
        tinv = [to_diag(tw[i], wide) for i in ids]

    av = [_dot(a_ak[i], vs[i]) for i in ids]
    pq = [_dot(tinv[i].astype(BF16), jnp.concatenate([at[i], av[i].astype(BF16)], axis=1)).astype(BF16) for i in ids]
    ry = [_dot(a_rb[i], pq[i]) for i in ids]
    rkv = [_dot(a_rk[i], vs[i]) for i in ids]
    mg = [_dot_tn(bh[i], pq[i]) for i in ids]
    kv = [_dot_tn(kh[i], vs[i]) for i in ids]
    r_hat = [(rt[i].astype(F32) + ry[i][:, :LANES]).astype(BF16) for i in ids]
    y0 = [ry[i][:, LANES:] + rkv[i] for i in ids]
    m_corr = [mg[i][:, :LANES].astype(BF16) for i in ids]
    g_add = [mg[i][:, LANES:] + kv[i] for i in ids]
    w_col = [jnp.sum(jnp.where(eye, jnp.broadcast_to(w_end[j * L:j * L + 1, cols[p]], (P, P)), 0.0),
                     axis=1, keepdims=True) for j, p in chains]

    def scan_chunks(order):
        y = [None] * n_chunks
        for j in order:
            ys = []
            for p in pairs:
                i = j * len(pairs) + p
                hprev = h_scr[p]
                hb = hprev.astype(BF16)
                yst = _dot(r_hat[i], hb) + y0[i]
                ys.append(yst[:L, :] + yst[L:, :])
                h_scr[p] = w_col[i] * hprev + _dot(m_corr[i], hb) + g_add[i]
            y[j] = jnp.concatenate(ys, axis=1)
        return jnp.concatenate(y, axis=0)

    cc = c + d * (nb - 1 - 2 * c)
    rows = pl.ds(pl.multiple_of(cc * LB, LB), LB)

    @pl.when(fwd)
    def _():
        out_ref[rows, :] = scan_chunks(range(n_chunks))

    @pl.when(jnp.logical_not(fwd))
    def _():
        inv_n = 1.0 / R_HEAD
        ysum = out_ref[rows, :] + scan_chunks(reversed(range(n_chunks)))
        mean = head_sum(ysum) * inv_n
        yc = ysum - mean
        var = head_sum(yc * yc) * inv_n
        yn = yc * lax.rsqrt(var + R_LN_EPS) * lng_ref[...] + lnb_ref[...]
        k_bar = k * (1.0 + (0.5 * (a_f + a_b) - 1.0) * ka)
        bonus = head_sum(r * k_bar * rk_ref[...]) * v
        gate = _dot(jax.nn.sigmoid(gin).astype(BF16), g2_ref[...])
        out_ref[rows, :] = (yn + bonus) * gate

    @pl.when(c == nb - 1)
    def _():
        for p in pairs:
            st = h_scr[p].T
            hs_ref[0, 0, 2 * p] = st[:R_HEAD, :R_HEAD]
            hs_ref[0, 0, 2 * p + 1] = st[R_HEAD:, R_HEAD:]


def _rwkv_call(xr, w0, w2p, a0, a2p, g2, k_k, k_a, r_k, ln_g, ln_b, hones, batch, seq_len, h0):
    n = xr.shape[0]
    LB = min(R_CHUNK * R_CHUNKS_PER_STEP, seq_len)
    nb = seq_len // LB
    zero_init = h0 is None
    n_pairs = R_HEADS // 2

    def tok(b, d, c):
        return (b * nb + c + d * (nb - 1 - 2 * c), 0)

    consts = [w0, w2p, a0, a2p, g2, k_k, k_a, r_k, ln_g, ln_b, hones]
    args = [xr] + consts
    specs = [pl.BlockSpec((LB, R_COLS), tok)] + [_const_spec(a.shape) for a in consts]
    if not zero_init:
        args.append(h0)
        specs.append(pl.BlockSpec((1, 1, n_pairs, LANES, LANES), lambda b, d, c: (b, d, 0, 0, 0)))
    outs = [jax.ShapeDtypeStruct((n, R_WIDTH), F32),
            jax.ShapeDtypeStruct((batch, 2, R_HEADS, R_HEAD, R_HEAD), F32)]
    out_specs = [pl.BlockSpec((seq_len, R_WIDTH), lambda b, d, c: (b, 0)),
                 pl.BlockSpec((1, 1, R_HEADS, R_HEAD, R_HEAD), lambda b, d, c: (b, d, 0, 0, 0))]
    body = functools.partial(_rwkv_body, zero_init, nb)
    return pl.pallas_call(
        body,
        grid=(batch, 2, nb),
        in_specs=specs,
        out_specs=out_specs,
        out_shape=outs,
        scratch_shapes=[pltpu.VMEM((n_pairs, LANES, LANES), F32)],
        compiler_params=_cparams(("parallel", "arbitrary", "arbitrary")),
        name="rwkv_zero" if zero_init else "rwkv_init",
    )(*args)


def _pairs_from_heads(s):
    ht = jnp.swapaxes(s, -1, -2)
    lead = ht.shape[:-3]
    ht = ht.reshape(lead + (R_HEADS // 2, 2, R_HEAD, R_HEAD))
    z = jnp.zeros_like(ht[..., 0, :, :])
    top = jnp.concatenate([ht[..., 0, :, :], z], axis=-1)
    bot = jnp.concatenate([z, ht[..., 1, :, :]], axis=-1)
    return jnp.concatenate([top, bot], axis=-2)


def _trunk(x, mod3, mod_row0, per_seq_rows, grid_mode, init, w):
    batch, seq_len, d = x.shape
    x2 = x.reshape(batch * seq_len, d)

    def rows_for(tile):
        per = seq_len // tile
        if per_seq_rows:
            return lambda i: mod_row0 + i // per
        return lambda i: mod_row0

    x1 = _ffn_call(x2, mod3, rows_for(FFN_TILE), 0, w["norm_ffn1"], w["ffn1_in"], w["ffn1_out"])
    mix_tile = MIX_TILE if grid_mode else seq_len
    q, k, vt, ot, gc, gr, xr = _mixin_call(
        x1, seq_len, grid_mode, mod3, rows_for(mix_tile), w["norm_mix"], w["wqk"], w["wvot"], w["wg"], w["wgt"],
        w["wr"], w["conv_w"], w["conv_b"], w["mu"])
    if init is None:
        m_init = None
        r_init = None
    else:
        c0, n0, m0, s0 = init
        m0p = jnp.broadcast_to(jnp.pad(m0, ((0, 0), (0, 0), (0, 8 - M_HEADS)))[..., None], m0.shape[:2] + (8, LANES))
        m_init = (c0, n0, m0p)
        r_init = _pairs_from_heads(s0)
    mo, cs, ns, ms = _mlstm_call(q, k, vt, ot, gc, gr, w["gate_b_row"], w["gate_b_col"], w["head_g_col"],
                                 batch, seq_len, m_init)
    ro, hs = _rwkv_call(xr, w["r_w0"], w["r_w2p"], w["r_a0"], w["r_a2p"], w["r_g2"], w["r_k_k"], w["r_k_a"],
                        w["r_r_k"], w["r_ln_g"], w["r_ln_b"], w["hones"], batch, seq_len, r_init)
    y = _ffn_call(x1, mod3, rows_for(FFN_TILE), 6, w["norm_ffn2"], w["ffn2_in"], w["ffn2_out"],
                  mix=(mo, ro, w["wo_m"], w["wo_r"]), g_final=w["norm_final"])
    states = (cs, ns, ms[:, :, :M_HEADS, 0], hs)
    return y.reshape(batch, seq_len, d), states


def _prepare_weights(ada_w, ada_b, norm_ffn1, ffn1_w_in, ffn1_w_out, norm_mix, mix_w_in, mix_w_out,
                     m_conv_w, m_conv_b, m_gate_b, m_head_g, r_mu, r_w0, r_w2, r_a0, r_a2, r_g2,
                     r_k_k, r_k_a, r_r_k, r_ln_g, r_ln_b, norm_ffn2, ffn2_w_in, ffn2_w_out, norm_final):
    assert ada_w.shape[0] == 1, "single trunk layer"
    lora = r_w2.shape[2]
    nm = 4 * M_WIDTH

    w_in = mix_w_in[0]
    gate_w = w_in[:, nm:nm + GATE_COLS]
    zpad = jnp.zeros((lora, R_WIDTH), F32)

    def dir_pad(w2):
        return jnp.stack([jnp.concatenate([w2[0], zpad], axis=0), jnp.concatenate([zpad, w2[1]], axis=0)])

    head_id = jnp.arange(LANES) // R_HEAD
    return dict(
        norm_ffn1=norm_ffn1, norm_mix=norm_mix, norm_ffn2=norm_ffn2, norm_final=norm_final[None],
        ffn1_in=ffn1_w_in[0].astype(BF16), ffn1_out=ffn1_w_out[0].astype(BF16),
        ffn2_in=ffn2_w_in[0].astype(BF16), ffn2_out=ffn2_w_out[0].astype(BF16),
        wqk=w_in[:, :2 * M_WIDTH].astype(BF16), wvot=w_in[:, 2 * M_WIDTH:nm].T.astype(BF16),
        wg=jnp.pad(gate_w, ((0, 0), (0, LANES - GATE_COLS))).astype(BF16),
        wgt=gate_w.T.astype(BF16),
        wr=w_in[:, nm + GATE_COLS:].astype(BF16),
        wo_m=mix_w_out[0, :M_WIDTH].astype(BF16), wo_r=mix_w_out[0, M_WIDTH:].astype(BF16),
        conv_w=m_conv_w[0], conv_b=m_conv_b, mu=r_mu[0],
        gate_b_row=jnp.pad(m_gate_b[0].reshape(1, GATE_COLS), ((0, 0), (0, LANES - GATE_COLS))),
        gate_b_col=m_gate_b[0].reshape(GATE_COLS, 1),
        head_g_col=m_head_g.reshape(M_WIDTH, 1),
        r_w0=r_w0[0], r_w2p=dir_pad(r_w2[0]).astype(BF16), r_a0=r_a0[0], r_a2p=dir_pad(r_a2[0]).astype(BF16),
        r_g2=r_g2[0].astype(BF16), r_k_k=r_k_k, r_k_a=r_k_a, r_r_k=r_r_k[0].reshape(1, R_WIDTH),
        r_ln_g=r_ln_g, r_ln_b=r_ln_b,
        hones=(head_id[:, None] == head_id[None, :]).astype(BF16),
    )


def kernel(x_prompt, x_sample, c, state_mlstm_C, state_mlstm_n, state_mlstm_m, state_rwkv_S, c_ctx,
           ada_w, ada_b, norm_ffn1, ffn1_w_in, ffn1_w_out, norm_mix, mix_w_in, mix_w_out,
           m_conv_w, m_conv_b, m_gate_b, m_head_g, r_mu, r_w0, r_w2, r_a0, r_a2, r_g2,
           r_k_k, r_k_a, r_r_k, r_ln_g, r_ln_b, norm_ffn2, ffn2_w_in, ffn2_w_out, norm_final):
    w = _prepare_weights(ada_w, ada_b, norm_ffn1, ffn1_w_in, ffn1_w_out, norm_mix, mix_w_in, mix_w_out,
                         m_conv_w, m_conv_b, m_gate_b, m_head_g, r_mu, r_w0, r_w2, r_a0, r_a2, r_g2,
                         r_k_k, r_k_a, r_r_k, r_ln_g, r_ln_b, norm_ffn2, ffn2_w_in, ffn2_w_out, norm_final)
    d = x_prompt.shape[-1]
    dec_batch = x_sample.shape[0]
    cond = jnp.concatenate([c_ctx[None], c, jnp.zeros((16 - 1 - dec_batch, d), F32)], axis=0)
    mod3 = _ada_call(cond, ada_w[0], ada_b).reshape(16, 9, d)

    y_prompt, (cs, ns, ms, ss) = _trunk(x_prompt, mod3, 0, False, False, None, w)
    init = (state_mlstm_C[:, 0], state_mlstm_n[:, 0], state_mlstm_m[:, 0], state_rwkv_S[:, 0])
    y_sample, _ = _trunk(x_sample, mod3, 1, True, True, init, w)
    return (y_prompt, y_sample, cs[:, None], ns[:, None], ms[:, None], ss[:, None])
```

```python
import functools

import jax
import jax.numpy as jnp
from jax import lax
from jax.experimental import pallas as pl
from jax.experimental.pallas import tpu as pltpu

F32 = jnp.float32
BF16 = jnp.bfloat16

EPS = 1e-6
R_LN_EPS = 64e-5
GRID_W = 64
M_HEADS = 4
M_DK = 128
M_DV = 128
M_WIDTH = M_HEADS * M_DV
R_HEADS = 8
R_HEAD = 64
R_WIDTH = R_HEADS * R_HEAD
R_LORA = 128
R_COLS = 3 * R_WIDTH + 3 * R_LORA
GATE_COLS = 16
LANES = 128

FFN_TILE = 1024
FFN_CHUNK = 256
MIX_TILE = 512
M_CHUNK = 256
R_CHUNK = 64
R_CHUNKS_PER_STEP = 8
VMEM_LIMIT = 56 * 1024 * 1024


def _dot(a, b):
    return jnp.dot(a, b, preferred_element_type=F32)


def _dot_nt(a, b):
    return lax.dot_general(a, b, (((1,), (1,)), ((), ())), preferred_element_type=F32)


def _dot_tn(a, b):
    return lax.dot_general(a, b, (((0,), (0,)), ((), ())), preferred_element_type=F32)


def _split(x, parts):
    pieces = []
    for _ in range(parts - 1):
        p = x.astype(BF16)
        pieces.append(p)
        x = x - p.astype(F32)
    pieces.append(x.astype(BF16))
    return pieces


def _sel_dot(mat, x, parts=3):
    return sum(_dot(mat, p) for p in _split(x, parts))


def _dot_sel(x, mat, parts=3):
    return sum(_dot(p, mat) for p in _split(x, parts))


def _silu(x):
    return x * jax.nn.sigmoid(x)


def _rms_mod(x, g, shift, scale):
    y = x * lax.rsqrt(jnp.mean(x * x, axis=-1, keepdims=True) + EPS) * g
    return y * (1.0 + scale) + shift


def _cparams(sem):
    return pltpu.CompilerParams(dimension_semantics=sem, vmem_limit_bytes=VMEM_LIMIT)


def _const_spec(shape):
    nd = len(shape)
    return pl.BlockSpec(shape, lambda *_: (0,) * nd, pipeline_mode=pl.Buffered(1))


def _ada_body(c_ref, w_ref, b_ref, o_ref):
    s = _silu(c_ref[...])
    o_ref[...] = _dot(s.astype(BF16), w_ref[...].astype(BF16)) + b_ref[...]


def _ada_call(cond, w, b):
    rows, d = cond.shape
    n = w.shape[1]
    tn = d
    return pl.pallas_call(
        _ada_body,
        grid=(n // tn,),
        in_specs=[
            pl.BlockSpec((rows, d), lambda j: (0, 0)),
            pl.BlockSpec((d, tn), lambda j: (0, j)),
            pl.BlockSpec((1, tn), lambda j: (0, j)),
        ],
        out_specs=pl.BlockSpec((rows, tn), lambda j: (0, j)),
        out_shape=jax.ShapeDtypeStruct((rows, n), F32),
        compiler_params=_cparams(("arbitrary",)),
        name="ada",
    )(cond, w, b)


def _ffn_body(mod_base, with_mix, with_final, *refs):
    refs = list(refs)
    x_ref = refs.pop(0)
    if with_mix:
        mo_ref, ro_ref, wmm_ref, wmr_ref = refs[:4]
        refs = refs[4:]
    mod_ref, g_ref, win_ref, wout_ref = refs[:4]
    refs = refs[4:]
    if with_final:
        gfin_ref = refs.pop(0)
    out_ref, a_scr = refs

    x = x_ref[...]
    if with_mix:
        mix = _dot(mo_ref[...].astype(BF16), wmm_ref[...]) + _dot(ro_ref[...].astype(BF16), wmr_ref[...])
        x = x + mod_ref[0, mod_base - 1:mod_base, :] * mix
    shift = mod_ref[0, mod_base:mod_base + 1, :]
    scale = mod_ref[0, mod_base + 1:mod_base + 2, :]
    gate = mod_ref[0, mod_base + 2:mod_base + 3, :]
    hb = _rms_mod(x, g_ref[...], shift, scale).astype(BF16)
    dff = wout_ref.shape[0]
    for j in range(dff // FFN_CHUNK):
        lo = j * FFN_CHUNK
        gt = _dot(hb, win_ref[:, lo:lo + FFN_CHUNK])
        up = _dot(hb, win_ref[:, dff + lo:dff + lo + FFN_CHUNK])
        a_scr[:, lo:lo + FFN_CHUNK] = (_silu(gt) * up).astype(BF16)
    y = x + (0.5 * gate) * _dot(a_scr[...], wout_ref[...])
    if with_final:
        y = y * lax.rsqrt(jnp.mean(y * y, axis=-1, keepdims=True) + EPS) * gfin_ref[...]
    out_ref[...] = y


def _ffn_call(x, mod3, row_of_tile, mod_base, g, w_in, w_out, mix=None, g_final=None):
    n, d = x.shape
    tm = FFN_TILE
    dff = w_out.shape[0]
    assert n % tm == 0 and dff % FFN_CHUNK == 0, (n, dff)
    tok = lambda i: (i, 0)
    args = [x]
    specs = [pl.BlockSpec((tm, d), tok)]
    if mix is not None:
        mo, ro, wmm, wmr = mix
        args += [mo, ro, wmm, wmr]
        specs += [pl.BlockSpec((tm, mo.shape[1]), tok), pl.BlockSpec((tm, ro.shape[1]), tok),
                  _const_spec(wmm.shape), _const_spec(wmr.shape)]
    args += [mod3, g, w_in, w_out]
    specs += [pl.BlockSpec((1,) + mod3.shape[1:], lambda i: (row_of_tile(i), 0, 0)),
              _const_spec(g.shape), _const_spec(w_in.shape), _const_spec(w_out.shape)]
    if g_final is not None:
        args.append(g_final)
        specs.append(_const_spec(g_final.shape))
    body = functools.partial(_ffn_body, mod_base, mix is not None, g_final is not None)
    return pl.pallas_call(
        body,
        grid=(n // tm,),
        in_specs=specs,
        out_specs=pl.BlockSpec((tm, d), tok),
        out_shape=jax.ShapeDtypeStruct((n, d), F32),
        scratch_shapes=[pltpu.VMEM((tm, dff), BF16)],
        compiler_params=_cparams(("parallel",)),
        name="ffn_mix" if mix is not None else "ffn",
    )(*args)


def _mixin_body(grid_mode, has_halo, tiles_per_seq, *refs):
    refs = list(refs)
    x_ref = refs.pop(0)
    if has_halo:
        xp_ref, xn_ref = refs[:2]
        refs = refs[2:]
    (mod_ref, g_ref, wqk_ref, wvot_ref, wg_ref, wgt_ref, wr_ref, cw_ref, cb_ref, mu_ref,
     q_ref, k_ref, vt_ref, ot_ref, gc_ref, gr_ref, xr_ref) = refs

    tm = x_ref.shape[0]
    shift = mod_ref[0, 3:4, :]
    scale = mod_ref[0, 4:5, :]
    g = g_ref[...]
    hb = _rms_mod(x_ref[...], g, shift, scale).astype(BF16)
    row = lax.broadcasted_iota(jnp.int32, (tm, 1), 0)
    nqk = 2 * M_WIDTH

    if has_halo:
        i = pl.program_id(0)
        pos = i % tiles_per_seq
        keep_p = (pos > 0).astype(F32)
        keep_n = (pos < tiles_per_seq - 1).astype(F32)
        hp32 = _rms_mod(xp_ref[...], g, shift, scale)
        hn32 = _rms_mod(xn_ref[...], g, shift, scale)
        hp = hp32.astype(BF16)
        hn = hn32.astype(BF16)
        nh = hp32.shape[0]
        edge = 16
        qk_prev = _dot(hp32[nh - edge:, :].astype(BF16), wqk_ref[...])[edge - 1:edge, :] * keep_p
        qk_next = _dot(hn32[:edge, :].astype(BF16), wqk_ref[...])[0:1, :] * keep_n
    else:
        qk_prev = jnp.zeros((1, nqk), F32)
        qk_next = jnp.zeros((1, nqk), F32)

    uqk = _dot(hb, wqk_ref[...])
    ur = _dot(hb, wr_ref[...])
    u_dn = jnp.where(row == 0, qk_prev, pltpu.roll(uqk, 1, 0))
    u_up = jnp.where(row == tm - 1, qk_next, pltpu.roll(uqk, tm - 1, 0))
    qk = _silu(cw_ref[0:1, :] * u_dn + cw_ref[1:2, :] * uqk + cw_ref[2:3, :] * u_up + cb_ref[...])
    q_ref[...] = qk[:, :M_WIDTH].astype(BF16)
    k_ref[...] = (qk[:, M_WIDTH:] * (M_DK ** -0.5)).astype(BF16)
    vo_t = _dot_nt(wvot_ref[...], hb)
    gc_ref[...] = _dot(hb, wg_ref[...])
    gr_ref[...] = _dot_nt(wgt_ref[...], hb)

    if grid_mode:
        col = row % GRID_W
        left = jnp.where(col == 0, 0.0, pltpu.roll(ur, 1, 0))
        right = jnp.where(col == GRID_W - 1, 0.0, pltpu.roll(ur, tm - 1, 0))
        ur_p = _dot(hp, wr_ref[...]) * keep_p
        ur_n = _dot(hn, wr_ref[...]) * keep_n
        up = jnp.concatenate([ur_p, ur[:tm - GRID_W, :]], axis=0)
        down = jnp.concatenate([ur[GRID_W:, :], ur_n], axis=0)
        mu_self = 1.0 - (mu_ref[0:1, :] + mu_ref[1:2, :] + mu_ref[2:3, :] + mu_ref[3:4, :])
        xr = (mu_self * ur + mu_ref[0:1, :] * left + mu_ref[1:2, :] * right
              + mu_ref[2:3, :] * up + mu_ref[3:4, :] * down)
    else:
        left = jnp.where(row == 0, 0.0, pltpu.roll(ur, 1, 0))
        right = jnp.where(row == tm - 1, 0.0, pltpu.roll(ur, tm - 1, 0))
        mu_self = 1.0 - (mu_ref[0:1, :] + mu_ref[1:2, :])
        xr = mu_self * ur + mu_ref[0:1, :] * left + mu_ref[1:2, :] * right
    xr_ref[...] = xr
    vt_ref[...] = vo_t[:M_WIDTH, :].astype(BF16)
    ot_ref[...] = vo_t[M_WIDTH:, :]


def _mixin_call(x, seq_len, grid_mode, mod3, row_of_tile, g, wqk, wvot, wg, wgt, wr, cw, cb, mu):
    n, d = x.shape
    if grid_mode:
        tm = MIX_TILE
        has_halo = True
    else:
        tm = seq_len
        has_halo = False
    assert seq_len % tm == 0 and tm % GRID_W == 0, (seq_len, tm)
    tiles_per_seq = seq_len // tm
    tok = lambda i: (i, 0)
    args = [x]
    specs = [pl.BlockSpec((tm, d), tok)]
    if has_halo:
        hb = R_CHUNK
        per = tm // hb
        last = n // hb - 1
        args += [x, x]
        specs += [pl.BlockSpec((hb, d), lambda i: (jnp.maximum(i * per - 1, 0), 0)),
                  pl.BlockSpec((hb, d), lambda i: (jnp.minimum((i + 1) * per, last), 0))]
    args += [mod3, g, wqk, wvot, wg, wgt, wr, cw, cb, mu]
    specs += [pl.BlockSpec((1,) + mod3.shape[1:], lambda i: (row_of_tile(i), 0, 0))]
    specs += [_const_spec(a.shape) for a in (g, wqk, wvot, wg, wgt, wr, cw, cb, mu)]
    outs = [jax.ShapeDtypeStruct((n, M_WIDTH), BF16)] * 2 + [
        jax.ShapeDtypeStruct((M_WIDTH, n), BF16),
        jax.ShapeDtypeStruct((M_WIDTH, n), F32),
        jax.ShapeDtypeStruct((n, LANES), F32),
        jax.ShapeDtypeStruct((GATE_COLS, n), F32),
        jax.ShapeDtypeStruct((n, R_COLS), F32),
    ]
    out_specs = [pl.BlockSpec((tm, M_WIDTH), tok)] * 2 + [pl.BlockSpec((M_WIDTH, tm), lambda i: (0, i))] * 2 + [
        pl.BlockSpec((tm, LANES), tok),
        pl.BlockSpec((GATE_COLS, tm), lambda i: (0, i)),
        pl.BlockSpec((tm, R_COLS), tok),
    ]
    body = functools.partial(_mixin_body, grid_mode, has_halo, tiles_per_seq)
    return pl.pallas_call(
        body,
        grid=(n // tm,),
        in_specs=specs,
        out_specs=out_specs,
        out_shape=outs,
        compiler_params=_cparams(("parallel",)),
        name="mix_in_grid" if grid_mode else "mix_in_seq",
    )(*args)


def _mlstm_body(zero_init, nc, *refs):
    refs = list(refs)
    q_ref, k_ref, vt_ref, ot_ref, gc_ref, gr_ref, gbr_ref, gbc_ref, hg_ref = refs[:9]
    refs = refs[9:]
    if not zero_init:
        c0_ref, n0_ref, m0_ref = refs[:3]
        refs = refs[3:]
    out_ref, cs_ref, ns_ref, ms_ref, ct_scr, n_scr, m_scr, acc_scr = refs

    d = pl.program_id(1)
    c = pl.program_id(2)
    L = q_ref.shape[0]
    fwd = d == 0
    heads = range(M_HEADS)
    hsl = [slice(h * M_DV, (h + 1) * M_DV) for h in heads]

    @pl.when(c == 0)
    def _():
        if zero_init:
            ct_scr[...] = jnp.zeros_like(ct_scr)
            n_scr[...] = jnp.zeros_like(n_scr)
            m_scr[...] = jnp.zeros_like(m_scr)
        else:
            for h in heads:
                ct_scr[h] = c0_ref[0, 0, h].T
            n_scr[0:M_HEADS, :] = n0_ref[0, 0]
            m_scr[...] = m0_ref[0, 0]

    ri = lax.broadcasted_iota(jnp.int32, (L, L), 0)
    ci = lax.broadcasted_iota(jnp.int32, (L, L), 1)
    sgn = 1 - 2 * d
    tri = ((ci - ri) * sgn <= 0).astype(BF16)
    mask_st = (ri - ci) * sgn <= 0
    tri_t = mask_st.astype(BF16)

    gcol = gc_ref[...] + gbr_ref[...]
    grow = gr_ref[...] + gbc_ref[...]
    bcol = _sel_dot(tri, jax.nn.log_sigmoid(gcol))
    brow = _dot_sel(jax.nn.log_sigmoid(grow), tri_t)

    def pick_col(a, j0, j1):
        return jnp.where(fwd, a[:, j0:j0 + 1], a[:, j1:j1 + 1])

    def pick_row(a, j0, j1):
        return jnp.where(fwd, a[j0:j0 + 1, :], a[j1:j1 + 1, :])

    cc = c + d * (nc - 1 - 2 * c)
    row0 = lax.broadcasted_iota(jnp.int32, (16, LANES), 0) == 0
    row0_l = lax.broadcasted_iota(jnp.int32, (16, L), 0) == 0

    qb = [q_ref[:, hs] for hs in hsl]
    kb = [k_ref[:, hs] for hs in hsl]
    vt = [vt_ref[hs, :] for hs in hsl]
    ra_c = [pick_col(gcol, h, 4 + h) - pick_col(bcol, 8 + h, 12 + h) for h in heads]
    i_r = [pick_row(grow, h, 4 + h) for h in heads]
    b_r = [pick_row(brow, 8 + h, 12 + h) for h in heads]
    b_tot = [jnp.where(fwd, b_r[h][:, L - 1:L], b_r[h][:, 0:1]) for h in heads]
    m_prev = [m_scr[h:h + 1, 0:1] for h in heads]
    n_prev = [n_scr[h:h + 1, :] for h in heads]
    ct_prev = [ct_scr[h] for h in heads]

    kq = [_dot_nt(kb[h], qb[h]) for h in heads]
    qct = [_dot_nt(ct_prev[h].astype(BF16), qb[h]) for h in heads]
    qn = [_dot_nt(jnp.where(row0, n_prev[h], 0.0).astype(BF16), qb[h])[0:1, :] for h in heads]
    dmt = [jnp.where(mask_st, ra_c[h], -jnp.inf) for h in heads]
    mx = [jnp.maximum(m_prev[h], jnp.max(dmt[h], axis=0, keepdims=True)) for h in heads]
    st = [kq[h] * jnp.exp(dmt[h] - mx[h]) for h in heads]
    sc = [jnp.exp(m_prev[h] - mx[h]) for h in heads]
    numt = [_dot(vt[h], st[h].astype(BF16)) + sc[h] * qct[h] for h in heads]
    den = [jnp.sum(st[h], axis=0, keepdims=True) + sc[h] * qn[h] for h in heads]
    ht = [numt[h] / jnp.maximum(jnp.abs(den[h]), jnp.exp(-(b_r[h] + mx[h]))) for h in heads]

    gs = [b_tot[h] - b_r[h] + i_r[h] for h in heads]
    m_new = [jnp.maximum(b_tot[h] + m_prev[h], jnp.max(gs[h], axis=1, keepdims=True)) for h in heads]
    wk = [jnp.exp(gs[h] - m_new[h]) for h in heads]
    decay = [jnp.exp(b_tot[h] + m_prev[h] - m_new[h]) for h in heads]
    ckv = [_dot((vt[h].astype(F32) * wk[h]).astype(BF16), kb[h]) for h in heads]
    nk = [_dot(jnp.where(row0_l, wk[h], 0.0).astype(BF16), kb[h])[0:1, :] for h in heads]
    for h in heads:
        ct_scr[h] = decay[h] * ct_prev[h] + ckv[h]
        n_scr[h:h + 1, :] = decay[h] * n_prev[h] + nk[h]
        m_scr[h:h + 1, :] = jnp.broadcast_to(m_new[h], (1, LANES))

    @pl.when(fwd)
    def _():
        acc_scr[cc] = jnp.concatenate(ht, axis=0)

    @pl.when(jnp.logical_not(fwd))
    def _():
        outs = []
        for h in heads:
            t = jax.nn.sigmoid(ot_ref[hsl[h], :]) * (acc_scr[cc, hsl[h], :] + ht[h])
            t = t * lax.rsqrt(jnp.mean(t * t, axis=0, keepdims=True) + EPS) * hg_ref[hsl[h], :]
            outs.append(t.T)
        out_ref[...] = jnp.concatenate(outs, axis=1).astype(BF16)

    @pl.when(c == nc - 1)
    def _():
        for h in heads:
            cs_ref[0, 0, h] = ct_scr[h].T
        ns_ref[0, 0] = n_scr[0:M_HEADS, :]
        ms_ref[0, 0] = m_scr[...]


def _mlstm_call(q, k, vt, ot, gc, gr, gate_b_row, gate_b_col, head_g_col, batch, seq_len, init):
    n = q.shape[0]
    L = min(M_CHUNK, seq_len)
    assert seq_len % L == 0, (seq_len, L)
    nc = seq_len // L
    zero_init = init is None

    def blk(b, d, c):
        return b * nc + c + d * (nc - 1 - 2 * c)

    tok = lambda b, d, c: (blk(b, d, c), 0)
    tok_t = lambda b, d, c: (0, blk(b, d, c))
    tok_out = lambda b, d, c: (b * nc + nc - 1 - c * d, 0)

    args = [q, k, vt, ot, gc, gr, gate_b_row, gate_b_col, head_g_col]
    specs = [pl.BlockSpec((L, M_WIDTH), tok)] * 2 + [pl.BlockSpec((M_WIDTH, L), tok_t)] * 2 + [
        pl.BlockSpec((L, LANES), tok),
        pl.BlockSpec((GATE_COLS, L), tok_t),
        _const_spec(gate_b_row.shape), _const_spec(gate_b_col.shape), _const_spec(head_g_col.shape),
    ]
    if not zero_init:
        c0, n0, m0 = init
        args += [c0, n0, m0]
        specs += [pl.BlockSpec((1, 1) + c0.shape[2:], lambda b, d, c: (b, d, 0, 0, 0)),
                  pl.BlockSpec((1, 1) + n0.shape[2:], lambda b, d, c: (b, d, 0, 0)),
                  pl.BlockSpec((1, 1) + m0.shape[2:], lambda b, d, c: (b, d, 0, 0))]
    outs = [jax.ShapeDtypeStruct((n, M_WIDTH), BF16),
            jax.ShapeDtypeStruct((batch, 2, M_HEADS, M_DK, M_DV), F32),
            jax.ShapeDtypeStruct((batch, 2, M_HEADS, M_DK), F32),
            jax.ShapeDtypeStruct((batch, 2, 8, LANES), F32)]
    out_specs = [pl.BlockSpec((L, M_WIDTH), tok_out),
                 pl.BlockSpec((1, 1, M_HEADS, M_DK, M_DV), lambda b, d, c: (b, d, 0, 0, 0)),
                 pl.BlockSpec((1, 1, M_HEADS, M_DK), lambda b, d, c: (b, d, 0, 0)),
                 pl.BlockSpec((1, 1, 8, LANES), lambda b, d, c: (b, d, 0, 0))]
    body = functools.partial(_mlstm_body, zero_init, nc)
    return pl.pallas_call(
        body,
        grid=(batch, 2, nc),
        in_specs=specs,
        out_specs=out_specs,
        out_shape=outs,
        scratch_shapes=[pltpu.VMEM((M_HEADS, M_DV, M_DK), F32),
                        pltpu.VMEM((8, LANES), F32),
                        pltpu.VMEM((8, LANES), F32),
                        pltpu.VMEM((nc, M_WIDTH, L), F32)],
        compiler_params=_cparams(("parallel", "arbitrary", "arbitrary")),
        name="mlstm_zero" if zero_init else "mlstm_init",
    )(*args)


def _rwkv_body(zero_init, nb, n_seq, *refs):
    refs = list(refs)
    (xr_ref, w0_ref, w2_ref, a0_ref, a2_ref, a0f_ref, a2f_ref, g2_ref, kk_ref, ka_ref, rk_ref,
     lng_ref, lnb_ref, hones_ref) = refs[:14]
    refs = refs[14:]
    if not zero_init:
        h0_ref = refs.pop(0)
    out_ref, hs_ref, h_scr = refs

    d = pl.program_id(1)
    c = pl.program_id(2)
    LB = xr_ref.shape[0]
    L = R_CHUNK
    n_chunks = LB // L
    P = 2 * L
    fwd = d == 0
    sgn = 1 - 2 * d
    pairs = range(R_HEADS // 2)
    cols = [slice(p * LANES, (p + 1) * LANES) for p in pairs]

    per_seq = n_chunks // n_seq

    def seq_of(j):
        return j // per_seq

    @pl.when(c == 0)
    def _():
        if zero_init:
            h_scr[...] = jnp.zeros_like(h_scr)
        else:
            h_scr[...] = h0_ref[:, 0]

    r = xr_ref[:, 0:R_WIDTH]
    k = xr_ref[:, R_WIDTH:2 * R_WIDTH]
    v = xr_ref[:, 2 * R_WIDTH:3 * R_WIDTH]
    wd = xr_ref[:, 3 * R_WIDTH:3 * R_WIDTH + R_LORA]
    ad = xr_ref[:, 3 * R_WIDTH + R_LORA:3 * R_WIDTH + 2 * R_LORA]
    gin = xr_ref[:, 3 * R_WIDTH + 2 * R_LORA:]
    hones = hones_ref[...]

    def head_sum(x):
        return jnp.concatenate([_dot_sel(x[:, cs], hones, 2) for cs in cols], axis=1)

    tw = jnp.tanh(wd).astype(BF16)
    adb = ad.astype(BF16)
    ww = w0_ref[0] + _dot(tw, w2_ref[0])
    lw = -jnp.exp(-jax.nn.softplus(-ww) - 0.5)
    a = jax.nn.sigmoid(a0_ref[0] + _dot(adb, a2_ref[0]))
    ka = ka_ref[...]
    kd = k * (1.0 + (a - 1.0) * ka)
    kk = k * kk_ref[...]
    kk = kk / jnp.maximum(jnp.sqrt(head_sum(kk * kk)), 1e-12)
    bvec = kk * a

    ri = lax.broadcasted_iota(jnp.int32, (L, L), 0)
    ci = lax.broadcasted_iota(jnp.int32, (L, L), 1)
    tri = ((ci - ri) * sgn <= 0).astype(BF16)
    cl_j = [_sel_dot(tri, lw[j * L:(j + 1) * L, :], 2) for j in range(n_chunks)]
    tot_j = [jnp.where(fwd, x[L - 1:L, :], x[0:1, :]) for x in cl_j]
    cl = jnp.concatenate(cl_j, axis=0)
    tot = jnp.concatenate([jnp.broadcast_to(x, (L, R_WIDTH)) for x in tot_j], axis=0)
    e_in = jnp.exp(cl)
    e_out = jnp.exp(-cl)
    e_end = jnp.exp(tot - cl)
    a_til = -kk * jnp.exp(cl - lw)
    r_til = r * e_in
    b_til = bvec * e_out
    k_til = kd * e_out
    b_hat = bvec * e_end
    k_hat = kd * e_end
    w_end = jnp.exp(tot)

    pr = lax.broadcasted_iota(jnp.int32, (P, P), 0)
    pc = lax.broadcasted_iota(jnp.int32, (P, P), 1)
    same = (pr // L) == (pc // L)
    before = (pc - pr) * sgn < 0
    m_strict = jnp.logical_and(same, before)
    m_incl = jnp.logical_and(same, jnp.logical_or(before, pc == pr))
    eye = pr == pc
    lane = lax.broadcasted_iota(jnp.int32, (L, LANES), 1)
    first = lane < R_HEAD

    def stack(x, j, p):
        x = x[j * L:(j + 1) * L, cols[p]]
        return jnp.concatenate([jnp.where(first, x, 0.0), jnp.where(first, 0.0, x)], axis=0).astype(BF16)

    def twice(x, j, p):
        x = x[j * L:(j + 1) * L, cols[p]].astype(BF16)
        return jnp.concatenate([x, x], axis=0)

    chains = [(j, p) for j in range(n_chunks) for p in pairs]
    at = [stack(a_til, j, p) for j, p in chains]
    rt = [stack(r_til, j, p) for j, p in chains]
    bt = [twice(b_til, j, p) for j, p in chains]
    kt = [twice(k_til, j, p) for j, p in chains]
    bh = [stack(b_hat, j, p) for j, p in chains]
    kh = [stack(k_hat, j, p) for j, p in chains]
    vs = [stack(v, j, p) for j, p in chains]
    ids = range(len(chains))

    big = [_dot_nt(jnp.concatenate([at[i], rt[i]], axis=0), jnp.concatenate([bt[i], kt[i]], axis=0)) for i in ids]
    a_ab = [jnp.where(m_strict, big[i][:P, :P], 0.0) for i in ids]
    a_ak = [jnp.where(m_strict, big[i][:P, P:], 0.0).astype(BF16) for i in ids]
    a_rb = [jnp.where(m_incl, big[i][P:, :P], 0.0).astype(BF16) for i in ids]
    a_rk = [jnp.where(m_incl, big[i][P:, P:], 0.0).astype(BF16) for i in ids]

    def sib_mask(blk):
        return jnp.logical_and((pr // (2 * blk)) == (pc // (2 * blk)), (pr // blk) != (pc // blk))

    def to_wide(x, w):
        lane_blk = lax.broadcasted_iota(jnp.int32, (w, P), 1) // w
        return sum(jnp.where(lane_blk == k, x[k * w:(k + 1) * w, :], 0.0) for k in range(P // w))

    def to_diag(x, w):
        lane_blk = lax.broadcasted_iota(jnp.int32, (w, P), 1) // w
        return jnp.concatenate([jnp.where(lane_blk == k, x, 0.0) for k in range(P // w)], axis=0)

    pair2 = (pr // 2) == (pc // 2)
    tinv = [jnp.where(eye, 1.0, jnp.where(pair2, a_ab[i], 0.0)) for i in ids]
    blk = 2
    for wide in (L // 2, L):
        tw = [to_wide(tinv[i], wide) for i in ids]
        while 2 * blk <= wide:
            sib = sib_mask(blk)
            tb = [tw[i].astype(BF16) for i in ids]
            half = [_dot(tb[i], jnp.where(sib, a_ab[i], 0.0).astype(BF16)) for i in ids]
            tw = [tw[i] + _dot(half[i].astype(BF16), to_diag(tw[i], wide).astype(BF16)) for i in ids]
            blk *= 2
        tinv = [to_diag(tw[i], wide) for i in ids]

    av = [_dot(a_ak[i], vs[i]) for i in ids]
    pq = [_dot(tinv[i].astype(BF16), jnp.concatenate([at[i], av[i].astype(BF16)], axis=1)).astype(BF16) for i in ids]
    ry = [_dot(a_rb[i], pq[i]) for i in ids]
    rkv = [_dot(a_rk[i], vs[i]) for i in ids]
    mg = [_dot_tn(bh[i], pq[i]) for i in ids]
    kv = [_dot_tn(kh[i], vs[i]) for i in ids]
    r_hat = [(rt[i].astype(F32) + ry[i][:, :LANES]).astype(BF16) for i in ids]
    y0 = [ry[i][:, LANES:] + rkv[i] for i in ids]
    m_corr = [mg[i][:, :LANES].astype(BF16) for i in ids]
    g_add = [mg[i][:, LANES:] + kv[i] for i in ids]
    w_col = [jnp.sum(jnp.where(eye, jnp.broadcast_to(w_end[j * L:j * L + 1, cols[p]], (P, P)), 0.0),
                     axis=1, keepdims=True) for j, p in chains]

    def scan_chunks(order):
        y = [None] * n_chunks
        h = {(q, p): h_scr[q, p] for q in range(n_seq) for p in pairs}
        for j in order:
            ys = []
            for p in pairs:
                i = j * len(pairs) + p
                key = (seq_of(j), p)
                hb = h[key].astype(BF16)
                yst = _dot(r_hat[i], hb) + y0[i]
                ys.append(yst[:L, :] + yst[L:, :])
                h[key] = w_col[i] * h[key] + _dot(m_corr[i], hb) + g_add[i]
            y[j] = jnp.concatenate(ys, axis=1)
        for (q, p), val in h.items():
            h_scr[q, p] = val
        return jnp.concatenate(y, axis=0)

    cc = c + d * (nb - 1 - 2 * c)
    rows = pl.ds(pl.multiple_of(cc * LB, LB), LB)

    @pl.when(fwd)
    def _():
        out_ref[rows, :] = scan_chunks(range(n_chunks))

    @pl.when(jnp.logical_not(fwd))
    def _():
        inv_n = 1.0 / R_HEAD
        ysum = out_ref[rows, :] + scan_chunks(reversed(range(n_chunks)))
        mean = head_sum(ysum) * inv_n
        yc = ysum - mean
        var = head_sum(yc * yc) * inv_n
        yn = yc * lax.rsqrt(var + R_LN_EPS) * lng_ref[...] + lnb_ref[...]
        a_f = jax.nn.sigmoid(a0f_ref[...] + _dot(adb, a2f_ref[...]))
        k_bar = k * (1.0 + (0.5 * (a_f + a) - 1.0) * ka)
        bonus = head_sum(r * k_bar * rk_ref[...]) * v
        gate = _dot(jax.nn.sigmoid(gin).astype(BF16), g2_ref[...])
        out_ref[rows, :] = (yn + bonus) * gate

    @pl.when(c == nb - 1)
    def _():
        for q in range(n_seq):
            for p in pairs:
                st = h_scr[q, p].T
                hs_ref[q, 0, 2 * p] = st[:R_HEAD, :R_HEAD]
                hs_ref[q, 0, 2 * p + 1] = st[R_HEAD:, R_HEAD:]


def _rwkv_call(xr, w0, w2p, a0, a2p, g2, k_k, k_a, r_k, ln_g, ln_b, hones, batch, seq_len, h0):
    n = xr.shape[0]
    LB = R_CHUNK * R_CHUNKS_PER_STEP
    n_seq = max(LB // seq_len, 1)
    assert (seq_len % LB == 0 or LB % seq_len == 0) and seq_len % R_CHUNK == 0 and batch % n_seq == 0
    nb = max(seq_len // LB, 1)
    groups = batch // n_seq
    zero_init = h0 is None
    n_pairs = R_HEADS // 2

    def tok(b, d, c):
        return (b * nb + c + d * (nb - 1 - 2 * c), 0)

    by_dir = [w0, w2p, a0, a2p]
    consts = [a0[0], a2p[0], g2, k_k, k_a, r_k, ln_g, ln_b, hones]
    args = [xr] + by_dir + consts
    specs = [pl.BlockSpec((LB, R_COLS), tok)]
    specs += [pl.BlockSpec((1,) + a.shape[1:], lambda b, d, c: (d, 0, 0)) for a in by_dir]
    specs += [_const_spec(a.shape) for a in consts]
    if not zero_init:
        args.append(h0)
        specs.append(pl.BlockSpec((n_seq, 1, n_pairs, LANES, LANES), lambda b, d, c: (b, d, 0, 0, 0)))
    outs = [jax.ShapeDtypeStruct((n, R_WIDTH), F32),
            jax.ShapeDtypeStruct((batch, 2, R_HEADS, R_HEAD, R_HEAD), F32)]
    out_specs = [pl.BlockSpec((n_seq * seq_len, R_WIDTH), lambda b, d, c: (b, 0)),
                 pl.BlockSpec((n_seq, 1, R_HEADS, R_HEAD, R_HEAD), lambda b, d, c: (b, d, 0, 0, 0))]
    body = functools.partial(_rwkv_body, zero_init, nb, n_seq)
    return pl.pallas_call(
        body,
        grid=(groups, 2, nb),
        in_specs=specs,
        out_specs=out_specs,
        out_shape=outs,
        scratch_shapes=[pltpu.VMEM((n_seq, n_pairs, LANES, LANES), F32)],
        compiler_params=_cparams(("parallel", "arbitrary", "arbitrary")),
        name="rwkv_zero" if zero_init else "rwkv_init",
    )(*args)


def _pairs_from_heads(s):
    ht = jnp.swapaxes(s, -1, -2)
    lead = ht.shape[:-3]
    ht = ht.reshape(lead + (R_HEADS // 2, 2, R_HEAD, R_HEAD))
    z = jnp.zeros_like(ht[..., 0, :, :])
    top = jnp.concatenate([ht[..., 0, :, :], z], axis=-1)
    bot = jnp.concatenate([z, ht[..., 1, :, :]], axis=-1)
    return jnp.concatenate([top, bot], axis=-2)


def _trunk(x, mod3, mod_row0, per_seq_rows, grid_mode, init, w):
    batch, seq_len, d = x.shape
    x2 = x.reshape(batch * seq_len, d)

    def rows_for(tile):
        per = seq_len // tile
        if per_seq_rows:
            return lambda i: mod_row0 + i // per
        return lambda i: mod_row0

    x1 = _ffn_call(x2, mod3, rows_for(FFN_TILE), 0, w["norm_ffn1"], w["ffn1_in"], w["ffn1_out"])
    mix_tile = MIX_TILE if grid_mode else seq_len
    q, k, vt, ot, gc, gr, xr = _mixin_call(
        x1, seq_len, grid_mode, mod3, rows_for(mix_tile), w["norm_mix"], w["wqk"], w["wvot"], w["wg"], w["wgt"],
        w["wr"], w["conv_w"], w["conv_b"], w["mu"])
    if init is None:
        m_init = None
        r_init = None
    else:
        c0, n0, m0, s0 = init
        m0p = jnp.broadcast_to(jnp.pad(m0, ((0, 0), (0, 0), (0, 8 - M_HEADS)))[..., None], m0.shape[:2] + (8, LANES))
        m_init = (c0, n0, m0p)
        r_init = _pairs_from_heads(s0)
    mo, cs, ns, ms = _mlstm_call(q, k, vt, ot, gc, gr, w["gate_b_row"], w["gate_b_col"], w["head_g_col"],
                                 batch, seq_len, m_init)
    ro, hs = _rwkv_call(xr, w["r_w0"], w["r_w2p"], w["r_a0"], w["r_a2p"], w["r_g2"], w["r_k_k"], w["r_k_a"],
                        w["r_r_k"], w["r_ln_g"], w["r_ln_b"], w["hones"], batch, seq_len, r_init)
    y = _ffn_call(x1, mod3, rows_for(FFN_TILE), 6, w["norm_ffn2"], w["ffn2_in"], w["ffn2_out"],
                  mix=(mo, ro, w["wo_m"], w["wo_r"]), g_final=w["norm_final"])
    states = (cs, ns, ms[:, :, :M_HEADS, 0], hs)
    return y.reshape(batch, seq_len, d), states


def _prepare_weights(ada_w, ada_b, norm_ffn1, ffn1_w_in, ffn1_w_out, norm_mix, mix_w_in, mix_w_out,
                     m_conv_w, m_conv_b, m_gate_b, m_head_g, r_mu, r_w0, r_w2, r_a0, r_a2, r_g2,
                     r_k_k, r_k_a, r_r_k, r_ln_g, r_ln_b, norm_ffn2, ffn2_w_in, ffn2_w_out, norm_final):
    assert ada_w.shape[0] == 1, "single trunk layer"
    lora = r_w2.shape[2]
    nm = 4 * M_WIDTH

    w_in = mix_w_in[0]
    gate_w = w_in[:, nm:nm + GATE_COLS]
    zpad = jnp.zeros((lora, R_WIDTH), F32)

    def dir_pad(w2):
        return jnp.stack([jnp.concatenate([w2[0], zpad], axis=0), jnp.concatenate([zpad, w2[1]], axis=0)])

    head_id = jnp.arange(LANES) // R_HEAD
    return dict(
        norm_ffn1=norm_ffn1, norm_mix=norm_mix, norm_ffn2=norm_ffn2, norm_final=norm_final[None],
        ffn1_in=ffn1_w_in[0].astype(BF16), ffn1_out=ffn1_w_out[0].astype(BF16),
        ffn2_in=ffn2_w_in[0].astype(BF16), ffn2_out=ffn2_w_out[0].astype(BF16),
        wqk=w_in[:, :2 * M_WIDTH].astype(BF16), wvot=w_in[:, 2 * M_WIDTH:nm].T.astype(BF16),
        wg=jnp.pad(gate_w, ((0, 0), (0, LANES - GATE_COLS))).astype(BF16),
        wgt=gate_w.T.astype(BF16),
        wr=w_in[:, nm + GATE_COLS:].astype(BF16),
        wo_m=mix_w_out[0, :M_WIDTH].astype(BF16), wo_r=mix_w_out[0, M_WIDTH:].astype(BF16),
        conv_w=m_conv_w[0], conv_b=m_conv_b, mu=r_mu[0],
        gate_b_row=jnp.pad(m_gate_b[0].reshape(1, GATE_COLS), ((0, 0), (0, LANES - GATE_COLS))),
        gate_b_col=m_gate_b[0].reshape(GATE_COLS, 1),
        head_g_col=m_head_g.reshape(M_WIDTH, 1),
        r_w0=r_w0[0][:, None, :], r_w2p=dir_pad(r_w2[0]).astype(BF16),
        r_a0=r_a0[0][:, None, :], r_a2p=dir_pad(r_a2[0]).astype(BF16),
        r_g2=r_g2[0].astype(BF16), r_k_k=r_k_k, r_k_a=r_k_a, r_r_k=r_r_k[0].reshape(1, R_WIDTH),
        r_ln_g=r_ln_g, r_ln_b=r_ln_b,
        hones=(head_id[:, None] == head_id[None, :]).astype(BF16),
    )


def kernel(x_prompt, x_sample, c, state_mlstm_C, state_mlstm_n, state_mlstm_m, state_rwkv_S, c_ctx,
           ada_w, ada_b, norm_ffn1, ffn1_w_in, ffn1_w_out, norm_mix, mix_w_in, mix_w_out,
           m_conv_w, m_conv_b, m_gate_b, m_head_g, r_mu, r_w0, r_w2, r_a0, r_a2, r_g2,
           r_k_k, r_k_a, r_r_k, r_ln_g, r_ln_b, norm_ffn2, ffn2_w_in, ffn2_w_out, norm_final):
    w = _prepare_weights(ada_w, ada_b, norm_ffn1, ffn1_w_in, ffn1_w_out, norm_mix, mix_w_in, mix_w_out,
                         m_conv_w, m_conv_b, m_gate_b, m_head_g, r_mu, r_w0, r_w2, r_a0, r_a2, r_g2,
                         r_k_k, r_k_a, r_r_k, r_ln_g, r_ln_b, norm_ffn2, ffn2_w_in, ffn2_w_out, norm_final)
    d = x_prompt.shape[-1]
    dec_batch = x_sample.shape[0]
    cond = jnp.concatenate([c_ctx[None], c, jnp.zeros((16 - 1 - dec_batch, d), F32)], axis=0)
    mod3 = _ada_call(cond, ada_w[0], ada_b).reshape(16, 9, d)

    y_prompt, (cs, ns, ms, ss) = _trunk(x_prompt, mod3, 0, False, False, None, w)
    init = (state_mlstm_C[:, 0], state_mlstm_n[:, 0], state_mlstm_m[:, 0], state_rwkv_S[:, 0])
    y_sample, _ = _trunk(x_sample, mod3, 1, True, True, init, w)
    return (y_prompt, y_sample, cs[:, None], ns[:, None], ms[:, None], ss[:, None])
```

```python
import functools
import math

import jax
import jax.numpy as jnp
from jax import lax
from jax.experimental import pallas as pl
from jax.experimental.pallas import tpu as pltpu

F32 = jnp.float32
BF16 = jnp.bfloat16

EPS = 1e-6
R_LN_EPS = 64e-5
GRID_W = 64
M_HEADS = 4
M_DK = 128
M_DV = 128
M_WIDTH = M_HEADS * M_DV
R_HEADS = 8
R_HEAD = 64
R_WIDTH = R_HEADS * R_HEAD
R_LORA = 128
R_COLS = 3 * R_WIDTH + 3 * R_LORA
GATE_COLS = 16
LANES = 128

FFN_TILE = 1024
FFN_CHUNK = 256
MIX_TILE = 512
M_CHUNK = 256
R_CHUNK = 64
R_CHUNKS_PER_STEP = 8
VMEM_LIMIT = 56 * 1024 * 1024


def _dot(a, b):
    return jnp.dot(a, b, preferred_element_type=F32)


def _dot_nt(a, b):
    return lax.dot_general(a, b, (((1,), (1,)), ((), ())), preferred_element_type=F32)


def _dot_tn(a, b):
    return lax.dot_general(a, b, (((0,), (0,)), ((), ())), preferred_element_type=F32)


def _split(x, parts):
    pieces = []
    for _ in range(parts - 1):
        p = x.astype(BF16)
        pieces.append(p)
        x = x - p.astype(F32)
    pieces.append(x.astype(BF16))
    return pieces


def _sel_dot(mat, x, parts=3):
    return sum(_dot(mat, p) for p in _split(x, parts))


def _dot_sel(x, mat, parts=3):
    return sum(_dot(p, mat) for p in _split(x, parts))


def _silu(x):
    return x * jax.nn.sigmoid(x)


def _rms_mod(x, g, shift, scale):
    y = x * lax.rsqrt(jnp.mean(x * x, axis=-1, keepdims=True) + EPS) * g
    return y * (1.0 + scale) + shift


def _cparams(sem):
    return pltpu.CompilerParams(dimension_semantics=sem, vmem_limit_bytes=VMEM_LIMIT)


def _const_spec(shape):
    nd = len(shape)
    return pl.BlockSpec(shape, lambda *_: (0,) * nd, pipeline_mode=pl.Buffered(1))


def _ada_body(c_ref, w_ref, b_ref, o_ref):
    s = _silu(c_ref[...])
    o_ref[...] = _dot(s.astype(BF16), w_ref[...].astype(BF16)) + b_ref[...]


def _ada_call(cond, w, b):
    rows, d = cond.shape
    n = w.shape[1]
    tn = d
    return pl.pallas_call(
        _ada_body,
        grid=(n // tn,),
        in_specs=[
            pl.BlockSpec((rows, d), lambda j: (0, 0)),
            pl.BlockSpec((d, tn), lambda j: (0, j)),
            pl.BlockSpec((1, tn), lambda j: (0, j)),
        ],
        out_specs=pl.BlockSpec((rows, tn), lambda j: (0, j)),
        out_shape=jax.ShapeDtypeStruct((rows, n), F32),
        compiler_params=_cparams(("arbitrary",)),
        name="ada",
    )(cond, w, b)


def _ffn_body(mod_base, with_mix, with_final, *refs):
    refs = list(refs)
    x_ref = refs.pop(0)
    if with_mix:
        mo_ref, ro_ref, wmm_ref, wmr_ref = refs[:4]
        refs = refs[4:]
    mod_ref, g_ref, win_ref, wout_ref = refs[:4]
    refs = refs[4:]
    if with_final:
        gfin_ref = refs.pop(0)
    out_ref, a_scr = refs

    x = x_ref[...]
    if with_mix:
        mix = _dot(mo_ref[...].astype(BF16), wmm_ref[...]) + _dot(ro_ref[...].astype(BF16), wmr_ref[...])
        x = x + mod_ref[0, mod_base - 1:mod_base, :] * mix
    shift = mod_ref[0, mod_base:mod_base + 1, :]
    scale = mod_ref[0, mod_base + 1:mod_base + 2, :]
    gate = mod_ref[0, mod_base + 2:mod_base + 3, :]
    hb = _rms_mod(x, g_ref[...], shift, scale).astype(BF16)
    dff = wout_ref.shape[0]
    for j in range(dff // FFN_CHUNK):
        lo = j * FFN_CHUNK
        gt = _dot(hb, win_ref[:, lo:lo + FFN_CHUNK])
        up = _dot(hb, win_ref[:, dff + lo:dff + lo + FFN_CHUNK])
        a_scr[:, lo:lo + FFN_CHUNK] = (_silu(gt) * up).astype(BF16)
    y = x + (0.5 * gate) * _dot(a_scr[...], wout_ref[...])
    if with_final:
        y = y * lax.rsqrt(jnp.mean(y * y, axis=-1, keepdims=True) + EPS) * gfin_ref[...]
    out_ref[...] = y


def _ffn_call(x, mod3, row_of_tile, mod_base, g, w_in, w_out, mix=None, g_final=None):
    n, d = x.shape
    tm = FFN_TILE
    dff = w_out.shape[0]
    assert n % tm == 0 and dff % FFN_CHUNK == 0, (n, dff)
    tok = lambda i: (i, 0)
    args = [x]
    specs = [pl.BlockSpec((tm, d), tok)]
    if mix is not None:
        mo, ro, wmm, wmr = mix
        args += [mo, ro, wmm, wmr]
        specs += [pl.BlockSpec((tm, mo.shape[1]), tok), pl.BlockSpec((tm, ro.shape[1]), tok),
                  _const_spec(wmm.shape), _const_spec(wmr.shape)]
    args += [mod3, g, w_in, w_out]
    specs += [pl.BlockSpec((1,) + mod3.shape[1:], lambda i: (row_of_tile(i), 0, 0)),
              _const_spec(g.shape), _const_spec(w_in.shape), _const_spec(w_out.shape)]
    if g_final is not None:
        args.append(g_final)
        specs.append(_const_spec(g_final.shape))
    body = functools.partial(_ffn_body, mod_base, mix is not None, g_final is not None)
    return pl.pallas_call(
        body,
        grid=(n // tm,),
        in_specs=specs,
        out_specs=pl.BlockSpec((tm, d), tok),
        out_shape=jax.ShapeDtypeStruct((n, d), F32),
        scratch_shapes=[pltpu.VMEM((tm, dff), BF16)],
        compiler_params=_cparams(("parallel",)),
        name="ffn_mix" if mix is not None else "ffn",
    )(*args)


def _mixin_body(grid_mode, has_halo, tiles_per_seq, *refs):
    refs = list(refs)
    x_ref = refs.pop(0)
    if has_halo:
        xp_ref, xn_ref = refs[:2]
        refs = refs[2:]
    (mod_ref, g_ref, wqk_ref, wvot_ref, wg_ref, wgt_ref, wr_ref, cw_ref, cb_ref, mu_ref,
     q_ref, k_ref, vt_ref, ot_ref, gc_ref, gr_ref, xr_ref) = refs

    tm = x_ref.shape[0]
    shift = mod_ref[0, 3:4, :]
    scale = mod_ref[0, 4:5, :]
    g = g_ref[...]
    hb = _rms_mod(x_ref[...], g, shift, scale).astype(BF16)
    row = lax.broadcasted_iota(jnp.int32, (tm, 1), 0)
    nqk = 2 * M_WIDTH

    if has_halo:
        i = pl.program_id(0)
        pos = i % tiles_per_seq
        keep_p = (pos > 0).astype(F32)
        keep_n = (pos < tiles_per_seq - 1).astype(F32)
        hp32 = _rms_mod(xp_ref[...], g, shift, scale)
        hn32 = _rms_mod(xn_ref[...], g, shift, scale)
        hp = hp32.astype(BF16)
        hn = hn32.astype(BF16)
        nh = hp32.shape[0]
        edge = 16
        qk_prev = _dot(hp32[nh - edge:, :].astype(BF16), wqk_ref[...])[edge - 1:edge, :] * keep_p
        qk_next = _dot(hn32[:edge, :].astype(BF16), wqk_ref[...])[0:1, :] * keep_n
    else:
        qk_prev = jnp.zeros((1, nqk), F32)
        qk_next = jnp.zeros((1, nqk), F32)

    uqk = _dot(hb, wqk_ref[...])
    ur = _dot(hb, wr_ref[...])
    u_dn = jnp.where(row == 0, qk_prev, pltpu.roll(uqk, 1, 0))
    u_up = jnp.where(row == tm - 1, qk_next, pltpu.roll(uqk, tm - 1, 0))
    qk = _silu(cw_ref[0:1, :] * u_dn + cw_ref[1:2, :] * uqk + cw_ref[2:3, :] * u_up + cb_ref[...])
    q_ref[...] = qk[:, :M_WIDTH].astype(BF16)
    k_ref[...] = (qk[:, M_WIDTH:] * (M_DK ** -0.5)).astype(BF16)
    vo_t = _dot_nt(wvot_ref[...], hb)
    gc_ref[...] = _dot(hb, wg_ref[...])
    gr_ref[...] = _dot_nt(wgt_ref[...], hb)

    if grid_mode:
        col = row % GRID_W
        left = jnp.where(col == 0, 0.0, pltpu.roll(ur, 1, 0))
        right = jnp.where(col == GRID_W - 1, 0.0, pltpu.roll(ur, tm - 1, 0))
        ur_p = _dot(hp, wr_ref[...]) * keep_p
        ur_n = _dot(hn, wr_ref[...]) * keep_n
        up = jnp.concatenate([ur_p, ur[:tm - GRID_W, :]], axis=0)
        down = jnp.concatenate([ur[GRID_W:, :], ur_n], axis=0)
        mu_self = 1.0 - (mu_ref[0:1, :] + mu_ref[1:2, :] + mu_ref[2:3, :] + mu_ref[3:4, :])
        xr = (mu_self * ur + mu_ref[0:1, :] * left + mu_ref[1:2, :] * right
              + mu_ref[2:3, :] * up + mu_ref[3:4, :] * down)
    else:
        left = jnp.where(row == 0, 0.0, pltpu.roll(ur, 1, 0))
        right = jnp.where(row == tm - 1, 0.0, pltpu.roll(ur, tm - 1, 0))
        mu_self = 1.0 - (mu_ref[0:1, :] + mu_ref[1:2, :])
        xr = mu_self * ur + mu_ref[0:1, :] * left + mu_ref[1:2, :] * right
    xr_ref[...] = xr
    vt_ref[...] = vo_t[:M_WIDTH, :].astype(BF16)
    ot_ref[...] = vo_t[M_WIDTH:, :]


def _mixin_call(x, seq_len, grid_mode, mod3, row_of_tile, g, wqk, wvot, wg, wgt, wr, cw, cb, mu):
    n, d = x.shape
    if grid_mode:
        tm = MIX_TILE
        has_halo = True
    else:
        tm = seq_len
        has_halo = False
    assert seq_len % tm == 0 and tm % GRID_W == 0, (seq_len, tm)
    tiles_per_seq = seq_len // tm
    tok = lambda i: (i, 0)
    args = [x]
    specs = [pl.BlockSpec((tm, d), tok)]
    if has_halo:
        hb = R_CHUNK
        per = tm // hb
        last = n // hb - 1
        args += [x, x]
        specs += [pl.BlockSpec((hb, d), lambda i: (jnp.maximum(i * per - 1, 0), 0)),
                  pl.BlockSpec((hb, d), lambda i: (jnp.minimum((i + 1) * per, last), 0))]
    args += [mod3, g, wqk, wvot, wg, wgt, wr, cw, cb, mu]
    specs += [pl.BlockSpec((1,) + mod3.shape[1:], lambda i: (row_of_tile(i), 0, 0))]
    specs += [_const_spec(a.shape) for a in (g, wqk, wvot, wg, wgt, wr, cw, cb, mu)]
    outs = [jax.ShapeDtypeStruct((n, M_WIDTH), BF16)] * 2 + [
        jax.ShapeDtypeStruct((M_WIDTH, n), BF16),
        jax.ShapeDtypeStruct((M_WIDTH, n), F32),
        jax.ShapeDtypeStruct((n, LANES), F32),
        jax.ShapeDtypeStruct((GATE_COLS, n), F32),
        jax.ShapeDtypeStruct((n, R_COLS), F32),
    ]
    out_specs = [pl.BlockSpec((tm, M_WIDTH), tok)] * 2 + [pl.BlockSpec((M_WIDTH, tm), lambda i: (0, i))] * 2 + [
        pl.BlockSpec((tm, LANES), tok),
        pl.BlockSpec((GATE_COLS, tm), lambda i: (0, i)),
        pl.BlockSpec((tm, R_COLS), tok),
    ]
    body = functools.partial(_mixin_body, grid_mode, has_halo, tiles_per_seq)
    return pl.pallas_call(
        body,
        grid=(n // tm,),
        in_specs=specs,
        out_specs=out_specs,
        out_shape=outs,
        compiler_params=_cparams(("parallel",)),
        name="mix_in_grid" if grid_mode else "mix_in_seq",
    )(*args)


def _mlstm_body(zero_init, nc, *refs):
    refs = list(refs)
    q_ref, k_ref, vt_ref, ot_ref, gc_ref, gr_ref, gbr_ref, gbc_ref, hg_ref = refs[:9]
    refs = refs[9:]
    if not zero_init:
        c0_ref, n0_ref, m0_ref = refs[:3]
        refs = refs[3:]
    out_ref, cs_ref, ns_ref, ms_ref, ct_scr, n_scr, m_scr, acc_scr = refs

    d = pl.program_id(1)
    c = pl.program_id(2)
    L = q_ref.shape[0]
    fwd = d == 0
    heads = range(M_HEADS)
    hsl = [slice(h * M_DV, (h + 1) * M_DV) for h in heads]

    @pl.when(c == 0)
    def _():
        if zero_init:
            ct_scr[...] = jnp.zeros_like(ct_scr)
            n_scr[...] = jnp.zeros_like(n_scr)
            m_scr[...] = jnp.zeros_like(m_scr)
        else:
            for h in heads:
                ct_scr[h] = c0_ref[0, 0, h].T
            n_scr[0:M_HEADS, :] = n0_ref[0, 0]
            m_scr[...] = m0_ref[0, 0]

    ri = lax.broadcasted_iota(jnp.int32, (L, L), 0)
    ci = lax.broadcasted_iota(jnp.int32, (L, L), 1)
    sgn = 1 - 2 * d
    tri = ((ci - ri) * sgn <= 0).astype(BF16)
    mask_st = (ri - ci) * sgn <= 0
    tri_t = mask_st.astype(BF16)

    gcol = gc_ref[...] + gbr_ref[...]
    grow = gr_ref[...] + gbc_ref[...]
    bcol = _sel_dot(tri, jax.nn.log_sigmoid(gcol))
    brow = _dot_sel(jax.nn.log_sigmoid(grow), tri_t)

    def pick_col(a, j0, j1):
        return jnp.where(fwd, a[:, j0:j0 + 1], a[:, j1:j1 + 1])

    def pick_row(a, j0, j1):
        return jnp.where(fwd, a[j0:j0 + 1, :], a[j1:j1 + 1, :])

    cc = c + d * (nc - 1 - 2 * c)
    row0 = lax.broadcasted_iota(jnp.int32, (16, LANES), 0) == 0
    row0_l = lax.broadcasted_iota(jnp.int32, (16, L), 0) == 0

    qb = [q_ref[:, hs] for hs in hsl]
    kb = [k_ref[:, hs] for hs in hsl]
    vt = [vt_ref[hs, :] for hs in hsl]
    ra_c = [pick_col(gcol, h, 4 + h) - pick_col(bcol, 8 + h, 12 + h) for h in heads]
    i_r = [pick_row(grow, h, 4 + h) for h in heads]
    b_r = [pick_row(brow, 8 + h, 12 + h) for h in heads]
    b_tot = [jnp.where(fwd, b_r[h][:, L - 1:L], b_r[h][:, 0:1]) for h in heads]
    m_prev = [m_scr[h:h + 1, 0:1] for h in heads]
    n_prev = [n_scr[h:h + 1, :] for h in heads]
    ct_prev = [ct_scr[h] for h in heads]

    kq = [_dot_nt(kb[h], qb[h]) for h in heads]
    qct = [_dot_nt(ct_prev[h].astype(BF16), qb[h]) for h in heads]
    qn = [_dot_nt(jnp.where(row0, n_prev[h], 0.0).astype(BF16), qb[h])[0:1, :] for h in heads]
    dmt = [jnp.where(mask_st, ra_c[h], -jnp.inf) for h in heads]
    mx = [jnp.maximum(m_prev[h], jnp.max(dmt[h], axis=0, keepdims=True)) for h in heads]
    st = [kq[h] * jnp.exp(dmt[h] - mx[h]) for h in heads]
    sc = [jnp.exp(m_prev[h] - mx[h]) for h in heads]
    numt = [_dot(vt[h], st[h].astype(BF16)) + sc[h] * qct[h] for h in heads]
    den = [jnp.sum(st[h], axis=0, keepdims=True) + sc[h] * qn[h] for h in heads]
    ht = [numt[h] / jnp.maximum(jnp.abs(den[h]), jnp.exp(-(b_r[h] + mx[h]))) for h in heads]

    gs = [b_tot[h] - b_r[h] + i_r[h] for h in heads]
    m_new = [jnp.maximum(b_tot[h] + m_prev[h], jnp.max(gs[h], axis=1, keepdims=True)) for h in heads]
    wk = [jnp.exp(gs[h] - m_new[h]) for h in heads]
    decay = [jnp.exp(b_tot[h] + m_prev[h] - m_new[h]) for h in heads]
    ckv = [_dot((vt[h].astype(F32) * wk[h]).astype(BF16), kb[h]) for h in heads]
    nk = [_dot(jnp.where(row0_l, wk[h], 0.0).astype(BF16), kb[h])[0:1, :] for h in heads]
    for h in heads:
        ct_scr[h] = decay[h] * ct_prev[h] + ckv[h]
        n_scr[h:h + 1, :] = decay[h] * n_prev[h] + nk[h]
        m_scr[h:h + 1, :] = jnp.broadcast_to(m_new[h], (1, LANES))

    @pl.when(fwd)
    def _():
        acc_scr[cc] = jnp.concatenate(ht, axis=0)

    @pl.when(jnp.logical_not(fwd))
    def _():
        outs = []
        for h in heads:
            t = jax.nn.sigmoid(ot_ref[hsl[h], :]) * (acc_scr[cc, hsl[h], :] + ht[h])
            t = t * lax.rsqrt(jnp.mean(t * t, axis=0, keepdims=True) + EPS) * hg_ref[hsl[h], :]
            outs.append(t.T)
        out_ref[...] = jnp.concatenate(outs, axis=1).astype(BF16)

    @pl.when(c == nc - 1)
    def _():
        for h in heads:
            cs_ref[0, 0, h] = ct_scr[h].T
        ns_ref[0, 0] = n_scr[0:M_HEADS, :]
        ms_ref[0, 0] = m_scr[...]


def _mlstm_call(q, k, vt, ot, gc, gr, gate_b_row, gate_b_col, head_g_col, batch, seq_len, init):
    n = q.shape[0]
    L = min(M_CHUNK, seq_len)
    assert seq_len % L == 0, (seq_len, L)
    nc = seq_len // L
    zero_init = init is None

    def blk(b, d, c):
        return b * nc + c + d * (nc - 1 - 2 * c)

    tok = lambda b, d, c: (blk(b, d, c), 0)
    tok_t = lambda b, d, c: (0, blk(b, d, c))
    tok_out = lambda b, d, c: (b * nc + nc - 1 - c * d, 0)

    args = [q, k, vt, ot, gc, gr, gate_b_row, gate_b_col, head_g_col]
    specs = [pl.BlockSpec((L, M_WIDTH), tok)] * 2 + [pl.BlockSpec((M_WIDTH, L), tok_t)] * 2 + [
        pl.BlockSpec((L, LANES), tok),
        pl.BlockSpec((GATE_COLS, L), tok_t),
        _const_spec(gate_b_row.shape), _const_spec(gate_b_col.shape), _const_spec(head_g_col.shape),
    ]
    if not zero_init:
        c0, n0, m0 = init
        args += [c0, n0, m0]
        specs += [pl.BlockSpec((1, 1) + c0.shape[2:], lambda b, d, c: (b, d, 0, 0, 0)),
                  pl.BlockSpec((1, 1) + n0.shape[2:], lambda b, d, c: (b, d, 0, 0)),
                  pl.BlockSpec((1, 1) + m0.shape[2:], lambda b, d, c: (b, d, 0, 0))]
    outs = [jax.ShapeDtypeStruct((n, M_WIDTH), BF16),
            jax.ShapeDtypeStruct((batch, 2, M_HEADS, M_DK, M_DV), F32),
            jax.ShapeDtypeStruct((batch, 2, M_HEADS, M_DK), F32),
            jax.ShapeDtypeStruct((batch, 2, 8, LANES), F32)]
    out_specs = [pl.BlockSpec((L, M_WIDTH), tok_out),
                 pl.BlockSpec((1, 1, M_HEADS, M_DK, M_DV), lambda b, d, c: (b, d, 0, 0, 0)),
                 pl.BlockSpec((1, 1, M_HEADS, M_DK), lambda b, d, c: (b, d, 0, 0)),
                 pl.BlockSpec((1, 1, 8, LANES), lambda b, d, c: (b, d, 0, 0))]
    body = functools.partial(_mlstm_body, zero_init, nc)
    return pl.pallas_call(
        body,
        grid=(batch, 2, nc),
        in_specs=specs,
        out_specs=out_specs,
        out_shape=outs,
        scratch_shapes=[pltpu.VMEM((M_HEADS, M_DV, M_DK), F32),
                        pltpu.VMEM((8, LANES), F32),
                        pltpu.VMEM((8, LANES), F32),
                        pltpu.VMEM((nc, M_WIDTH, L), F32)],
        compiler_params=_cparams(("parallel", "arbitrary", "arbitrary")),
        name="mlstm_zero" if zero_init else "mlstm_init",
    )(*args)


def _rwkv_body(zero_init, nb, n_seq, *refs):
    refs = list(refs)
    (xr_ref, w0_ref, w2_ref, a0_ref, a2_ref, a0f_ref, a2f_ref, g2_ref, kk_ref, ka_ref, rk_ref,
     lng_ref, lnb_ref, hones_ref) = refs[:14]
    refs = refs[14:]
    if not zero_init:
        h0_ref = refs.pop(0)
    out_ref, hs_ref, h_scr = refs

    d = pl.program_id(1)
    c = pl.program_id(2)
    LB = xr_ref.shape[0]
    L = R_CHUNK
    n_chunks = LB // L
    P = 2 * L
    fwd = d == 0
    sgn = 1 - 2 * d
    pairs = range(R_HEADS // 2)
    cols = [slice(p * LANES, (p + 1) * LANES) for p in pairs]

    per_seq = n_chunks // n_seq

    def seq_of(j):
        return j // per_seq

    @pl.when(c == 0)
    def _():
        if zero_init:
            h_scr[...] = jnp.zeros_like(h_scr)
        else:
            h_scr[...] = h0_ref[:, 0]

    r = xr_ref[:, 0:R_WIDTH]
    k = xr_ref[:, R_WIDTH:2 * R_WIDTH]
    v = xr_ref[:, 2 * R_WIDTH:3 * R_WIDTH]
    wd = xr_ref[:, 3 * R_WIDTH:3 * R_WIDTH + R_LORA]
    ad = xr_ref[:, 3 * R_WIDTH + R_LORA:3 * R_WIDTH + 2 * R_LORA]
    gin = xr_ref[:, 3 * R_WIDTH + 2 * R_LORA:]
    hones = hones_ref[...]

    def head_sum(x):
        return jnp.concatenate([_dot_sel(x[:, cs], hones, 2) for cs in cols], axis=1)

    tw = jnp.tanh(wd).astype(BF16)
    adb = ad.astype(BF16)
    ww = w0_ref[0] + _dot(tw, w2_ref[0])
    lw = -math.exp(-0.5) * jax.nn.sigmoid(ww)
    a = jax.nn.sigmoid(a0_ref[0] + _dot(adb, a2_ref[0]))
    ka = ka_ref[...]
    kd = k * (1.0 + (a - 1.0) * ka)
    kk = k * kk_ref[...]
    kk = kk * lax.rsqrt(jnp.maximum(head_sum(kk * kk), 1e-24))
    bvec = kk * a

    ri = lax.broadcasted_iota(jnp.int32, (L, L), 0)
    ci = lax.broadcasted_iota(jnp.int32, (L, L), 1)
    tri = ((ci - ri) * sgn <= 0).astype(BF16)
    cl_j = [_sel_dot(tri, lw[j * L:(j + 1) * L, :], 2) for j in range(n_chunks)]
    tot_j = [jnp.where(fwd, x[L - 1:L, :], x[0:1, :]) for x in cl_j]
    cl = jnp.concatenate(cl_j, axis=0)
    w_end = [jnp.exp(x) for x in tot_j]
    e_in = jnp.exp(cl)
    e_out = jnp.exp(-cl)
    e_end = jnp.concatenate([e_out[j * L:(j + 1) * L, :] * w_end[j] for j in range(n_chunks)], axis=0)
    a_til = -kk * jnp.exp(cl - lw)
    r_til = r * e_in
    b_til = bvec * e_out
    k_til = kd * e_out
    b_hat = bvec * e_end
    k_hat = kd * e_end

    pr = lax.broadcasted_iota(jnp.int32, (P, P), 0)
    pc = lax.broadcasted_iota(jnp.int32, (P, P), 1)
    same = (pr // L) == (pc // L)
    before = (pc - pr) * sgn < 0
    m_strict = jnp.logical_and(same, before)
    m_incl = jnp.logical_and(same, jnp.logical_or(before, pc == pr))
    eye = pr == pc
    lane = lax.broadcasted_iota(jnp.int32, (L, LANES), 1)
    first = lane < R_HEAD

    def stack(x, j, p):
        x = x[j * L:(j + 1) * L, cols[p]].astype(BF16)
        zero = jnp.zeros_like(x)
        return jnp.concatenate([jnp.where(first, x, zero), jnp.where(first, zero, x)], axis=0)

    def twice(x, j, p):
        x = x[j * L:(j + 1) * L, cols[p]].astype(BF16)
        return jnp.concatenate([x, x], axis=0)

    chains = [(j, p) for j in range(n_chunks) for p in pairs]
    at = [stack(a_til, j, p) for j, p in chains]
    rt = [stack(r_til, j, p) for j, p in chains]
    bt = [twice(b_til, j, p) for j, p in chains]
    kt = [twice(k_til, j, p) for j, p in chains]
    bh = [stack(b_hat, j, p) for j, p in chains]
    kh = [stack(k_hat, j, p) for j, p in chains]
    vs = [stack(v, j, p) for j, p in chains]
    ids = range(len(chains))

    big = [_dot_nt(jnp.concatenate([at[i], rt[i]], axis=0), jnp.concatenate([bt[i], kt[i]], axis=0)) for i in ids]
    a_ab = [jnp.where(m_strict, big[i][:P, :P], 0.0) for i in ids]
    a_ak = [jnp.where(m_strict, big[i][:P, P:], 0.0).astype(BF16) for i in ids]
    a_rb = [jnp.where(m_incl, big[i][P:, :P], 0.0).astype(BF16) for i in ids]
    a_rk = [jnp.where(m_incl, big[i][P:, P:], 0.0).astype(BF16) for i in ids]

    def sib_mask(blk):
        return jnp.logical_and((pr // (2 * blk)) == (pc // (2 * blk)), (pr // blk) != (pc // blk))

    def to_wide(x, w):
        lane_blk = lax.broadcasted_iota(jnp.int32, (w, P), 1) // w
        return sum(jnp.where(lane_blk == k, x[k * w:(k + 1) * w, :], 0.0) for k in range(P // w))

    def to_diag(x, w):
        lane_blk = lax.broadcasted_iota(jnp.int32, (w, P), 1) // w
        return jnp.concatenate([jnp.where(lane_blk == k, x, 0.0) for k in range(P // w)], axis=0)

    pair2 = (pr // 2) == (pc // 2)
    tinv = [jnp.where(eye, 1.0, jnp.where(pair2, a_ab[i], 0.0)) for i in ids]
    blk = 2
    for wide in (L // 2, L):
        tw = [to_wide(tinv[i], wide) for i in ids]
        while 2 * blk <= wide:
            sib = sib_mask(blk)
            tb = [tw[i].astype(BF16) for i in ids]
            half = [_dot(tb[i], jnp.where(sib, a_ab[i], 0.0).astype(BF16)) for i in ids]
            tw = [tw[i] + _dot(half[i].astype(BF16), to_diag(tw[i], wide).astype(BF16)) for i in ids]
            blk *= 2
        tinv = [to_diag(tw[i], wide) for i in ids]

    av = [_dot(a_ak[i], vs[i]) for i in ids]
    pq = [_dot(tinv[i].astype(BF16), jnp.concatenate([at[i], av[i].astype(BF16)], axis=1)).astype(BF16) for i in ids]
    ry = [_dot(a_rb[i], pq[i]) for i in ids]
    rkv = [_dot(a_rk[i], vs[i]) for i in ids]
    mg = [_dot_tn(bh[i], pq[i]) for i in ids]
    kv = [_dot_tn(kh[i], vs[i]) for i in ids]
    r_hat = [(rt[i].astype(F32) + ry[i][:, :LANES]).astype(BF16) for i in ids]
    y0 = [ry[i][:, LANES:] + rkv[i] for i in ids]
    m_corr = [mg[i][:, :LANES].astype(BF16) for i in ids]
    g_add = [mg[i][:, LANES:] + kv[i] for i in ids]
    w_col = [jnp.sum(jnp.where(eye, jnp.broadcast_to(w_end[j][:, cols[p]], (P, P)), 0.0),
                     axis=1, keepdims=True) for j, p in chains]

    def scan_chunks(order):
        y = [None] * n_chunks
        h = {(q, p): h_scr[q, p] for q in range(n_seq) for p in pairs}
        for j in order:
            ys = []
            for p in pairs:
                i = j * len(pairs) + p
                key = (seq_of(j), p)
                hb = h[key].astype(BF16)
                yst = _dot(r_hat[i], hb) + y0[i]
                ys.append(yst[:L, :] + yst[L:, :])
                h[key] = w_col[i] * h[key] + _dot(m_corr[i], hb) + g_add[i]
            y[j] = jnp.concatenate(ys, axis=1)
        for (q, p), val in h.items():
            h_scr[q, p] = val
        return jnp.concatenate(y, axis=0)

    cc = c + d * (nb - 1 - 2 * c)
    rows = pl.ds(pl.multiple_of(cc * LB, LB), LB)

    @pl.when(fwd)
    def _():
        out_ref[rows, :] = scan_chunks(range(n_chunks))

    @pl.when(jnp.logical_not(fwd))
    def _():
        inv_n = 1.0 / R_HEAD
        ysum = out_ref[rows, :] + scan_chunks(reversed(range(n_chunks)))
        mean = head_sum(ysum) * inv_n
        yc = ysum - mean
        var = head_sum(yc * yc) * inv_n
        yn = yc * lax.rsqrt(var + R_LN_EPS) * lng_ref[...] + lnb_ref[...]
        a_f = jax.nn.sigmoid(a0f_ref[...] + _dot(adb, a2f_ref[...]))
        k_bar = k * (1.0 + (0.5 * (a_f + a) - 1.0) * ka)
        bonus = head_sum(r * k_bar * rk_ref[...]) * v
        gate = _dot(jax.nn.sigmoid(gin).astype(BF16), g2_ref[...])
        out_ref[rows, :] = (yn + bonus) * gate

    @pl.when(c == nb - 1)
    def _():
        for q in range(n_seq):
            for p in pairs:
                st = h_scr[q, p].T
                hs_ref[q, 0, 2 * p] = st[:R_HEAD, :R_HEAD]
                hs_ref[q, 0, 2 * p + 1] = st[R_HEAD:, R_HEAD:]


def _rwkv_call(xr, w0, w2p, a0, a2p, g2, k_k, k_a, r_k, ln_g, ln_b, hones, batch, seq_len, h0):
    n = xr.shape[0]
    LB = R_CHUNK * R_CHUNKS_PER_STEP
    n_seq = max(LB // seq_len, 1)
    assert (seq_len % LB == 0 or LB % seq_len == 0) and seq_len % R_CHUNK == 0 and batch % n_seq == 0
    nb = max(seq_len // LB, 1)
    groups = batch // n_seq
    zero_init = h0 is None
    n_pairs = R_HEADS // 2

    def tok(b, d, c):
        return (b * nb + c + d * (nb - 1 - 2 * c), 0)

    by_dir = [w0, w2p, a0, a2p]
    consts = [a0[0], a2p[0], g2, k_k, k_a, r_k, ln_g, ln_b, hones]
    args = [xr] + by_dir + consts
    specs = [pl.BlockSpec((LB, R_COLS), tok)]
    specs += [pl.BlockSpec((1,) + a.shape[1:], lambda b, d, c: (d, 0, 0)) for a in by_dir]
    specs += [_const_spec(a.shape) for a in consts]
    if not zero_init:
        args.append(h0)
        specs.append(pl.BlockSpec((n_seq, 1, n_pairs, LANES, LANES), lambda b, d, c: (b, d, 0, 0, 0)))
    outs = [jax.ShapeDtypeStruct((n, R_WIDTH), F32),
            jax.ShapeDtypeStruct((batch, 2, R_HEADS, R_HEAD, R_HEAD), F32)]
    out_specs = [pl.BlockSpec((n_seq * seq_len, R_WIDTH), lambda b, d, c: (b, 0)),
                 pl.BlockSpec((n_seq, 1, R_HEADS, R_HEAD, R_HEAD), lambda b, d, c: (b, d, 0, 0, 0))]
    body = functools.partial(_rwkv_body, zero_init, nb, n_seq)
    return pl.pallas_call(
        body,
        grid=(groups, 2, nb),
        in_specs=specs,
        out_specs=out_specs,
        out_shape=outs,
        scratch_shapes=[pltpu.VMEM((n_seq, n_pairs, LANES, LANES), F32)],
        compiler_params=_cparams(("parallel", "arbitrary", "arbitrary")),
        name="rwkv_zero" if zero_init else "rwkv_init",
    )(*args)


def _pairs_from_heads(s):
    ht = jnp.swapaxes(s, -1, -2)
    lead = ht.shape[:-3]
    ht = ht.reshape(lead + (R_HEADS // 2, 2, R_HEAD, R_HEAD))
    z = jnp.zeros_like(ht[..., 0, :, :])
    top = jnp.concatenate([ht[..., 0, :, :], z], axis=-1)
    bot = jnp.concatenate([z, ht[..., 1, :, :]], axis=-1)
    return jnp.concatenate([top, bot], axis=-2)


def _trunk(x, mod3, mod_row0, per_seq_rows, grid_mode, init, w):
    batch, seq_len, d = x.shape
    x2 = x.reshape(batch * seq_len, d)

    def rows_for(tile):
        per = seq_len // tile
        if per_seq_rows:
            return lambda i: mod_row0 + i // per
        return lambda i: mod_row0

    x1 = _ffn_call(x2, mod3, rows_for(FFN_TILE), 0, w["norm_ffn1"], w["ffn1_in"], w["ffn1_out"])
    mix_tile = MIX_TILE if grid_mode else seq_len
    q, k, vt, ot, gc, gr, xr = _mixin_call(
        x1, seq_len, grid_mode, mod3, rows_for(mix_tile), w["norm_mix"], w["wqk"], w["wvot"], w["wg"], w["wgt"],
        w["wr"], w["conv_w"], w["conv_b"], w["mu"])
    if init is None:
        m_init = None
        r_init = None
    else:
        c0, n0, m0, s0 = init
        m0p = jnp.broadcast_to(jnp.pad(m0, ((0, 0), (0, 0), (0, 8 - M_HEADS)))[..., None], m0.shape[:2] + (8, LANES))
        m_init = (c0, n0, m0p)
        r_init = _pairs_from_heads(s0)
    mo, cs, ns, ms = _mlstm_call(q, k, vt, ot, gc, gr, w["gate_b_row"], w["gate_b_col"], w["head_g_col"],
                                 batch, seq_len, m_init)
    ro, hs = _rwkv_call(xr, w["r_w0"], w["r_w2p"], w["r_a0"], w["r_a2p"], w["r_g2"], w["r_k_k"], w["r_k_a"],
                        w["r_r_k"], w["r_ln_g"], w["r_ln_b"], w["hones"], batch, seq_len, r_init)
    y = _ffn_call(x1, mod3, rows_for(FFN_TILE), 6, w["norm_ffn2"], w["ffn2_in"], w["ffn2_out"],
                  mix=(mo, ro, w["wo_m"], w["wo_r"]), g_final=w["norm_final"])
    states = (cs, ns, ms[:, :, :M_HEADS, 0], hs)
    return y.reshape(batch, seq_len, d), states


def _prepare_weights(ada_w, ada_b, norm_ffn1, ffn1_w_in, ffn1_w_out, norm_mix, mix_w_in, mix_w_out,
                     m_conv_w, m_conv_b, m_gate_b, m_head_g, r_mu, r_w0, r_w2, r_a0, r_a2, r_g2,
                     r_k_k, r_k_a, r_r_k, r_ln_g, r_ln_b, norm_ffn2, ffn2_w_in, ffn2_w_out, norm_final):
    assert ada_w.shape[0] == 1, "single trunk layer"
    lora = r_w2.shape[2]
    nm = 4 * M_WIDTH

    w_in = mix_w_in[0]
    gate_w = w_in[:, nm:nm + GATE_COLS]
    zpad = jnp.zeros((lora, R_WIDTH), F32)

    def dir_pad(w2):
        return jnp.stack([jnp.concatenate([w2[0], zpad], axis=0), jnp.concatenate([zpad, w2[1]], axis=0)])

    head_id = jnp.arange(LANES) // R_HEAD
    return dict(
        norm_ffn1=norm_ffn1, norm_mix=norm_mix, norm_ffn2=norm_ffn2, norm_final=norm_final[None],
        ffn1_in=ffn1_w_in[0].astype(BF16), ffn1_out=ffn1_w_out[0].astype(BF16),
        ffn2_in=ffn2_w_in[0].astype(BF16), ffn2_out=ffn2_w_out[0].astype(BF16),
        wqk=w_in[:, :2 * M_WIDTH].astype(BF16), wvot=w_in[:, 2 * M_WIDTH:nm].T.astype(BF16),
        wg=jnp.pad(gate_w, ((0, 0), (0, LANES - GATE_COLS))).astype(BF16),
        wgt=gate_w.T.astype(BF16),
        wr=w_in[:, nm + GATE_COLS:].astype(BF16),
        wo_m=mix_w_out[0, :M_WIDTH].astype(BF16), wo_r=mix_w_out[0, M_WIDTH:].astype(BF16),
        conv_w=m_conv_w[0], conv_b=m_conv_b, mu=r_mu[0],
        gate_b_row=jnp.pad(m_gate_b[0].reshape(1, GATE_COLS), ((0, 0), (0, LANES - GATE_COLS))),
        gate_b_col=m_gate_b[0].reshape(GATE_COLS, 1),
        head_g_col=m_head_g.reshape(M_WIDTH, 1),
        r_w0=r_w0[0][:, None, :], r_w2p=dir_pad(r_w2[0]).astype(BF16),
        r_a0=r_a0[0][:, None, :], r_a2p=dir_pad(r_a2[0]).astype(BF16),
        r_g2=r_g2[0].astype(BF16), r_k_k=r_k_k, r_k_a=r_k_a, r_r_k=r_r_k[0].reshape(1, R_WIDTH),
        r_ln_g=r_ln_g, r_ln_b=r_ln_b,
        hones=(head_id[:, None] == head_id[None, :]).astype(BF16),
    )


def kernel(x_prompt, x_sample, c, state_mlstm_C, state_mlstm_n, state_mlstm_m, state_rwkv_S, c_ctx,
           ada_w, ada_b, norm_ffn1, ffn1_w_in, ffn1_w_out, norm_mix, mix_w_in, mix_w_out,
           m_conv_w, m_conv_b, m_gate_b, m_head_g, r_mu, r_w0, r_w2, r_a0, r_a2, r_g2,
           r_k_k, r_k_a, r_r_k, r_ln_g, r_ln_b, norm_ffn2, ffn2_w_in, ffn2_w_out, norm_final):
    w = _prepare_weights(ada_w, ada_b, norm_ffn1, ffn1_w_in, ffn1_w_out, norm_mix, mix_w_in, mix_w_out,
                         m_conv_w, m_conv_b, m_gate_b, m_head_g, r_mu, r_w0, r_w2, r_a0, r_a2, r_g2,
                         r_k_k, r_k_a, r_r_k, r_ln_g, r_ln_b, norm_ffn2, ffn2_w_in, ffn2_w_out, norm_final)
    d = x_prompt.shape[-1]
    dec_batch = x_sample.shape[0]
    cond = jnp.concatenate([c_ctx[None], c, jnp.zeros((16 - 1 - dec_batch, d), F32)], axis=0)
    mod3 = _ada_call(cond, ada_w[0], ada_b).reshape(16, 9, d)

    y_prompt, (cs, ns, ms, ss) = _trunk(x_prompt, mod3, 0, False, False, None, w)
    init = (state_mlstm_C[:, 0], state_mlstm_n[:, 0], state_mlstm_m[:, 0], state_rwkv_S[:, 0])
    y_sample, _ = _trunk(x_sample, mod3, 1, True, True, init, w)
    return (y_prompt, y_sample, cs[:, None], ns[:, None], ms[:, None], ss[:, None])
```

```python
import functools
import math

import jax
import jax.numpy as jnp
from jax import lax
from jax.experimental import pallas as pl
from jax.experimental.pallas import tpu as pltpu

F32 = jnp.float32
BF16 = jnp.bfloat16

EPS = 1e-6
R_LN_EPS = 64e-5
GRID_W = 64
M_HEADS = 4
M_DK = 128
M_DV = 128
M_WIDTH = M_HEADS * M_DV
R_HEADS = 8
R_HEAD = 64
R_WIDTH = R_HEADS * R_HEAD
R_LORA = 128
R_COLS = 3 * R_WIDTH + 3 * R_LORA
GATE_COLS = 16
LANES = 128

FFN_TILE = 1024
FFN_CHUNK = 256
MIX_TILE = 512
M_CHUNK = 256
R_CHUNK = 64
R_CHUNKS_PER_STEP = 8
VMEM_LIMIT = 56 * 1024 * 1024


def _dot(a, b):
    return jnp.dot(a, b, preferred_element_type=F32)


def _dot_nt(a, b):
    return lax.dot_general(a, b, (((1,), (1,)), ((), ())), preferred_element_type=F32)


def _dot_tn(a, b):
    return lax.dot_general(a, b, (((0,), (0,)), ((), ())), preferred_element_type=F32)


def _split(x, parts):
    pieces = []
    for _ in range(parts - 1):
        p = x.astype(BF16)
        pieces.append(p)
        x = x - p.astype(F32)
    pieces.append(x.astype(BF16))
    return pieces


def _sel_dot(mat, x, parts=3):
    return sum(_dot(mat, p) for p in _split(x, parts))


def _dot_sel(x, mat, parts=3):
    return sum(_dot(p, mat) for p in _split(x, parts))


def _silu(x):
    return x * jax.nn.sigmoid(x)


def _rms_mod(x, g, shift, scale):
    y = x * lax.rsqrt(jnp.mean(x * x, axis=-1, keepdims=True) + EPS) * g
    return y * (1.0 + scale) + shift


def _cparams(sem):
    return pltpu.CompilerParams(dimension_semantics=sem, vmem_limit_bytes=VMEM_LIMIT)


def _const_spec(shape):
    nd = len(shape)
    return pl.BlockSpec(shape, lambda *_: (0,) * nd, pipeline_mode=pl.Buffered(1))


def _ada_body(c_ref, w_ref, b_ref, o_ref):
    s = _silu(c_ref[...])
    o_ref[...] = _dot(s.astype(BF16), w_ref[...].astype(BF16)) + b_ref[...]


def _ada_call(cond, w, b):
    rows, d = cond.shape
    n = w.shape[1]
    tn = d
    return pl.pallas_call(
        _ada_body,
        grid=(n // tn,),
        in_specs=[
            pl.BlockSpec((rows, d), lambda j: (0, 0)),
            pl.BlockSpec((d, tn), lambda j: (0, j)),
            pl.BlockSpec((1, tn), lambda j: (0, j)),
        ],
        out_specs=pl.BlockSpec((rows, tn), lambda j: (0, j)),
        out_shape=jax.ShapeDtypeStruct((rows, n), F32),
        compiler_params=_cparams(("arbitrary",)),
        name="ada",
    )(cond, w, b)


def _ffn_body(mod_base, with_mix, with_final, *refs):
    refs = list(refs)
    x_ref = refs.pop(0)
    if with_mix:
        mo_ref, ro_ref, wmm_ref, wmr_ref = refs[:4]
        refs = refs[4:]
    mod_ref, g_ref, win_ref, wout_ref = refs[:4]
    refs = refs[4:]
    if with_final:
        gfin_ref = refs.pop(0)
    out_ref, a_scr = refs

    x = x_ref[...]
    if with_mix:
        mix = _dot(mo_ref[...].astype(BF16), wmm_ref[...]) + _dot(ro_ref[...].astype(BF16), wmr_ref[...])
        x = x + mod_ref[0, mod_base - 1:mod_base, :] * mix
    shift = mod_ref[0, mod_base:mod_base + 1, :]
    scale = mod_ref[0, mod_base + 1:mod_base + 2, :]
    gate = mod_ref[0, mod_base + 2:mod_base + 3, :]
    hb = _rms_mod(x, g_ref[...], shift, scale).astype(BF16)
    dff = wout_ref.shape[0]
    for j in range(dff // FFN_CHUNK):
        lo = j * FFN_CHUNK
        gt = _dot(hb, win_ref[:, lo:lo + FFN_CHUNK])
        up = _dot(hb, win_ref[:, dff + lo:dff + lo + FFN_CHUNK])
        a_scr[:, lo:lo + FFN_CHUNK] = (_silu(gt) * up).astype(BF16)
    y = x + (0.5 * gate) * _dot(a_scr[...], wout_ref[...])
    if with_final:
        y = y * lax.rsqrt(jnp.mean(y * y, axis=-1, keepdims=True) + EPS) * gfin_ref[...]
    out_ref[...] = y


def _ffn_call(x, mod3, row_of_tile, mod_base, g, w_in, w_out, mix=None, g_final=None):
    n, d = x.shape
    tm = FFN_TILE
    dff = w_out.shape[0]
    assert n % tm == 0 and dff % FFN_CHUNK == 0, (n, dff)
    tok = lambda i: (i, 0)
    args = [x]
    specs = [pl.BlockSpec((tm, d), tok)]
    if mix is not None:
        mo, ro, wmm, wmr = mix
        args += [mo, ro, wmm, wmr]
        specs += [pl.BlockSpec((tm, mo.shape[1]), tok), pl.BlockSpec((tm, ro.shape[1]), tok),
                  _const_spec(wmm.shape), _const_spec(wmr.shape)]
    args += [mod3, g, w_in, w_out]
    specs += [pl.BlockSpec((1,) + mod3.shape[1:], lambda i: (row_of_tile(i), 0, 0)),
              _const_spec(g.shape), _const_spec(w_in.shape), _const_spec(w_out.shape)]
    if g_final is not None:
        args.append(g_final)
        specs.append(_const_spec(g_final.shape))
    body = functools.partial(_ffn_body, mod_base, mix is not None, g_final is not None)
    return pl.pallas_call(
        body,
        grid=(n // tm,),
        in_specs=specs,
        out_specs=pl.BlockSpec((tm, d), tok),
        out_shape=jax.ShapeDtypeStruct((n, d), F32),
        scratch_shapes=[pltpu.VMEM((tm, dff), BF16)],
        compiler_params=_cparams(("parallel",)),
        name="ffn_mix" if mix is not None else "ffn",
    )(*args)


def _mixin_body(grid_mode, has_halo, tiles_per_seq, *refs):
    refs = list(refs)
    x_ref = refs.pop(0)
    if has_halo:
        xp_ref, xn_ref = refs[:2]
        refs = refs[2:]
    (mod_ref, g_ref, wqk_ref, wvot_ref, wg_ref, wgt_ref, wr_ref, cw_ref, cb_ref, mu_ref,
     q_ref, k_ref, vt_ref, ot_ref, gc_ref, gr_ref, xr_ref) = refs

    tm = x_ref.shape[0]
    shift = mod_ref[0, 3:4, :]
    scale = mod_ref[0, 4:5, :]
    g = g_ref[...]
    hb = _rms_mod(x_ref[...], g, shift, scale).astype(BF16)
    row = lax.broadcasted_iota(jnp.int32, (tm, 1), 0)
    nqk = 2 * M_WIDTH

    if has_halo:
        i = pl.program_id(0)
        pos = i % tiles_per_seq
        keep_p = (pos > 0).astype(F32)
        keep_n = (pos < tiles_per_seq - 1).astype(F32)
        hp32 = _rms_mod(xp_ref[...], g, shift, scale)
        hn32 = _rms_mod(xn_ref[...], g, shift, scale)
        hp = hp32.astype(BF16)
        hn = hn32.astype(BF16)
        nh = hp32.shape[0]
        edge = 16
        qk_prev = _dot(hp32[nh - edge:, :].astype(BF16), wqk_ref[...])[edge - 1:edge, :] * keep_p
        qk_next = _dot(hn32[:edge, :].astype(BF16), wqk_ref[...])[0:1, :] * keep_n
    else:
        qk_prev = jnp.zeros((1, nqk), F32)
        qk_next = jnp.zeros((1, nqk), F32)

    uqk = _dot(hb, wqk_ref[...])
    ur = _dot(hb, wr_ref[...])
    u_dn = jnp.where(row == 0, qk_prev, pltpu.roll(uqk, 1, 0))
    u_up = jnp.where(row == tm - 1, qk_next, pltpu.roll(uqk, tm - 1, 0))
    qk = _silu(cw_ref[0:1, :] * u_dn + cw_ref[1:2, :] * uqk + cw_ref[2:3, :] * u_up + cb_ref[...])
    q_ref[...] = qk[:, :M_WIDTH].astype(BF16)
    k_ref[...] = (qk[:, M_WIDTH:] * (M_DK ** -0.5)).astype(BF16)
    vo_t = _dot_nt(wvot_ref[...], hb)
    gc_ref[...] = _dot(hb, wg_ref[...])
    gr_ref[...] = _dot_nt(wgt_ref[...], hb)

    if grid_mode:
        col = row % GRID_W
        left = jnp.where(col == 0, 0.0, pltpu.roll(ur, 1, 0))
        right = jnp.where(col == GRID_W - 1, 0.0, pltpu.roll(ur, tm - 1, 0))
        ur_p = _dot(hp, wr_ref[...]) * keep_p
        ur_n = _dot(hn, wr_ref[...]) * keep_n
        up = jnp.concatenate([ur_p, ur[:tm - GRID_W, :]], axis=0)
        down = jnp.concatenate([ur[GRID_W:, :], ur_n], axis=0)
        mu_self = 1.0 - (mu_ref[0:1, :] + mu_ref[1:2, :] + mu_ref[2:3, :] + mu_ref[3:4, :])
        xr = (mu_self * ur + mu_ref[0:1, :] * left + mu_ref[1:2, :] * right
              + mu_ref[2:3, :] * up + mu_ref[3:4, :] * down)
    else:
        left = jnp.where(row == 0, 0.0, pltpu.roll(ur, 1, 0))
        right = jnp.where(row == tm - 1, 0.0, pltpu.roll(ur, tm - 1, 0))
        mu_self = 1.0 - (mu_ref[0:1, :] + mu_ref[1:2, :])
        xr = mu_self * ur + mu_ref[0:1, :] * left + mu_ref[1:2, :] * right
    xr_ref[...] = xr
    vt_ref[...] = vo_t[:M_WIDTH, :].astype(BF16)
    ot_ref[...] = vo_t[M_WIDTH:, :]


def _mixin_call(x, seq_len, grid_mode, mod3, row_of_tile, g, wqk, wvot, wg, wgt, wr, cw, cb, mu):
    n, d = x.shape
    if grid_mode:
        tm = MIX_TILE
        has_halo = True
    else:
        tm = seq_len
        has_halo = False
    assert seq_len % tm == 0 and tm % GRID_W == 0, (seq_len, tm)
    tiles_per_seq = seq_len // tm
    tok = lambda i: (i, 0)
    args = [x]
    specs = [pl.BlockSpec((tm, d), tok)]
    if has_halo:
        hb = R_CHUNK
        per = tm // hb
        last = n // hb - 1
        args += [x, x]
        specs += [pl.BlockSpec((hb, d), lambda i: (jnp.maximum(i * per - 1, 0), 0)),
                  pl.BlockSpec((hb, d), lambda i: (jnp.minimum((i + 1) * per, last), 0))]
    args += [mod3, g, wqk, wvot, wg, wgt, wr, cw, cb, mu]
    specs += [pl.BlockSpec((1,) + mod3.shape[1:], lambda i: (row_of_tile(i), 0, 0))]
    specs += [_const_spec(a.shape) for a in (g, wqk, wvot, wg, wgt, wr, cw, cb, mu)]
    outs = [jax.ShapeDtypeStruct((n, M_WIDTH), BF16)] * 2 + [
        jax.ShapeDtypeStruct((M_WIDTH, n), BF16),
        jax.ShapeDtypeStruct((M_WIDTH, n), F32),
        jax.ShapeDtypeStruct((n, LANES), F32),
        jax.ShapeDtypeStruct((GATE_COLS, n), F32),
        jax.ShapeDtypeStruct((n, R_COLS), F32),
    ]
    out_specs = [pl.BlockSpec((tm, M_WIDTH), tok)] * 2 + [pl.BlockSpec((M_WIDTH, tm), lambda i: (0, i))] * 2 + [
        pl.BlockSpec((tm, LANES), tok),
        pl.BlockSpec((GATE_COLS, tm), lambda i: (0, i)),
        pl.BlockSpec((tm, R_COLS), tok),
    ]
    body = functools.partial(_mixin_body, grid_mode, has_halo, tiles_per_seq)
    return pl.pallas_call(
        body,
        grid=(n // tm,),
        in_specs=specs,
        out_specs=out_specs,
        out_shape=outs,
        compiler_params=_cparams(("parallel",)),
        name="mix_in_grid" if grid_mode else "mix_in_seq",
    )(*args)


def _mlstm_body(zero_init, nc, n_seq, *refs):
    refs = list(refs)
    q_ref, k_ref, vt_ref, ot_ref, gc_ref, gr_ref, gbr_ref, gbc_ref, hg_ref = refs[:9]
    refs = refs[9:]
    if not zero_init:
        c0_ref, n0_ref, m0_ref = refs[:3]
        refs = refs[3:]
    out_ref, cs_ref, ns_ref, ms_ref, ct_scr, n_scr, m_scr, acc_scr = refs

    d = pl.program_id(1)
    c = pl.program_id(2)
    L = q_ref.shape[0] // n_seq
    fwd = d == 0
    heads = range(M_HEADS)
    hsl = [slice(h * M_DV, (h + 1) * M_DV) for h in heads]
    seqs = range(n_seq)
    rsl = [slice(q * L, (q + 1) * L) for q in seqs]

    @pl.when(c == 0)
    def _():
        if zero_init:
            ct_scr[...] = jnp.zeros_like(ct_scr)
            n_scr[...] = jnp.zeros_like(n_scr)
            m_scr[...] = jnp.zeros_like(m_scr)
        else:
            for h in heads:
                ct_scr[0, h] = c0_ref[0, 0, h].T
            n_scr[0, 0:M_HEADS, :] = n0_ref[0, 0]
            m_scr[0] = m0_ref[0, 0]

    ri = lax.broadcasted_iota(jnp.int32, (L, L), 0)
    ci = lax.broadcasted_iota(jnp.int32, (L, L), 1)
    sgn = 1 - 2 * d
    tri = ((ci - ri) * sgn <= 0).astype(BF16)
    mask_st = (ri - ci) * sgn <= 0
    tri_t = mask_st.astype(BF16)

    gcol = gc_ref[...] + gbr_ref[...]
    grow = gr_ref[...] + gbc_ref[...]
    lf_col = jax.nn.log_sigmoid(gcol)
    lf_row = jax.nn.log_sigmoid(grow)
    bcol = jnp.concatenate([_sel_dot(tri, lf_col[rs, :]) for rs in rsl], axis=0)
    brow = jnp.concatenate([_dot_sel(lf_row[:, rs], tri_t) for rs in rsl], axis=1)

    def pick_col(a, j0, j1):
        return jnp.where(fwd, a[:, j0:j0 + 1], a[:, j1:j1 + 1])

    def pick_row(a, j0, j1):
        return jnp.where(fwd, a[j0:j0 + 1, :], a[j1:j1 + 1, :])

    cc = c + d * (nc - 1 - 2 * c)
    row0 = lax.broadcasted_iota(jnp.int32, (16, LANES), 0) == 0
    row0_l = lax.broadcasted_iota(jnp.int32, (16, L), 0) == 0

    chains = [(q, h) for q in seqs for h in heads]
    ids = range(len(chains))
    qb = [q_ref[rsl[q], hsl[h]] for q, h in chains]
    kb = [k_ref[rsl[q], hsl[h]] for q, h in chains]
    vt = [vt_ref[hsl[h], rsl[q]] for q, h in chains]
    ra_c = [pick_col(gcol, h, 4 + h)[rsl[q], :] - pick_col(bcol, 8 + h, 12 + h)[rsl[q], :]
            for q, h in chains]
    i_r = [pick_row(grow, h, 4 + h)[:, rsl[q]] for q, h in chains]
    b_r = [pick_row(brow, 8 + h, 12 + h)[:, rsl[q]] for q, h in chains]
    b_tot = [jnp.where(fwd, b_r[i][:, L - 1:L], b_r[i][:, 0:1]) for i in ids]
    m_prev = [m_scr[q, h:h + 1, 0:1] for q, h in chains]
    n_prev = [n_scr[q, h:h + 1, :] for q, h in chains]
    ct_prev = [ct_scr[q, h] for q, h in chains]

    kq = [_dot_nt(kb[i], qb[i]) for i in ids]
    qct = [_dot_nt(ct_prev[i].astype(BF16), qb[i]) for i in ids]
    qn = [_dot_nt(jnp.where(row0, n_prev[i], 0.0).astype(BF16), qb[i])[0:1, :] for i in ids]
    dmt = [jnp.where(mask_st, ra_c[i], -jnp.inf) for i in ids]
    mx = [jnp.maximum(m_prev[i], jnp.max(dmt[i], axis=0, keepdims=True)) for i in ids]
    st = [kq[i] * jnp.exp(dmt[i] - mx[i]) for i in ids]
    sc = [jnp.exp(m_prev[i] - mx[i]) for i in ids]
    numt = [_dot(vt[i], st[i].astype(BF16)) + sc[i] * qct[i] for i in ids]
    den = [jnp.sum(st[i], axis=0, keepdims=True) + sc[i] * qn[i] for i in ids]
    ht = [numt[i] / jnp.maximum(jnp.abs(den[i]), jnp.exp(-(b_r[i] + mx[i]))) for i in ids]

    gs = [b_tot[i] - b_r[i] + i_r[i] for i in ids]
    m_new = [jnp.maximum(b_tot[i] + m_prev[i], jnp.max(gs[i], axis=1, keepdims=True)) for i in ids]
    wk = [jnp.exp(gs[i] - m_new[i]) for i in ids]
    decay = [jnp.exp(b_tot[i] + m_prev[i] - m_new[i]) for i in ids]
    ckv = [_dot((vt[i].astype(F32) * wk[i]).astype(BF16), kb[i]) for i in ids]
    nk = [_dot(jnp.where(row0_l, wk[i], 0.0).astype(BF16), kb[i])[0:1, :] for i in ids]
    for i, (q, h) in enumerate(chains):
        ct_scr[q, h] = decay[i] * ct_prev[i] + ckv[i]
        n_scr[q, h:h + 1, :] = decay[i] * n_prev[i] + nk[i]
        m_scr[q, h:h + 1, :] = jnp.broadcast_to(m_new[i], (1, LANES))

    def block_t(x):
        return jnp.concatenate([jnp.concatenate(x[q * M_HEADS:(q + 1) * M_HEADS], axis=0) for q in seqs], axis=1)

    @pl.when(fwd)
    def _():
        acc_scr[cc] = block_t(ht)

    @pl.when(jnp.logical_not(fwd))
    def _():
        outs = []
        for q in seqs:
            row = []
            for h in heads:
                i = q * M_HEADS + h
                t = jax.nn.sigmoid(ot_ref[hsl[h], rsl[q]]) * (acc_scr[cc, hsl[h], rsl[q]] + ht[i])
                t = t * lax.rsqrt(jnp.mean(t * t, axis=0, keepdims=True) + EPS) * hg_ref[hsl[h], :]
                row.append(t.T)
            outs.append(jnp.concatenate(row, axis=1))
        out_ref[...] = jnp.concatenate(outs, axis=0).astype(BF16)

    @pl.when(c == nc - 1)
    def _():
        for q in seqs:
            for h in heads:
                cs_ref[q, 0, h] = ct_scr[q, h].T
            ns_ref[q, 0] = n_scr[q, 0:M_HEADS, :]
            ms_ref[q, 0] = m_scr[q]


def _mlstm_call(q, k, vt, ot, gc, gr, gate_b_row, gate_b_col, head_g_col, batch, seq_len, init):
    n = q.shape[0]
    L = min(M_CHUNK, seq_len)
    assert seq_len % L == 0, (seq_len, L)
    nc = seq_len // L
    zero_init = init is None
    n_seq = 2 if (nc == 1 and zero_init and batch % 2 == 0) else 1
    LB = n_seq * L
    groups = batch // n_seq

    def blk(b, d, c):
        return b * nc + c + d * (nc - 1 - 2 * c)

    tok = lambda b, d, c: (blk(b, d, c), 0)
    tok_t = lambda b, d, c: (0, blk(b, d, c))
    tok_out = lambda b, d, c: (b * nc + nc - 1 - c * d, 0)

    args = [q, k, vt, ot, gc, gr, gate_b_row, gate_b_col, head_g_col]
    specs = [pl.BlockSpec((LB, M_WIDTH), tok)] * 2 + [pl.BlockSpec((M_WIDTH, LB), tok_t)] * 2 + [
        pl.BlockSpec((LB, LANES), tok),
        pl.BlockSpec((GATE_COLS, LB), tok_t),
        _const_spec(gate_b_row.shape), _const_spec(gate_b_col.shape), _const_spec(head_g_col.shape),
    ]
    if not zero_init:
        c0, n0, m0 = init
        args += [c0, n0, m0]
        specs += [pl.BlockSpec((1, 1) + c0.shape[2:], lambda b, d, c: (b, d, 0, 0, 0)),
                  pl.BlockSpec((1, 1) + n0.shape[2:], lambda b, d, c: (b, d, 0, 0)),
                  pl.BlockSpec((1, 1) + m0.shape[2:], lambda b, d, c: (b, d, 0, 0))]
    outs = [jax.ShapeDtypeStruct((n, M_WIDTH), BF16),
            jax.ShapeDtypeStruct((batch, 2, M_HEADS, M_DK, M_DV), F32),
            jax.ShapeDtypeStruct((batch, 2, M_HEADS, M_DK), F32),
            jax.ShapeDtypeStruct((batch, 2, 8, LANES), F32)]
    out_specs = [pl.BlockSpec((LB, M_WIDTH), tok_out),
                 pl.BlockSpec((n_seq, 1, M_HEADS, M_DK, M_DV), lambda b, d, c: (b, d, 0, 0, 0)),
                 pl.BlockSpec((n_seq, 1, M_HEADS, M_DK), lambda b, d, c: (b, d, 0, 0)),
                 pl.BlockSpec((n_seq, 1, 8, LANES), lambda b, d, c: (b, d, 0, 0))]
    body = functools.partial(_mlstm_body, zero_init, nc, n_seq)
    return pl.pallas_call(
        body,
        grid=(groups, 2, nc),
        in_specs=specs,
        out_specs=out_specs,
        out_shape=outs,
        scratch_shapes=[pltpu.VMEM((n_seq, M_HEADS, M_DV, M_DK), F32),
                        pltpu.VMEM((n_seq, 8, LANES), F32),
                        pltpu.VMEM((n_seq, 8, LANES), F32),
                        pltpu.VMEM((nc, M_WIDTH, LB), F32)],
        compiler_params=_cparams(("parallel", "arbitrary", "arbitrary")),
        name="mlstm_zero" if zero_init else "mlstm_init",
    )(*args)


def _rwkv_body(zero_init, nb, n_seq, *refs):
    refs = list(refs)
    (xr_ref, w0_ref, w2_ref, a0_ref, a2_ref, a0f_ref, a2f_ref, g2_ref, kk_ref, ka_ref, rk_ref,
     lng_ref, lnb_ref, hones_ref) = refs[:14]
    refs = refs[14:]
    if not zero_init:
        h0_ref = refs.pop(0)
    out_ref, hs_ref, h_scr = refs

    d = pl.program_id(1)
    c = pl.program_id(2)
    LB = xr_ref.shape[0]
    L = R_CHUNK
    n_chunks = LB // L
    P = 2 * L
    fwd = d == 0
    sgn = 1 - 2 * d
    pairs = range(R_HEADS // 2)
    cols = [slice(p * LANES, (p + 1) * LANES) for p in pairs]

    per_seq = n_chunks // n_seq

    def seq_of(j):
        return j // per_seq

    @pl.when(c == 0)
    def _():
        if zero_init:
            h_scr[...] = jnp.zeros_like(h_scr)
        else:
            h_scr[...] = h0_ref[:, 0]

    r = xr_ref[:, 0:R_WIDTH]
    k = xr_ref[:, R_WIDTH:2 * R_WIDTH]
    v = xr_ref[:, 2 * R_WIDTH:3 * R_WIDTH]
    wd = xr_ref[:, 3 * R_WIDTH:3 * R_WIDTH + R_LORA]
    ad = xr_ref[:, 3 * R_WIDTH + R_LORA:3 * R_WIDTH + 2 * R_LORA]
    gin = xr_ref[:, 3 * R_WIDTH + 2 * R_LORA:]
    hones = hones_ref[...]

    def head_sum(x):
        return jnp.concatenate([_dot_sel(x[:, cs], hones, 2) for cs in cols], axis=1)

    tw = jnp.tanh(wd).astype(BF16)
    adb = ad.astype(BF16)
    ww = w0_ref[0] + _dot(tw, w2_ref[0])
    lw = -math.exp(-0.5) * jax.nn.sigmoid(ww)
    a = jax.nn.sigmoid(a0_ref[0] + _dot(adb, a2_ref[0]))
    ka = ka_ref[...]
    kd = k * (1.0 + (a - 1.0) * ka)
    kk = k * kk_ref[...]
    kk = kk * lax.rsqrt(jnp.maximum(head_sum(kk * kk), 1e-24))
    bvec = kk * a

    ri = lax.broadcasted_iota(jnp.int32, (L, L), 0)
    ci = lax.broadcasted_iota(jnp.int32, (L, L), 1)
    tri = ((ci - ri) * sgn <= 0).astype(BF16)
    cl_j = [_sel_dot(tri, lw[j * L:(j + 1) * L, :], 2) for j in range(n_chunks)]
    tot_j = [jnp.where(fwd, x[L - 1:L, :], x[0:1, :]) for x in cl_j]
    cl = jnp.concatenate(cl_j, axis=0)
    w_end = [jnp.exp(x) for x in tot_j]
    e_in = jnp.exp(cl)
    e_out = jnp.exp(-cl)
    e_end = jnp.concatenate([e_out[j * L:(j + 1) * L, :] * w_end[j] for j in range(n_chunks)], axis=0)
    a_til = -kk * jnp.exp(cl - lw)
    r_til = r * e_in
    b_til = bvec * e_out
    k_til = kd * e_out
    b_hat = bvec * e_end
    k_hat = kd * e_end

    pr = lax.broadcasted_iota(jnp.int32, (P, P), 0)
    pc = lax.broadcasted_iota(jnp.int32, (P, P), 1)
    same = (pr // L) == (pc // L)
    before = (pc - pr) * sgn < 0
    m_strict = jnp.logical_and(same, before)
    m_incl = jnp.logical_and(same, jnp.logical_or(before, pc == pr))
    eye = pr == pc
    lane = lax.broadcasted_iota(jnp.int32, (L, LANES), 1)
    first = lane < R_HEAD

    def stack(x, j, p):
        x = x[j * L:(j + 1) * L, cols[p]].astype(BF16)
        zero = jnp.zeros_like(x)
        return jnp.concatenate([jnp.where(first, x, zero), jnp.where(first, zero, x)], axis=0)

    def twice(x, j, p):
        x = x[j * L:(j + 1) * L, cols[p]].astype(BF16)
        return jnp.concatenate([x, x], axis=0)

    chains = [(j, p) for j in range(n_chunks) for p in pairs]
    at = [stack(a_til, j, p) for j, p in chains]
    rt = [stack(r_til, j, p) for j, p in chains]
    bt = [twice(b_til, j, p) for j, p in chains]
    kt = [twice(k_til, j, p) for j, p in chains]
    bh = [stack(b_hat, j, p) for j, p in chains]
    kh = [stack(k_hat, j, p) for j, p in chains]
    vs = [stack(v, j, p) for j, p in chains]
    ids = range(len(chains))

    big = [_dot_nt(jnp.concatenate([at[i], rt[i]], axis=0), jnp.concatenate([bt[i], kt[i]], axis=0)) for i in ids]
    a_ab = [jnp.where(m_strict, big[i][:P, :P], 0.0) for i in ids]
    a_ak = [jnp.where(m_strict, big[i][:P, P:], 0.0).astype(BF16) for i in ids]
    a_rb = [jnp.where(m_incl, big[i][P:, :P], 0.0).astype(BF16) for i in ids]
    a_rk = [jnp.where(m_incl, big[i][P:, P:], 0.0).astype(BF16) for i in ids]

    def sib_mask(blk):
        return jnp.logical_and((pr // (2 * blk)) == (pc // (2 * blk)), (pr // blk) != (pc // blk))

    def to_wide(x, w):
        lane_blk = lax.broadcasted_iota(jnp.int32, (w, P), 1) // w
        return sum(jnp.where(lane_blk == k, x[k * w:(k + 1) * w, :], 0.0) for k in range(P // w))

    def to_diag(x, w):
        lane_blk = lax.broadcasted_iota(jnp.int32, (w, P), 1) // w
        return jnp.concatenate([jnp.where(lane_blk == k, x, 0.0) for k in range(P // w)], axis=0)

    pair2 = (pr // 2) == (pc // 2)
    tinv = [jnp.where(eye, 1.0, jnp.where(pair2, a_ab[i], 0.0)) for i in ids]
    blk = 2
    for wide in (L // 2, L):
        tw = [to_wide(tinv[i], wide) for i in ids]
        while 2 * blk <= wide:
            sib = sib_mask(blk)
            tb = [tw[i].astype(BF16) for i in ids]
            half = [_dot(tb[i], jnp.where(sib, a_ab[i], 0.0).astype(BF16)) for i in ids]
            tw = [tw[i] + _dot(half[i].astype(BF16), to_diag(tw[i], wide).astype(BF16)) for i in ids]
            blk *= 2
        tinv = [to_diag(tw[i], wide) for i in ids]

    av = [_dot(a_ak[i], vs[i]) for i in ids]
    pq = [_dot(tinv[i].astype(BF16), jnp.concatenate([at[i], av[i].astype(BF16)], axis=1)).astype(BF16) for i in ids]
    ry = [_dot(a_rb[i], pq[i]) for i in ids]
    rkv = [_dot(a_rk[i], vs[i]) for i in ids]
    mg = [_dot_tn(bh[i], pq[i]) for i in ids]
    kv = [_dot_tn(kh[i], vs[i]) for i in ids]
    r_hat = [(rt[i].astype(F32) + ry[i][:, :LANES]).astype(BF16) for i in ids]
    y0 = [ry[i][:, LANES:] + rkv[i] for i in ids]
    m_corr = [mg[i][:, :LANES].astype(BF16) for i in ids]
    g_add = [mg[i][:, LANES:] + kv[i] for i in ids]
    w_col = [jnp.sum(jnp.where(eye, jnp.broadcast_to(w_end[j][:, cols[p]], (P, P)), 0.0),
                     axis=1, keepdims=True) for j, p in chains]

    def scan_chunks(order):
        y = [None] * n_chunks
        h = {(q, p): h_scr[q, p] for q in range(n_seq) for p in pairs}
        for j in order:
            ys = []
            for p in pairs:
                i = j * len(pairs) + p
                key = (seq_of(j), p)
                hb = h[key].astype(BF16)
                yst = _dot(r_hat[i], hb) + y0[i]
                ys.append(yst[:L, :] + yst[L:, :])
                h[key] = w_col[i] * h[key] + _dot(m_corr[i], hb) + g_add[i]
            y[j] = jnp.concatenate(ys, axis=1)
        for (q, p), val in h.items():
            h_scr[q, p] = val
        return jnp.concatenate(y, axis=0)

    cc = c + d * (nb - 1 - 2 * c)
    rows = pl.ds(pl.multiple_of(cc * LB, LB), LB)

    @pl.when(fwd)
    def _():
        out_ref[rows, :] = scan_chunks(range(n_chunks))

    @pl.when(jnp.logical_not(fwd))
    def _():
        inv_n = 1.0 / R_HEAD
        ysum = out_ref[rows, :] + scan_chunks(reversed(range(n_chunks)))
        mean = head_sum(ysum) * inv_n
        yc = ysum - mean
        var = head_sum(yc * yc) * inv_n
        yn = yc * lax.rsqrt(var + R_LN_EPS) * lng_ref[...] + lnb_ref[...]
        a_f = jax.nn.sigmoid(a0f_ref[...] + _dot(adb, a2f_ref[...]))
        k_bar = k * (1.0 + (0.5 * (a_f + a) - 1.0) * ka)
        bonus = head_sum(r * k_bar * rk_ref[...]) * v
        gate = _dot(jax.nn.sigmoid(gin).astype(BF16), g2_ref[...])
        out_ref[rows, :] = (yn + bonus) * gate

    @pl.when(c == nb - 1)
    def _():
        for q in range(n_seq):
            for p in pairs:
                st = h_scr[q, p].T
                hs_ref[q, 0, 2 * p] = st[:R_HEAD, :R_HEAD]
                hs_ref[q, 0, 2 * p + 1] = st[R_HEAD:, R_HEAD:]


def _rwkv_call(xr, w0, w2p, a0, a2p, g2, k_k, k_a, r_k, ln_g, ln_b, hones, batch, seq_len, h0):
    n = xr.shape[0]
    LB = R_CHUNK * R_CHUNKS_PER_STEP
    n_seq = max(LB // seq_len, 1)
    assert (seq_len % LB == 0 or LB % seq_len == 0) and seq_len % R_CHUNK == 0 and batch % n_seq == 0
    nb = max(seq_len // LB, 1)
    groups = batch // n_seq
    zero_init = h0 is None
    n_pairs = R_HEADS // 2

    def tok(b, d, c):
        return (b * nb + c + d * (nb - 1 - 2 * c), 0)

    by_dir = [w0, w2p, a0, a2p]
    consts = [a0[0], a2p[0], g2, k_k, k_a, r_k, ln_g, ln_b, hones]
    args = [xr] + by_dir + consts
    specs = [pl.BlockSpec((LB, R_COLS), tok)]
    specs += [pl.BlockSpec((1,) + a.shape[1:], lambda b, d, c: (d, 0, 0)) for a in by_dir]
    specs += [_const_spec(a.shape) for a in consts]
    if not zero_init:
        args.append(h0)
        specs.append(pl.BlockSpec((n_seq, 1, n_pairs, LANES, LANES), lambda b, d, c: (b, d, 0, 0, 0)))
    outs = [jax.ShapeDtypeStruct((n, R_WIDTH), F32),
            jax.ShapeDtypeStruct((batch, 2, R_HEADS, R_HEAD, R_HEAD), F32)]
    out_specs = [pl.BlockSpec((n_seq * seq_len, R_WIDTH), lambda b, d, c: (b, 0)),
                 pl.BlockSpec((n_seq, 1, R_HEADS, R_HEAD, R_HEAD), lambda b, d, c: (b, d, 0, 0, 0))]
    body = functools.partial(_rwkv_body, zero_init, nb, n_seq)
    return pl.pallas_call(
        body,
        grid=(groups, 2, nb),
        in_specs=specs,
        out_specs=out_specs,
        out_shape=outs,
        scratch_shapes=[pltpu.VMEM((n_seq, n_pairs, LANES, LANES), F32)],
        compiler_params=_cparams(("parallel", "arbitrary", "arbitrary")),
        name="rwkv_zero" if zero_init else "rwkv_init",
    )(*args)


def _pairs_from_heads(s):
    ht = jnp.swapaxes(s, -1, -2)
    lead = ht.shape[:-3]
    ht = ht.reshape(lead + (R_HEADS // 2, 2, R_HEAD, R_HEAD))
    z = jnp.zeros_like(ht[..., 0, :, :])
    top = jnp.concatenate([ht[..., 0, :, :], z], axis=-1)
    bot = jnp.concatenate([z, ht[..., 1, :, :]], axis=-1)
    return jnp.concatenate([top, bot], axis=-2)


def _trunk(x, mod3, mod_row0, per_seq_rows, grid_mode, init, w):
    batch, seq_len, d = x.shape
    x2 = x.reshape(batch * seq_len, d)

    def rows_for(tile):
        per = seq_len // tile
        if per_seq_rows:
            return lambda i: mod_row0 + i // per
        return lambda i: mod_row0

    x1 = _ffn_call(x2, mod3, rows_for(FFN_TILE), 0, w["norm_ffn1"], w["ffn1_in"], w["ffn1_out"])
    mix_tile = MIX_TILE if grid_mode else seq_len
    q, k, vt, ot, gc, gr, xr = _mixin_call(
        x1, seq_len, grid_mode, mod3, rows_for(mix_tile), w["norm_mix"], w["wqk"], w["wvot"], w["wg"], w["wgt"],
        w["wr"], w["conv_w"], w["conv_b"], w["mu"])
    if init is None:
        m_init = None
        r_init = None
    else:
        c0, n0, m0, s0 = init
        m0p = jnp.broadcast_to(jnp.pad(m0, ((0, 0), (0, 0), (0, 8 - M_HEADS)))[..., None], m0.shape[:2] + (8, LANES))
        m_init = (c0, n0, m0p)
        r_init = _pairs_from_heads(s0)
    mo, cs, ns, ms = _mlstm_call(q, k, vt, ot, gc, gr, w["gate_b_row"], w["gate_b_col"], w["head_g_col"],
                                 batch, seq_len, m_init)
    ro, hs = _rwkv_call(xr, w["r_w0"], w["r_w2p"], w["r_a0"], w["r_a2p"], w["r_g2"], w["r_k_k"], w["r_k_a"],
                        w["r_r_k"], w["r_ln_g"], w["r_ln_b"], w["hones"], batch, seq_len, r_init)
    y = _ffn_call(x1, mod3, rows_for(FFN_TILE), 6, w["norm_ffn2"], w["ffn2_in"], w["ffn2_out"],
                  mix=(mo, ro, w["wo_m"], w["wo_r"]), g_final=w["norm_final"])
    states = (cs, ns, ms[:, :, :M_HEADS, 0], hs)
    return y.reshape(batch, seq_len, d), states


def _prepare_weights(ada_w, ada_b, norm_ffn1, ffn1_w_in, ffn1_w_out, norm_mix, mix_w_in, mix_w_out,
                     m_conv_w, m_conv_b, m_gate_b, m_head_g, r_mu, r_w0, r_w2, r_a0, r_a2, r_g2,
                     r_k_k, r_k_a, r_r_k, r_ln_g, r_ln_b, norm_ffn2, ffn2_w_in, ffn2_w_out, norm_final):
    assert ada_w.shape[0] == 1, "single trunk layer"
    lora = r_w2.shape[2]
    nm = 4 * M_WIDTH

    w_in = mix_w_in[0]
    gate_w = w_in[:, nm:nm + GATE_COLS]
    zpad = jnp.zeros((lora, R_WIDTH), F32)

    def dir_pad(w2):
        return jnp.stack([jnp.concatenate([w2[0], zpad], axis=0), jnp.concatenate([zpad, w2[1]], axis=0)])

    head_id = jnp.arange(LANES) // R_HEAD
    return dict(
        norm_ffn1=norm_ffn1, norm_mix=norm_mix, norm_ffn2=norm_ffn2, norm_final=norm_final[None],
        ffn1_in=ffn1_w_in[0].astype(BF16), ffn1_out=ffn1_w_out[0].astype(BF16),
        ffn2_in=ffn2_w_in[0].astype(BF16), ffn2_out=ffn2_w_out[0].astype(BF16),
        wqk=w_in[:, :2 * M_WIDTH].astype(BF16), wvot=w_in[:, 2 * M_WIDTH:nm].T.astype(BF16),
        wg=jnp.pad(gate_w, ((0, 0), (0, LANES - GATE_COLS))).astype(BF16),
        wgt=gate_w.T.astype(BF16),
        wr=w_in[:, nm + GATE_COLS:].astype(BF16),
        wo_m=mix_w_out[0, :M_WIDTH].astype(BF16), wo_r=mix_w_out[0, M_WIDTH:].astype(BF16),
        conv_w=m_conv_w[0], conv_b=m_conv_b, mu=r_mu[0],
        gate_b_row=jnp.pad(m_gate_b[0].reshape(1, GATE_COLS), ((0, 0), (0, LANES - GATE_COLS))),
        gate_b_col=m_gate_b[0].reshape(GATE_COLS, 1),
        head_g_col=m_head_g.reshape(M_WIDTH, 1),
        r_w0=r_w0[0][:, None, :], r_w2p=dir_pad(r_w2[0]).astype(BF16),
        r_a0=r_a0[0][:, None, :], r_a2p=dir_pad(r_a2[0]).astype(BF16),
        r_g2=r_g2[0].astype(BF16), r_k_k=r_k_k, r_k_a=r_k_a, r_r_k=r_r_k[0].reshape(1, R_WIDTH),
        r_ln_g=r_ln_g, r_ln_b=r_ln_b,
        hones=(head_id[:, None] == head_id[None, :]).astype(BF16),
    )


def kernel(x_prompt, x_sample, c, state_mlstm_C, state_mlstm_n, state_mlstm_m, state_rwkv_S, c_ctx,
           ada_w, ada_b, norm_ffn1, ffn1_w_in, ffn1_w_out, norm_mix, mix_w_in, mix_w_out,
           m_conv_w, m_conv_b, m_gate_b, m_head_g, r_mu, r_w0, r_w2, r_a0, r_a2, r_g2,
           r_k_k, r_k_a, r_r_k, r_ln_g, r_ln_b, norm_ffn2, ffn2_w_in, ffn2_w_out, norm_final):
    w = _prepare_weights(ada_w, ada_b, norm_ffn1, ffn1_w_in, ffn1_w_out, norm_mix, mix_w_in, mix_w_out,
                         m_conv_w, m_conv_b, m_gate_b, m_head_g, r_mu, r_w0, r_w2, r_a0, r_a2, r_g2,
                         r_k_k, r_k_a, r_r_k, r_ln_g, r_ln_b, norm_ffn2, ffn2_w_in, ffn2_w_out, norm_final)
    d = x_prompt.shape[-1]
    dec_batch = x_sample.shape[0]
    cond = jnp.concatenate([c_ctx[None], c, jnp.zeros((16 - 1 - dec_batch, d), F32)], axis=0)
    mod3 = _ada_call(cond, ada_w[0], ada_b).reshape(16, 9, d)

    y_prompt, (cs, ns, ms, ss) = _trunk(x_prompt, mod3, 0, False, False, None, w)
    init = (state_mlstm_C[:, 0], state_mlstm_n[:, 0], state_mlstm_m[:, 0], state_rwkv_S[:, 0])
    y_sample, _ = _trunk(x_sample, mod3, 1, True, True, init, w)
    return (y_prompt, y_sample, cs[:, None], ns[:, None], ms[:, None], ss[:, None])
```

```python
import functools
import math

import jax
import jax.numpy as jnp
from jax import lax
from jax.experimental import pallas as pl
from jax.experimental.pallas import tpu as pltpu

F32 = jnp.float32
BF16 = jnp.bfloat16

EPS = 1e-6
R_LN_EPS = 64e-5
GRID_W = 64
M_HEADS = 4
M_DK = 128
M_DV = 128
M_WIDTH = M_HEADS * M_DV
R_HEADS = 8
R_HEAD = 64
R_WIDTH = R_HEADS * R_HEAD
R_LORA = 128
R_COLS = 3 * R_WIDTH + 3 * R_LORA
GATE_COLS = 16
LANES = 128

FFN_TILE = 1024
FFN_CHUNK = 256
MIX_TILE = 512
M_CHUNK = 256
R_CHUNK = 64
R_CHUNKS_PER_STEP = 8
VMEM_LIMIT = 56 * 1024 * 1024


def _dot(a, b):
    return jnp.dot(a, b, preferred_element_type=F32)


def _dot_nt(a, b):
    return lax.dot_general(a, b, (((1,), (1,)), ((), ())), preferred_element_type=F32)


def _dot_tn(a, b):
    return lax.dot_general(a, b, (((0,), (0,)), ((), ())), preferred_element_type=F32)


def _split(x, parts):
    pieces = []
    for _ in range(parts - 1):
        p = x.astype(BF16)
        pieces.append(p)
        x = x - p.astype(F32)
    pieces.append(x.astype(BF16))
    return pieces


def _sel_dot(mat, x, parts=3):
    return sum(_dot(mat, p) for p in _split(x, parts))


def _dot_sel(x, mat, parts=3):
    return sum(_dot(p, mat) for p in _split(x, parts))


def _silu(x):
    return x * jax.nn.sigmoid(x)


def _rms_mod(x, g, shift, scale):
    y = x * lax.rsqrt(jnp.mean(x * x, axis=-1, keepdims=True) + EPS) * g
    return y * (1.0 + scale) + shift


def _cparams(sem):
    return pltpu.CompilerParams(dimension_semantics=sem, vmem_limit_bytes=VMEM_LIMIT)


def _const_spec(shape):
    nd = len(shape)
    return pl.BlockSpec(shape, lambda *_: (0,) * nd, pipeline_mode=pl.Buffered(1))


def _ada_body(c_ref, w_ref, b_ref, o_ref):
    s = _silu(c_ref[...])
    o_ref[...] = _dot(s.astype(BF16), w_ref[...].astype(BF16)) + b_ref[...]


def _ada_call(cond, w, b):
    rows, d = cond.shape
    n = w.shape[1]
    tn = d
    return pl.pallas_call(
        _ada_body,
        grid=(n // tn,),
        in_specs=[
            pl.BlockSpec((rows, d), lambda j: (0, 0)),
            pl.BlockSpec((d, tn), lambda j: (0, j)),
            pl.BlockSpec((1, tn), lambda j: (0, j)),
        ],
        out_specs=pl.BlockSpec((rows, tn), lambda j: (0, j)),
        out_shape=jax.ShapeDtypeStruct((rows, n), F32),
        compiler_params=_cparams(("arbitrary",)),
        name="ada",
    )(cond, w, b)


def _ffn_body(mod_base, with_mix, with_final, *refs):
    refs = list(refs)
    x_ref = refs.pop(0)
    if with_mix:
        mo_ref, ro_ref, wmm_ref, wmr_ref = refs[:4]
        refs = refs[4:]
    mod_ref, g_ref, win_ref, wout_ref = refs[:4]
    refs = refs[4:]
    if with_final:
        gfin_ref = refs.pop(0)
    out_ref, a_scr = refs

    x = x_ref[...]
    if with_mix:
        mix = _dot(mo_ref[...].astype(BF16), wmm_ref[...]) + _dot(ro_ref[...].astype(BF16), wmr_ref[...])
        x = x + mod_ref[0, mod_base - 1:mod_base, :] * mix
    shift = mod_ref[0, mod_base:mod_base + 1, :]
    scale = mod_ref[0, mod_base + 1:mod_base + 2, :]
    gate = mod_ref[0, mod_base + 2:mod_base + 3, :]
    hb = _rms_mod(x, g_ref[...], shift, scale).astype(BF16)
    dff = wout_ref.shape[0]
    for j in range(dff // FFN_CHUNK):
        lo = j * FFN_CHUNK
        gt = _dot(hb, win_ref[:, lo:lo + FFN_CHUNK])
        up = _dot(hb, win_ref[:, dff + lo:dff + lo + FFN_CHUNK])
        a_scr[:, lo:lo + FFN_CHUNK] = (_silu(gt) * up).astype(BF16)
    y = x + (0.5 * gate) * _dot(a_scr[...], wout_ref[...])
    if with_final:
        y = y * lax.rsqrt(jnp.mean(y * y, axis=-1, keepdims=True) + EPS) * gfin_ref[...]
    out_ref[...] = y


def _ffn_call(x, mod3, row_of_tile, mod_base, g, w_in, w_out, mix=None, g_final=None):
    n, d = x.shape
    tm = FFN_TILE
    dff = w_out.shape[0]
    assert n % tm == 0 and dff % FFN_CHUNK == 0, (n, dff)
    tok = lambda i: (i, 0)
    args = [x]
    specs = [pl.BlockSpec((tm, d), tok)]
    if mix is not None:
        mo, ro, wmm, wmr = mix
        args += [mo, ro, wmm, wmr]
        specs += [pl.BlockSpec((tm, mo.shape[1]), tok), pl.BlockSpec((tm, ro.shape[1]), tok),
                  _const_spec(wmm.shape), _const_spec(wmr.shape)]
    args += [mod3, g, w_in, w_out]
    specs += [pl.BlockSpec((1,) + mod3.shape[1:], lambda i: (row_of_tile(i), 0, 0)),
              _const_spec(g.shape), _const_spec(w_in.shape), _const_spec(w_out.shape)]
    if g_final is not None:
        args.append(g_final)
        specs.append(_const_spec(g_final.shape))
    body = functools.partial(_ffn_body, mod_base, mix is not None, g_final is not None)
    return pl.pallas_call(
        body,
        grid=(n // tm,),
        in_specs=specs,
        out_specs=pl.BlockSpec((tm, d), tok),
        out_shape=jax.ShapeDtypeStruct((n, d), F32),
        scratch_shapes=[pltpu.VMEM((tm, dff), BF16)],
        compiler_params=_cparams(("parallel",)),
        name="ffn_mix" if mix is not None else "ffn",
    )(*args)


def _mixin_body(grid_mode, has_halo, tiles_per_seq, *refs):
    refs = list(refs)
    x_ref = refs.pop(0)
    if has_halo:
        xp_ref, xn_ref = refs[:2]
        refs = refs[2:]
    (mod_ref, g_ref, wqk_ref, wvot_ref, wg_ref, wgt_ref, wr_ref, cw_ref, cb_ref, mu_ref,
     q_ref, k_ref, vt_ref, ot_ref, gc_ref, gr_ref, xr_ref) = refs

    tm = x_ref.shape[0]
    shift = mod_ref[0, 3:4, :]
    scale = mod_ref[0, 4:5, :]
    g = g_ref[...]
    hb = _rms_mod(x_ref[...], g, shift, scale).astype(BF16)
    row = lax.broadcasted_iota(jnp.int32, (tm, 1), 0)
    nqk = 2 * M_WIDTH

    if has_halo:
        i = pl.program_id(0)
        pos = i % tiles_per_seq
        keep_p = (pos > 0).astype(F32)
        keep_n = (pos < tiles_per_seq - 1).astype(F32)
        hp32 = _rms_mod(xp_ref[...], g, shift, scale)
        hn32 = _rms_mod(xn_ref[...], g, shift, scale)
        hp = hp32.astype(BF16)
        hn = hn32.astype(BF16)
        nh = hp32.shape[0]
        edge = 16
        qk_prev = _dot(hp32[nh - edge:, :].astype(BF16), wqk_ref[...])[edge - 1:edge, :] * keep_p
        qk_next = _dot(hn32[:edge, :].astype(BF16), wqk_ref[...])[0:1, :] * keep_n
    else:
        qk_prev = jnp.zeros((1, nqk), F32)
        qk_next = jnp.zeros((1, nqk), F32)

    uqk = _dot(hb, wqk_ref[...])
    ur = _dot(hb, wr_ref[...])
    u_dn = jnp.where(row == 0, qk_prev, pltpu.roll(uqk, 1, 0))
    u_up = jnp.where(row == tm - 1, qk_next, pltpu.roll(uqk, tm - 1, 0))
    qk = _silu(cw_ref[0:1, :] * u_dn + cw_ref[1:2, :] * uqk + cw_ref[2:3, :] * u_up + cb_ref[...])
    q_ref[...] = qk[:, :M_WIDTH].astype(BF16)
    k_ref[...] = (qk[:, M_WIDTH:] * (M_DK ** -0.5)).astype(BF16)
    vo_t = _dot_nt(wvot_ref[...], hb)
    gc_ref[...] = _dot(hb, wg_ref[...])
    gr_ref[...] = _dot_nt(wgt_ref[...], hb)

    if grid_mode:
        col = row % GRID_W
        left = jnp.where(col == 0, 0.0, pltpu.roll(ur, 1, 0))
        right = jnp.where(col == GRID_W - 1, 0.0, pltpu.roll(ur, tm - 1, 0))
        ur_p = _dot(hp, wr_ref[...]) * keep_p
        ur_n = _dot(hn, wr_ref[...]) * keep_n
        up = jnp.concatenate([ur_p, ur[:tm - GRID_W, :]], axis=0)
        down = jnp.concatenate([ur[GRID_W:, :], ur_n], axis=0)
        mu_self = 1.0 - (mu_ref[0:1, :] + mu_ref[1:2, :] + mu_ref[2:3, :] + mu_ref[3:4, :])
        xr = (mu_self * ur + mu_ref[0:1, :] * left + mu_ref[1:2, :] * right
              + mu_ref[2:3, :] * up + mu_ref[3:4, :] * down)
    else:
        left = jnp.where(row == 0, 0.0, pltpu.roll(ur, 1, 0))
        right = jnp.where(row == tm - 1, 0.0, pltpu.roll(ur, tm - 1, 0))
        mu_self = 1.0 - (mu_ref[0:1, :] + mu_ref[1:2, :])
        xr = mu_self * ur + mu_ref[0:1, :] * left + mu_ref[1:2, :] * right
    xr_ref[...] = xr
    vt_ref[...] = vo_t[:M_WIDTH, :].astype(BF16)
    ot_ref[...] = vo_t[M_WIDTH:, :]


def _mixin_call(x, seq_len, grid_mode, mod3, row_of_tile, g, wqk, wvot, wg, wgt, wr, cw, cb, mu):
    n, d = x.shape
    if grid_mode:
        tm = MIX_TILE
        has_halo = True
    else:
        tm = seq_len
        has_halo = False
    assert seq_len % tm == 0 and tm % GRID_W == 0, (seq_len, tm)
    tiles_per_seq = seq_len // tm
    tok = lambda i: (i, 0)
    args = [x]
    specs = [pl.BlockSpec((tm, d), tok)]
    if has_halo:
        hb = R_CHUNK
        per = tm // hb
        last = n // hb - 1
        args += [x, x]
        specs += [pl.BlockSpec((hb, d), lambda i: (jnp.maximum(i * per - 1, 0), 0)),
                  pl.BlockSpec((hb, d), lambda i: (jnp.minimum((i + 1) * per, last), 0))]
    args += [mod3, g, wqk, wvot, wg, wgt, wr, cw, cb, mu]
    specs += [pl.BlockSpec((1,) + mod3.shape[1:], lambda i: (row_of_tile(i), 0, 0))]
    specs += [_const_spec(a.shape) for a in (g, wqk, wvot, wg, wgt, wr, cw, cb, mu)]
    outs = [jax.ShapeDtypeStruct((n, M_WIDTH), BF16)] * 2 + [
        jax.ShapeDtypeStruct((n // seq_len, M_WIDTH, seq_len), BF16),
        jax.ShapeDtypeStruct((n // seq_len, M_WIDTH, seq_len), F32),
        jax.ShapeDtypeStruct((n, LANES), F32),
        jax.ShapeDtypeStruct((n // seq_len, GATE_COLS, seq_len), F32),
        jax.ShapeDtypeStruct((n, R_COLS), F32),
    ]
    tok_t = lambda i: (i // tiles_per_seq, 0, i % tiles_per_seq)
    out_specs = [pl.BlockSpec((tm, M_WIDTH), tok)] * 2 + [pl.BlockSpec((None, M_WIDTH, tm), tok_t)] * 2 + [
        pl.BlockSpec((tm, LANES), tok),
        pl.BlockSpec((None, GATE_COLS, tm), tok_t),
        pl.BlockSpec((tm, R_COLS), tok),
    ]
    body = functools.partial(_mixin_body, grid_mode, has_halo, tiles_per_seq)
    return pl.pallas_call(
        body,
        grid=(n // tm,),
        in_specs=specs,
        out_specs=out_specs,
        out_shape=outs,
        compiler_params=_cparams(("parallel",)),
        name="mix_in_grid" if grid_mode else "mix_in_seq",
    )(*args)


def _mlstm_body(zero_init, nc, n_seq, *refs):
    refs = list(refs)
    q_ref, k_ref, vt_ref, ot_ref, gc_ref, gr_ref, gbr_ref, gbc_ref, hg_ref = refs[:9]
    refs = refs[9:]
    if not zero_init:
        c0_ref, n0_ref, m0_ref = refs[:3]
        refs = refs[3:]
    out_ref, cs_ref, ns_ref, ms_ref, ct_scr, n_scr, m_scr, acc_scr = refs

    d = pl.program_id(1)
    c = pl.program_id(2)
    L = q_ref.shape[1]
    fwd = d == 0
    heads = range(M_HEADS)
    hsl = [slice(h * M_DV, (h + 1) * M_DV) for h in heads]
    seqs = range(n_seq)

    @pl.when(c == 0)
    def _():
        if zero_init:
            ct_scr[...] = jnp.zeros_like(ct_scr)
            n_scr[...] = jnp.zeros_like(n_scr)
            m_scr[...] = jnp.zeros_like(m_scr)
        else:
            for q in seqs:
                for h in heads:
                    ct_scr[q, h] = c0_ref[q, 0, h].T
                n_scr[q, 0:M_HEADS, :] = n0_ref[q, 0]
                m_scr[q] = m0_ref[q, 0]

    ri = lax.broadcasted_iota(jnp.int32, (L, L), 0)
    ci = lax.broadcasted_iota(jnp.int32, (L, L), 1)
    sgn = 1 - 2 * d
    tri = ((ci - ri) * sgn <= 0).astype(BF16)
    mask_st = (ri - ci) * sgn <= 0
    tri_t = mask_st.astype(BF16)

    gcol = [gc_ref[q] + gbr_ref[...] for q in seqs]
    grow = [gr_ref[q] + gbc_ref[...] for q in seqs]
    bcol = [_sel_dot(tri, jax.nn.log_sigmoid(x)) for x in gcol]
    brow = [_dot_sel(jax.nn.log_sigmoid(x), tri_t) for x in grow]

    def pick_col(a, j0, j1):
        return jnp.where(fwd, a[:, j0:j0 + 1], a[:, j1:j1 + 1])

    def pick_row(a, j0, j1):
        return jnp.where(fwd, a[j0:j0 + 1, :], a[j1:j1 + 1, :])

    cc = c + d * (nc - 1 - 2 * c)
    row0 = lax.broadcasted_iota(jnp.int32, (16, LANES), 0) == 0
    row0_l = lax.broadcasted_iota(jnp.int32, (16, L), 0) == 0

    chains = [(q, h) for q in seqs for h in heads]
    ids = range(len(chains))
    qb = [q_ref[q, :, hsl[h]] for q, h in chains]
    kb = [k_ref[q, :, hsl[h]] for q, h in chains]
    vt = [vt_ref[q, hsl[h], :] for q, h in chains]
    ra_c = [pick_col(gcol[q], h, 4 + h) - pick_col(bcol[q], 8 + h, 12 + h) for q, h in chains]
    i_r = [pick_row(grow[q], h, 4 + h) for q, h in chains]
    b_r = [pick_row(brow[q], 8 + h, 12 + h) for q, h in chains]
    b_tot = [jnp.where(fwd, b_r[i][:, L - 1:L], b_r[i][:, 0:1]) for i in ids]
    m_prev = [m_scr[q, h:h + 1, 0:1] for q, h in chains]
    n_prev = [n_scr[q, h:h + 1, :] for q, h in chains]
    ct_prev = [ct_scr[q, h] for q, h in chains]

    kq = [_dot_nt(kb[i], qb[i]) for i in ids]
    qct = [_dot_nt(ct_prev[i].astype(BF16), qb[i]) for i in ids]
    qn = [_dot_nt(jnp.where(row0, n_prev[i], 0.0).astype(BF16), qb[i])[0:1, :] for i in ids]
    dmt = [jnp.where(mask_st, ra_c[i], -jnp.inf) for i in ids]
    mx = [jnp.maximum(m_prev[i], jnp.max(dmt[i], axis=0, keepdims=True)) for i in ids]
    st = [kq[i] * jnp.exp(dmt[i] - mx[i]) for i in ids]
    sc = [jnp.exp(m_prev[i] - mx[i]) for i in ids]
    numt = [_dot(vt[i], st[i].astype(BF16)) + sc[i] * qct[i] for i in ids]
    den = [jnp.sum(st[i], axis=0, keepdims=True) + sc[i] * qn[i] for i in ids]
    ht = [numt[i] / jnp.maximum(jnp.abs(den[i]), jnp.exp(-(b_r[i] + mx[i]))) for i in ids]

    gs = [b_tot[i] - b_r[i] + i_r[i] for i in ids]
    m_new = [jnp.maximum(b_tot[i] + m_prev[i], jnp.max(gs[i], axis=1, keepdims=True)) for i in ids]
    wk = [jnp.exp(gs[i] - m_new[i]) for i in ids]
    decay = [jnp.exp(b_tot[i] + m_prev[i] - m_new[i]) for i in ids]
    ckv = [_dot((vt[i].astype(F32) * wk[i]).astype(BF16), kb[i]) for i in ids]
    nk = [_dot(jnp.where(row0_l, wk[i], 0.0).astype(BF16), kb[i])[0:1, :] for i in ids]
    for i, (q, h) in enumerate(chains):
        ct_scr[q, h] = decay[i] * ct_prev[i] + ckv[i]
        n_scr[q, h:h + 1, :] = decay[i] * n_prev[i] + nk[i]
        m_scr[q, h:h + 1, :] = jnp.broadcast_to(m_new[i], (1, LANES))

    @pl.when(fwd)
    def _():
        for q in seqs:
            acc_scr[cc, q] = jnp.concatenate(ht[q * M_HEADS:(q + 1) * M_HEADS], axis=0)

    @pl.when(jnp.logical_not(fwd))
    def _():
        for q in seqs:
            row = []
            for h in heads:
                i = q * M_HEADS + h
                t = jax.nn.sigmoid(ot_ref[q, hsl[h], :]) * (acc_scr[cc, q, hsl[h], :] + ht[i])
                t = t * lax.rsqrt(jnp.mean(t * t, axis=0, keepdims=True) + EPS) * hg_ref[hsl[h], :]
                row.append(t.T)
            out_ref[q] = jnp.concatenate(row, axis=1).astype(BF16)

    @pl.when(c == nc - 1)
    def _():
        for q in seqs:
            for h in heads:
                cs_ref[q, 0, h] = ct_scr[q, h].T
            ns_ref[q, 0] = n_scr[q, 0:M_HEADS, :]
            ms_ref[q, 0] = m_scr[q]


def _mlstm_call(q, k, vt, ot, gc, gr, gate_b_row, gate_b_col, head_g_col, batch, seq_len, init):
    n = q.shape[0]
    L = min(M_CHUNK, seq_len)
    assert seq_len % L == 0, (seq_len, L)
    nc = seq_len // L
    zero_init = init is None
    n_seq = 2 if batch % 2 == 0 else 1
    groups = batch // n_seq
    per_seq = lambda a: a.reshape(batch, seq_len, a.shape[-1])

    def chunk(d, c):
        return c + d * (nc - 1 - 2 * c)

    tok = lambda b, d, c: (b, chunk(d, c), 0)
    tok_t = lambda b, d, c: (b, 0, chunk(d, c))
    tok_out = lambda b, d, c: (b, nc - 1 - c * d, 0)

    args = [per_seq(q), per_seq(k), vt, ot, per_seq(gc), gr, gate_b_row, gate_b_col, head_g_col]
    specs = [pl.BlockSpec((n_seq, L, M_WIDTH), tok)] * 2 + [pl.BlockSpec((n_seq, M_WIDTH, L), tok_t)] * 2 + [
        pl.BlockSpec((n_seq, L, LANES), tok),
        pl.BlockSpec((n_seq, GATE_COLS, L), tok_t),
        _const_spec(gate_b_row.shape), _const_spec(gate_b_col.shape), _const_spec(head_g_col.shape),
    ]
    if not zero_init:
        c0, n0, m0 = init
        args += [c0, n0, m0]
        specs += [pl.BlockSpec((n_seq, 1) + c0.shape[2:], lambda b, d, c: (b, d, 0, 0, 0)),
                  pl.BlockSpec((n_seq, 1) + n0.shape[2:], lambda b, d, c: (b, d, 0, 0)),
                  pl.BlockSpec((n_seq, 1) + m0.shape[2:], lambda b, d, c: (b, d, 0, 0))]
    outs = [jax.ShapeDtypeStruct((batch, seq_len, M_WIDTH), BF16),
            jax.ShapeDtypeStruct((batch, 2, M_HEADS, M_DK, M_DV), F32),
            jax.ShapeDtypeStruct((batch, 2, M_HEADS, M_DK), F32),
            jax.ShapeDtypeStruct((batch, 2, 8, LANES), F32)]
    out_specs = [pl.BlockSpec((n_seq, L, M_WIDTH), tok_out),
                 pl.BlockSpec((n_seq, 1, M_HEADS, M_DK, M_DV), lambda b, d, c: (b, d, 0, 0, 0)),
                 pl.BlockSpec((n_seq, 1, M_HEADS, M_DK), lambda b, d, c: (b, d, 0, 0)),
                 pl.BlockSpec((n_seq, 1, 8, LANES), lambda b, d, c: (b, d, 0, 0))]
    body = functools.partial(_mlstm_body, zero_init, nc, n_seq)
    res = pl.pallas_call(
        body,
        grid=(groups, 2, nc),
        in_specs=specs,
        out_specs=out_specs,
        out_shape=outs,
        scratch_shapes=[pltpu.VMEM((n_seq, M_HEADS, M_DV, M_DK), F32),
                        pltpu.VMEM((n_seq, 8, LANES), F32),
                        pltpu.VMEM((n_seq, 8, LANES), F32),
                        pltpu.VMEM((nc, n_seq, M_WIDTH, L), F32)],
        compiler_params=_cparams(("parallel", "arbitrary", "arbitrary")),
        name="mlstm_zero" if zero_init else "mlstm_init",
    )(*args)
    return (res[0].reshape(n, M_WIDTH),) + tuple(res[1:])


def _rwkv_body(zero_init, nb, n_seq, *refs):
    refs = list(refs)
    (xr_ref, w0_ref, w2_ref, a0_ref, a2_ref, a0f_ref, a2f_ref, g2_ref, kk_ref, ka_ref, rk_ref,
     lng_ref, lnb_ref, hones_ref) = refs[:14]
    refs = refs[14:]
    if not zero_init:
        h0_ref = refs.pop(0)
    out_ref, hs_ref, h_scr = refs

    d = pl.program_id(1)
    c = pl.program_id(2)
    LB = xr_ref.shape[0]
    L = R_CHUNK
    n_chunks = LB // L
    P = 2 * L
    fwd = d == 0
    sgn = 1 - 2 * d
    pairs = range(R_HEADS // 2)
    cols = [slice(p * LANES, (p + 1) * LANES) for p in pairs]

    per_seq = n_chunks // n_seq

    def seq_of(j):
        return j // per_seq

    @pl.when(c == 0)
    def _():
        if zero_init:
            h_scr[...] = jnp.zeros_like(h_scr)
        else:
            h_scr[...] = h0_ref[:, 0]

    r = xr_ref[:, 0:R_WIDTH]
    k = xr_ref[:, R_WIDTH:2 * R_WIDTH]
    v = xr_ref[:, 2 * R_WIDTH:3 * R_WIDTH]
    wd = xr_ref[:, 3 * R_WIDTH:3 * R_WIDTH + R_LORA]
    ad = xr_ref[:, 3 * R_WIDTH + R_LORA:3 * R_WIDTH + 2 * R_LORA]
    gin = xr_ref[:, 3 * R_WIDTH + 2 * R_LORA:]
    hones = hones_ref[...]

    def head_sum(x):
        return jnp.concatenate([_dot_sel(x[:, cs], hones, 2) for cs in cols], axis=1)

    tw = jnp.tanh(wd).astype(BF16)
    adb = ad.astype(BF16)
    ww = w0_ref[0] + _dot(tw, w2_ref[0])
    lw = -math.exp(-0.5) * jax.nn.sigmoid(ww)
    a = jax.nn.sigmoid(a0_ref[0] + _dot(adb, a2_ref[0]))
    ka = ka_ref[...]
    kd = k * (1.0 + (a - 1.0) * ka)
    kk = k * kk_ref[...]
    kk = kk * lax.rsqrt(jnp.maximum(head_sum(kk * kk), 1e-24))
    bvec = kk * a

    ri = lax.broadcasted_iota(jnp.int32, (L, L), 0)
    ci = lax.broadcasted_iota(jnp.int32, (L, L), 1)
    tri = ((ci - ri) * sgn <= 0).astype(BF16)
    cl_j = [_sel_dot(tri, lw[j * L:(j + 1) * L, :], 2) for j in range(n_chunks)]
    tot_j = [jnp.where(fwd, x[L - 1:L, :], x[0:1, :]) for x in cl_j]
    cl = jnp.concatenate(cl_j, axis=0)
    w_end = [jnp.exp(x) for x in tot_j]
    e_in = jnp.exp(cl)
    e_out = jnp.exp(-cl)
    e_end = jnp.concatenate([e_out[j * L:(j + 1) * L, :] * w_end[j] for j in range(n_chunks)], axis=0)
    a_til = -kk * jnp.exp(cl - lw)
    r_til = r * e_in
    b_til = bvec * e_out
    k_til = kd * e_out
    b_hat = bvec * e_end
    k_hat = kd * e_end

    pr = lax.broadcasted_iota(jnp.int32, (P, P), 0)
    pc = lax.broadcasted_iota(jnp.int32, (P, P), 1)
    same = (pr // L) == (pc // L)
    before = (pc - pr) * sgn < 0
    m_strict = jnp.logical_and(same, before)
    m_incl = jnp.logical_and(same, jnp.logical_or(before, pc == pr))
    eye = pr == pc
    lane = lax.broadcasted_iota(jnp.int32, (L, LANES), 1)
    first = lane < R_HEAD

    def stack(x, j, p):
        x = x[j * L:(j + 1) * L, cols[p]].astype(BF16)
        zero = jnp.zeros_like(x)
        return jnp.concatenate([jnp.where(first, x, zero), jnp.where(first, zero, x)], axis=0)

    def twice(x, j, p):
        x = x[j * L:(j + 1) * L, cols[p]].astype(BF16)
        return jnp.concatenate([x, x], axis=0)

    chains = [(j, p) for j in range(n_chunks) for p in pairs]
    at = [stack(a_til, j, p) for j, p in chains]
    rt = [stack(r_til, j, p) for j, p in chains]
    bt = [twice(b_til, j, p) for j, p in chains]
    kt = [twice(k_til, j, p) for j, p in chains]
    bh = [stack(b_hat, j, p) for j, p in chains]
    kh = [stack(k_hat, j, p) for j, p in chains]
    vs = [stack(v, j, p) for j, p in chains]
    ids = range(len(chains))

    big = [_dot_nt(jnp.concatenate([at[i], rt[i]], axis=0), jnp.concatenate([bt[i], kt[i]], axis=0)) for i in ids]
    a_ab = [jnp.where(m_strict, big[i][:P, :P], 0.0) for i in ids]
    a_ak = [jnp.where(m_strict, big[i][:P, P:], 0.0).astype(BF16) for i in ids]
    a_rb = [jnp.where(m_incl, big[i][P:, :P], 0.0).astype(BF16) for i in ids]
    a_rk = [jnp.where(m_incl, big[i][P:, P:], 0.0).astype(BF16) for i in ids]

    def sib_mask(blk):
        return jnp.logical_and((pr // (2 * blk)) == (pc // (2 * blk)), (pr // blk) != (pc // blk))

    def to_wide(x, w):
        lane_blk = lax.broadcasted_iota(jnp.int32, (w, P), 1) // w
        return sum(jnp.where(lane_blk == k, x[k * w:(k + 1) * w, :], 0.0) for k in range(P // w))

    def to_diag(x, w):
        lane_blk = lax.broadcasted_iota(jnp.int32, (w, P), 1) // w
        return jnp.concatenate([jnp.where(lane_blk == k, x, 0.0) for k in range(P // w)], axis=0)

    pair2 = (pr // 2) == (pc // 2)
    tinv = [jnp.where(eye, 1.0, jnp.where(pair2, a_ab[i], 0.0)) for i in ids]
    blk = 2
    for wide in (L // 2, L):
        tw = [to_wide(tinv[i], wide) for i in ids]
        while 2 * blk <= wide:
            sib = sib_mask(blk)
            tb = [tw[i].astype(BF16) for i in ids]
            half = [_dot(tb[i], jnp.where(sib, a_ab[i], 0.0).astype(BF16)) for i in ids]
            tw = [tw[i] + _dot(half[i].astype(BF16), to_diag(tw[i], wide).astype(BF16)) for i in ids]
            blk *= 2
        tinv = [to_diag(tw[i], wide) for i in ids]

    av = [_dot(a_ak[i], vs[i]) for i in ids]
    pq = [_dot(tinv[i].astype(BF16), jnp.concatenate([at[i], av[i].astype(BF16)], axis=1)).astype(BF16) for i in ids]
    ry = [_dot(a_rb[i], pq[i]) for i in ids]
    rkv = [_dot(a_rk[i], vs[i]) for i in ids]
    mg = [_dot_tn(bh[i], pq[i]) for i in ids]
    kv = [_dot_tn(kh[i], vs[i]) for i in ids]
    r_hat = [(rt[i].astype(F32) + ry[i][:, :LANES]).astype(BF16) for i in ids]
    y0 = [ry[i][:, LANES:] + rkv[i] for i in ids]
    m_corr = [mg[i][:, :LANES].astype(BF16) for i in ids]
    g_add = [mg[i][:, LANES:] + kv[i] for i in ids]
    w_col = [jnp.sum(jnp.where(eye, jnp.broadcast_to(w_end[j][:, cols[p]], (P, P)), 0.0),
                     axis=1, keepdims=True) for j, p in chains]

    def scan_chunks(order):
        y = [None] * n_chunks
        h = {(q, p): h_scr[q, p] for q in range(n_seq) for p in pairs}
        for j in order:
            ys = []
            for p in pairs:
                i = j * len(pairs) + p
                key = (seq_of(j), p)
                hb = h[key].astype(BF16)
                yst = _dot(r_hat[i], hb) + y0[i]
                ys.append(yst[:L, :] + yst[L:, :])
                h[key] = w_col[i] * h[key] + _dot(m_corr[i], hb) + g_add[i]
            y[j] = jnp.concatenate(ys, axis=1)
        for (q, p), val in h.items():
            h_scr[q, p] = val
        return jnp.concatenate(y, axis=0)

    cc = c + d * (nb - 1 - 2 * c)
    rows = pl.ds(pl.multiple_of(cc * LB, LB), LB)

    @pl.when(fwd)
    def _():
        out_ref[rows, :] = scan_chunks(range(n_chunks))

    @pl.when(jnp.logical_not(fwd))
    def _():
        inv_n = 1.0 / R_HEAD
        ysum = out_ref[rows, :] + scan_chunks(reversed(range(n_chunks)))
        mean = head_sum(ysum) * inv_n
        yc = ysum - mean
        var = head_sum(yc * yc) * inv_n
        yn = yc * lax.rsqrt(var + R_LN_EPS) * lng_ref[...] + lnb_ref[...]
        a_f = jax.nn.sigmoid(a0f_ref[...] + _dot(adb, a2f_ref[...]))
        k_bar = k * (1.0 + (0.5 * (a_f + a) - 1.0) * ka)
        bonus = head_sum(r * k_bar * rk_ref[...]) * v
        gate = _dot(jax.nn.sigmoid(gin).astype(BF16), g2_ref[...])
        out_ref[rows, :] = (yn + bonus) * gate

    @pl.when(c == nb - 1)
    def _():
        for q in range(n_seq):
            for p in pairs:
                st = h_scr[q, p].T
                hs_ref[q, 0, 2 * p] = st[:R_HEAD, :R_HEAD]
                hs_ref[q, 0, 2 * p + 1] = st[R_HEAD:, R_HEAD:]


def _rwkv_call(xr, w0, w2p, a0, a2p, g2, k_k, k_a, r_k, ln_g, ln_b, hones, batch, seq_len, h0):
    n = xr.shape[0]
    LB = R_CHUNK * R_CHUNKS_PER_STEP
    n_seq = max(LB // seq_len, 1)
    assert (seq_len % LB == 0 or LB % seq_len == 0) and seq_len % R_CHUNK == 0 and batch % n_seq == 0
    nb = max(seq_len // LB, 1)
    groups = batch // n_seq
    zero_init = h0 is None
    n_pairs = R_HEADS // 2

    def tok(b, d, c):
        return (b * nb + c + d * (nb - 1 - 2 * c), 0)

    by_dir = [w0, w2p, a0, a2p]
    consts = [a0[0], a2p[0], g2, k_k, k_a, r_k, ln_g, ln_b, hones]
    args = [xr] + by_dir + consts
    specs = [pl.BlockSpec((LB, R_COLS), tok)]
    specs += [pl.BlockSpec((1,) + a.shape[1:], lambda b, d, c: (d, 0, 0)) for a in by_dir]
    specs += [_const_spec(a.shape) for a in consts]
    if not zero_init:
        args.append(h0)
        specs.append(pl.BlockSpec((n_seq, 1, n_pairs, LANES, LANES), lambda b, d, c: (b, d, 0, 0, 0)))
    outs = [jax.ShapeDtypeStruct((n, R_WIDTH), F32),
            jax.ShapeDtypeStruct((batch, 2, R_HEADS, R_HEAD, R_HEAD), F32)]
    out_specs = [pl.BlockSpec((n_seq * seq_len, R_WIDTH), lambda b, d, c: (b, 0)),
                 pl.BlockSpec((n_seq, 1, R_HEADS, R_HEAD, R_HEAD), lambda b, d, c: (b, d, 0, 0, 0))]
    body = functools.partial(_rwkv_body, zero_init, nb, n_seq)
    return pl.pallas_call(
        body,
        grid=(groups, 2, nb),
        in_specs=specs,
        out_specs=out_specs,
        out_shape=outs,
        scratch_shapes=[pltpu.VMEM((n_seq, n_pairs, LANES, LANES), F32)],
        compiler_params=_cparams(("parallel", "arbitrary", "arbitrary")),
        name="rwkv_zero" if zero_init else "rwkv_init",
    )(*args)


def _pairs_from_heads(s):
    ht = jnp.swapaxes(s, -1, -2)
    lead = ht.shape[:-3]
    ht = ht.reshape(lead + (R_HEADS // 2, 2, R_HEAD, R_HEAD))
    z = jnp.zeros_like(ht[..., 0, :, :])
    top = jnp.concatenate([ht[..., 0, :, :], z], axis=-1)
    bot = jnp.concatenate([z, ht[..., 1, :, :]], axis=-1)
    return jnp.concatenate([top, bot], axis=-2)


def _trunk(x, mod3, mod_row0, per_seq_rows, grid_mode, init, w):
    batch, seq_len, d = x.shape
    x2 = x.reshape(batch * seq_len, d)

    def rows_for(tile):
        per = seq_len // tile
        if per_seq_rows:
            return lambda i: mod_row0 + i // per
        return lambda i: mod_row0

    x1 = _ffn_call(x2, mod3, rows_for(FFN_TILE), 0, w["norm_ffn1"], w["ffn1_in"], w["ffn1_out"])
    mix_tile = MIX_TILE if grid_mode else seq_len
    q, k, vt, ot, gc, gr, xr = _mixin_call(
        x1, seq_len, grid_mode, mod3, rows_for(mix_tile), w["norm_mix"], w["wqk"], w["wvot"], w["wg"], w["wgt"],
        w["wr"], w["conv_w"], w["conv_b"], w["mu"])
    if init is None:
        m_init = None
        r_init = None
    else:
        c0, n0, m0, s0 = init
        m0p = jnp.broadcast_to(jnp.pad(m0, ((0, 0), (0, 0), (0, 8 - M_HEADS)))[..., None], m0.shape[:2] + (8, LANES))
        m_init = (c0, n0, m0p)
        r_init = _pairs_from_heads(s0)
    mo, cs, ns, ms = _mlstm_call(q, k, vt, ot, gc, gr, w["gate_b_row"], w["gate_b_col"], w["head_g_col"],
                                 batch, seq_len, m_init)
    ro, hs = _rwkv_call(xr, w["r_w0"], w["r_w2p"], w["r_a0"], w["r_a2p"], w["r_g2"], w["r_k_k"], w["r_k_a"],
                        w["r_r_k"], w["r_ln_g"], w["r_ln_b"], w["hones"], batch, seq_len, r_init)
    y = _ffn_call(x1, mod3, rows_for(FFN_TILE), 6, w["norm_ffn2"], w["ffn2_in"], w["ffn2_out"],
                  mix=(mo, ro, w["wo_m"], w["wo_r"]), g_final=w["norm_final"])
    states = (cs, ns, ms[:, :, :M_HEADS, 0], hs)
    return y.reshape(batch, seq_len, d), states


def _prepare_weights(ada_w, ada_b, norm_ffn1, ffn1_w_in, ffn1_w_out, norm_mix, mix_w_in, mix_w_out,
                     m_conv_w, m_conv_b, m_gate_b, m_head_g, r_mu, r_w0, r_w2, r_a0, r_a2, r_g2,
                     r_k_k, r_k_a, r_r_k, r_ln_g, r_ln_b, norm_ffn2, ffn2_w_in, ffn2_w_out, norm_final):
    assert ada_w.shape[0] == 1, "single trunk layer"
    lora = r_w2.shape[2]
    nm = 4 * M_WIDTH

    w_in = mix_w_in[0]
    gate_w = w_in[:, nm:nm + GATE_COLS]
    zpad = jnp.zeros((lora, R_WIDTH), F32)

    def dir_pad(w2):
        return jnp.stack([jnp.concatenate([w2[0], zpad], axis=0), jnp.concatenate([zpad, w2[1]], axis=0)])

    head_id = jnp.arange(LANES) // R_HEAD
    return dict(
        norm_ffn1=norm_ffn1, norm_mix=norm_mix, norm_ffn2=norm_ffn2, norm_final=norm_final[None],
        ffn1_in=ffn1_w_in[0].astype(BF16), ffn1_out=ffn1_w_out[0].astype(BF16),
        ffn2_in=ffn2_w_in[0].astype(BF16), ffn2_out=ffn2_w_out[0].astype(BF16),
        wqk=w_in[:, :2 * M_WIDTH].astype(BF16), wvot=w_in[:, 2 * M_WIDTH:nm].T.astype(BF16),
        wg=jnp.pad(gate_w, ((0, 0), (0, LANES - GATE_COLS))).astype(BF16),
        wgt=gate_w.T.astype(BF16),
        wr=w_in[:, nm + GATE_COLS:].astype(BF16),
        wo_m=mix_w_out[0, :M_WIDTH].astype(BF16), wo_r=mix_w_out[0, M_WIDTH:].astype(BF16),
        conv_w=m_conv_w[0], conv_b=m_conv_b, mu=r_mu[0],
        gate_b_row=jnp.pad(m_gate_b[0].reshape(1, GATE_COLS), ((0, 0), (0, LANES - GATE_COLS))),
        gate_b_col=m_gate_b[0].reshape(GATE_COLS, 1),
        head_g_col=m_head_g.reshape(M_WIDTH, 1),
        r_w0=r_w0[0][:, None, :], r_w2p=dir_pad(r_w2[0]).astype(BF16),
        r_a0=r_a0[0][:, None, :], r_a2p=dir_pad(r_a2[0]).astype(BF16),
        r_g2=r_g2[0].astype(BF16), r_k_k=r_k_k, r_k_a=r_k_a, r_r_k=r_r_k[0].reshape(1, R_WIDTH),
        r_ln_g=r_ln_g, r_ln_b=r_ln_b,
        hones=(head_id[:, None] == head_id[None, :]).astype(BF16),
    )


def kernel(x_prompt, x_sample, c, state_mlstm_C, state_mlstm_n, state_mlstm_m, state_rwkv_S, c_ctx,
           ada_w, ada_b, norm_ffn1, ffn1_w_in, ffn1_w_out, norm_mix, mix_w_in, mix_w_out,
           m_conv_w, m_conv_b, m_gate_b, m_head_g, r_mu, r_w0, r_w2, r_a0, r_a2, r_g2,
           r_k_k, r_k_a, r_r_k, r_ln_g, r_ln_b, norm_ffn2, ffn2_w_in, ffn2_w_out, norm_final):
    w = _prepare_weights(ada_w, ada_b, norm_ffn1, ffn1_w_in, ffn1_w_out, norm_mix, mix_w_in, mix_w_out,
                         m_conv_w, m_conv_b, m_gate_b, m_head_g, r_mu, r_w0, r_w2, r_a0, r_a2, r_g2,
                         r_k_k, r_k_a, r_r_k, r_ln_g, r_ln_b, norm_ffn2, ffn2_w_in, ffn2_w_out, norm_final)
    d = x_prompt.shape[-1]
    dec_batch = x_sample.shape[0]
    cond = jnp.concatenate([c_ctx[None], c, jnp.zeros((16 - 1 - dec_batch, d), F32)], axis=0)
    mod3 = _ada_call(cond, ada_w[0], ada_b).reshape(16, 9, d)

    y_prompt, (cs, ns, ms, ss) = _trunk(x_prompt, mod3, 0, False, False, None, w)
    init = (state_mlstm_C[:, 0], state_mlstm_n[:, 0], state_mlstm_m[:, 0], state_rwkv_S[:, 0])
    y_sample, _ = _trunk(x_sample, mod3, 1, True, True, init, w)
    return (y_prompt, y_sample, cs[:, None], ns[:, None], ms[:, None], ss[:, None])
```

```python
import functools
import math

import jax
import jax.numpy as jnp
from jax import lax
from jax.experimental import pallas as pl
from jax.experimental.pallas import tpu as pltpu

F32 = jnp.float32
BF16 = jnp.bfloat16

EPS = 1e-6
R_LN_EPS = 64e-5
GRID_W = 64
M_HEADS = 4
M_DK = 128
M_DV = 128
M_WIDTH = M_HEADS * M_DV
R_HEADS = 8
R_HEAD = 64
R_WIDTH = R_HEADS * R_HEAD
R_LORA = 128
R_COLS = 3 * R_WIDTH + 3 * R_LORA
GATE_COLS = 16
LANES = 128

ADA_TILE = 2304
FFN_TILE = 1024
FFN_CHUNK = 256
MIX_TILE = 512
M_CHUNK = 256
R_CHUNK = 64
R_CHUNKS_PER_STEP = 8
VMEM_LIMIT = 56 * 1024 * 1024


def _dot(a, b):
    return jnp.dot(a, b, preferred_element_type=F32)


def _dot_nt(a, b):
    return lax.dot_general(a, b, (((1,), (1,)), ((), ())), preferred_element_type=F32)


def _dot_tn(a, b):
    return lax.dot_general(a, b, (((0,), (0,)), ((), ())), preferred_element_type=F32)


def _split(x, parts):
    pieces = []
    for _ in range(parts - 1):
        p = x.astype(BF16)
        pieces.append(p)
        x = x - p.astype(F32)
    pieces.append(x.astype(BF16))
    return pieces


def _sel_dot(mat, x, parts=3):
    return sum(_dot(mat, p) for p in _split(x, parts))


def _dot_sel(x, mat, parts=3):
    return sum(_dot(p, mat) for p in _split(x, parts))


def _silu(x):
    return x * jax.nn.sigmoid(x)


def _rms_mod(x, g, shift, scale):
    y = x * lax.rsqrt(jnp.mean(x * x, axis=-1, keepdims=True) + EPS) * g
    return y * (1.0 + scale) + shift


def _cparams(sem):
    return pltpu.CompilerParams(dimension_semantics=sem, vmem_limit_bytes=VMEM_LIMIT)


def _const_spec(shape):
    nd = len(shape)
    return pl.BlockSpec(shape, lambda *_: (0,) * nd, pipeline_mode=pl.Buffered(1))


def _ada_body(c_ref, w_ref, b_ref, o_ref):
    s = _silu(c_ref[...])
    o_ref[...] = _dot(s.astype(BF16), w_ref[...].astype(BF16)) + b_ref[...]


def _ada_call(cond, w, b):
    rows, d = cond.shape
    n = w.shape[1]
    tn = ADA_TILE if n % ADA_TILE == 0 else d
    return pl.pallas_call(
        _ada_body,
        grid=(n // tn,),
        in_specs=[
            pl.BlockSpec((rows, d), lambda j: (0, 0)),
            pl.BlockSpec((d, tn), lambda j: (0, j)),
            pl.BlockSpec((1, tn), lambda j: (0, j)),
        ],
        out_specs=pl.BlockSpec((rows, tn), lambda j: (0, j)),
        out_shape=jax.ShapeDtypeStruct((rows, n), F32),
        compiler_params=_cparams(("arbitrary",)),
        name="ada",
    )(cond, w, b)


def _ffn_body(mod_base, with_mix, with_final, *refs):
    refs = list(refs)
    x_ref = refs.pop(0)
    if with_mix:
        mo_ref, ro_ref, wmm_ref, wmr_ref = refs[:4]
        refs = refs[4:]
    mod_ref, g_ref, win_ref, wout_ref = refs[:4]
    refs = refs[4:]
    if with_final:
        gfin_ref = refs.pop(0)
    out_ref, a_scr = refs

    x = x_ref[...]
    if with_mix:
        mix = _dot(mo_ref[...].astype(BF16), wmm_ref[...]) + _dot(ro_ref[...].astype(BF16), wmr_ref[...])
        x = x + mod_ref[0, mod_base - 1:mod_base, :] * mix
    shift = mod_ref[0, mod_base:mod_base + 1, :]
    scale = mod_ref[0, mod_base + 1:mod_base + 2, :]
    gate = mod_ref[0, mod_base + 2:mod_base + 3, :]
    hb = _rms_mod(x, g_ref[...], shift, scale).astype(BF16)
    dff = wout_ref.shape[0]
    for j in range(dff // FFN_CHUNK):
        lo = j * FFN_CHUNK
        gt = _dot(hb, win_ref[:, lo:lo + FFN_CHUNK])
        up = _dot(hb, win_ref[:, dff + lo:dff + lo + FFN_CHUNK])
        a_scr[:, lo:lo + FFN_CHUNK] = (_silu(gt) * up).astype(BF16)
    y = x + (0.5 * gate) * _dot(a_scr[...], wout_ref[...])
    if with_final:
        y = y * lax.rsqrt(jnp.mean(y * y, axis=-1, keepdims=True) + EPS) * gfin_ref[...]
    out_ref[...] = y


def _ffn_call(x, mod3, row_of_tile, mod_base, g, w_in, w_out, mix=None, g_final=None):
    n, d = x.shape
    tm = FFN_TILE
    dff = w_out.shape[0]
    assert n % tm == 0 and dff % FFN_CHUNK == 0, (n, dff)
    tok = lambda i: (i, 0)
    args = [x]
    specs = [pl.BlockSpec((tm, d), tok)]
    if mix is not None:
        mo, ro, wmm, wmr = mix
        args += [mo, ro, wmm, wmr]
        specs += [pl.BlockSpec((tm, mo.shape[1]), tok), pl.BlockSpec((tm, ro.shape[1]), tok),
                  _const_spec(wmm.shape), _const_spec(wmr.shape)]
    args += [mod3, g, w_in, w_out]
    specs += [pl.BlockSpec((1,) + mod3.shape[1:], lambda i: (row_of_tile(i), 0, 0)),
              _const_spec(g.shape), _const_spec(w_in.shape), _const_spec(w_out.shape)]
    if g_final is not None:
        args.append(g_final)
        specs.append(_const_spec(g_final.shape))
    body = functools.partial(_ffn_body, mod_base, mix is not None, g_final is not None)
    return pl.pallas_call(
        body,
        grid=(n // tm,),
        in_specs=specs,
        out_specs=pl.BlockSpec((tm, d), tok),
        out_shape=jax.ShapeDtypeStruct((n, d), F32),
        scratch_shapes=[pltpu.VMEM((tm, dff), BF16)],
        compiler_params=_cparams(("parallel",)),
        name="ffn_mix" if mix is not None else "ffn",
    )(*args)


def _mixin_body(grid_mode, has_halo, tiles_per_seq, *refs):
    refs = list(refs)
    x_ref = refs.pop(0)
    if has_halo:
        xp_ref, xn_ref = refs[:2]
        refs = refs[2:]
    (mod_ref, g_ref, wqk_ref, wvot_ref, wg_ref, wgt_ref, wr_ref, cw_ref, cb_ref, mu_ref,
     q_ref, k_ref, vt_ref, ot_ref, gc_ref, gr_ref, xr_ref) = refs

    tm = x_ref.shape[0]
    shift = mod_ref[0, 3:4, :]
    scale = mod_ref[0, 4:5, :]
    g = g_ref[...]
    hb = _rms_mod(x_ref[...], g, shift, scale).astype(BF16)
    row = lax.broadcasted_iota(jnp.int32, (tm, 1), 0)
    nqk = 2 * M_WIDTH

    if has_halo:
        i = pl.program_id(0)
        pos = i % tiles_per_seq
        keep_p = (pos > 0).astype(F32)
        keep_n = (pos < tiles_per_seq - 1).astype(F32)
        hp32 = _rms_mod(xp_ref[...], g, shift, scale)
        hn32 = _rms_mod(xn_ref[...], g, shift, scale)
        hp = hp32.astype(BF16)
        hn = hn32.astype(BF16)
        nh = hp32.shape[0]
        edge = 16
        qk_prev = _dot(hp32[nh - edge:, :].astype(BF16), wqk_ref[...])[edge - 1:edge, :] * keep_p
        qk_next = _dot(hn32[:edge, :].astype(BF16), wqk_ref[...])[0:1, :] * keep_n
    else:
        qk_prev = jnp.zeros((1, nqk), F32)
        qk_next = jnp.zeros((1, nqk), F32)

    uqk = _dot(hb, wqk_ref[...])
    ur = _dot(hb, wr_ref[...])
    u_dn = jnp.where(row == 0, qk_prev, pltpu.roll(uqk, 1, 0))
    u_up = jnp.where(row == tm - 1, qk_next, pltpu.roll(uqk, tm - 1, 0))
    qk = _silu(cw_ref[0:1, :] * u_dn + cw_ref[1:2, :] * uqk + cw_ref[2:3, :] * u_up + cb_ref[...])
    q_ref[...] = qk[:, :M_WIDTH].astype(BF16)
    k_ref[...] = (qk[:, M_WIDTH:] * (M_DK ** -0.5)).astype(BF16)
    vo_t = _dot_nt(wvot_ref[...], hb)
    gc_ref[...] = _dot(hb, wg_ref[...])
    gr_ref[...] = _dot_nt(wgt_ref[...], hb)

    if grid_mode:
        col = row % GRID_W
        left = jnp.where(col == 0, 0.0, pltpu.roll(ur, 1, 0))
        right = jnp.where(col == GRID_W - 1, 0.0, pltpu.roll(ur, tm - 1, 0))
        ur_p = _dot(hp, wr_ref[...]) * keep_p
        ur_n = _dot(hn, wr_ref[...]) * keep_n
        up = jnp.concatenate([ur_p, ur[:tm - GRID_W, :]], axis=0)
        down = jnp.concatenate([ur[GRID_W:, :], ur_n], axis=0)
        mu_self = 1.0 - (mu_ref[0:1, :] + mu_ref[1:2, :] + mu_ref[2:3, :] + mu_ref[3:4, :])
        xr = (mu_self * ur + mu_ref[0:1, :] * left + mu_ref[1:2, :] * right
              + mu_ref[2:3, :] * up + mu_ref[3:4, :] * down)
    else:
        left = jnp.where(row == 0, 0.0, pltpu.roll(ur, 1, 0))
        right = jnp.where(row == tm - 1, 0.0, pltpu.roll(ur, tm - 1, 0))
        mu_self = 1.0 - (mu_ref[0:1, :] + mu_ref[1:2, :])
        xr = mu_self * ur + mu_ref[0:1, :] * left + mu_ref[1:2, :] * right
    xr_ref[...] = xr
    vt_ref[...] = vo_t[:M_WIDTH, :].astype(BF16)
    ot_ref[...] = vo_t[M_WIDTH:, :]


def _mixin_call(x, seq_len, grid_mode, mod3, row_of_tile, g, wqk, wvot, wg, wgt, wr, cw, cb, mu):
    n, d = x.shape
    if grid_mode:
        tm = MIX_TILE
        has_halo = True
    else:
        tm = seq_len
        has_halo = False
    assert seq_len % tm == 0 and tm % GRID_W == 0, (seq_len, tm)
    tiles_per_seq = seq_len // tm
    tok = lambda i: (i, 0)
    args = [x]
    specs = [pl.BlockSpec((tm, d), tok)]
    if has_halo:
        hb = R_CHUNK
        per = tm // hb
        last = n // hb - 1
        args += [x, x]
        specs += [pl.BlockSpec((hb, d), lambda i: (jnp.maximum(i * per - 1, 0), 0)),
                  pl.BlockSpec((hb, d), lambda i: (jnp.minimum((i + 1) * per, last), 0))]
    args += [mod3, g, wqk, wvot, wg, wgt, wr, cw, cb, mu]
    specs += [pl.BlockSpec((1,) + mod3.shape[1:], lambda i: (row_of_tile(i), 0, 0))]
    specs += [_const_spec(a.shape) for a in (g, wqk, wvot, wg, wgt, wr, cw, cb, mu)]
    outs = [jax.ShapeDtypeStruct((n, M_WIDTH), BF16)] * 2 + [
        jax.ShapeDtypeStruct((n // seq_len, M_WIDTH, seq_len), BF16),
        jax.ShapeDtypeStruct((n // seq_len, M_WIDTH, seq_len), F32),
        jax.ShapeDtypeStruct((n, LANES), F32),
        jax.ShapeDtypeStruct((n // seq_len, GATE_COLS, seq_len), F32),
        jax.ShapeDtypeStruct((n, R_COLS), F32),
    ]
    tok_t = lambda i: (i // tiles_per_seq, 0, i % tiles_per_seq)
    out_specs = [pl.BlockSpec((tm, M_WIDTH), tok)] * 2 + [pl.BlockSpec((None, M_WIDTH, tm), tok_t)] * 2 + [
        pl.BlockSpec((tm, LANES), tok),
        pl.BlockSpec((None, GATE_COLS, tm), tok_t),
        pl.BlockSpec((tm, R_COLS), tok),
    ]
    body = functools.partial(_mixin_body, grid_mode, has_halo, tiles_per_seq)
    return pl.pallas_call(
        body,
        grid=(n // tm,),
        in_specs=specs,
        out_specs=out_specs,
        out_shape=outs,
        compiler_params=_cparams(("parallel",)),
        name="mix_in_grid" if grid_mode else "mix_in_seq",
    )(*args)


def _mlstm_body(zero_init, nc, n_seq, *refs):
    refs = list(refs)
    q_ref, k_ref, vt_ref, ot_ref, gc_ref, gr_ref, gbr_ref, gbc_ref, hg_ref = refs[:9]
    refs = refs[9:]
    if not zero_init:
        c0_ref, n0_ref, m0_ref = refs[:3]
        refs = refs[3:]
    out_ref, cs_ref, ns_ref, ms_ref, ct_scr, n_scr, m_scr, acc_scr = refs

    d = pl.program_id(1)
    c = pl.program_id(2)
    L = q_ref.shape[1]
    fwd = d == 0
    heads = range(M_HEADS)
    hsl = [slice(h * M_DV, (h + 1) * M_DV) for h in heads]
    seqs = range(n_seq)

    @pl.when(c == 0)
    def _():
        if zero_init:
            ct_scr[...] = jnp.zeros_like(ct_scr)
            n_scr[...] = jnp.zeros_like(n_scr)
            m_scr[...] = jnp.zeros_like(m_scr)
        else:
            for q in seqs:
                for h in heads:
                    ct_scr[q, h] = c0_ref[q, 0, h].T
                n_scr[q, 0:M_HEADS, :] = n0_ref[q, 0]
                m_scr[q] = m0_ref[q, 0]

    ri = lax.broadcasted_iota(jnp.int32, (L, L), 0)
    ci = lax.broadcasted_iota(jnp.int32, (L, L), 1)
    sgn = 1 - 2 * d
    tri = ((ci - ri) * sgn <= 0).astype(BF16)
    mask_st = (ri - ci) * sgn <= 0
    tri_t = mask_st.astype(BF16)

    gcol = [gc_ref[q] + gbr_ref[...] for q in seqs]
    grow = [gr_ref[q] + gbc_ref[...] for q in seqs]
    bcol = [_sel_dot(tri, jax.nn.log_sigmoid(x)) for x in gcol]
    brow = [_dot_sel(jax.nn.log_sigmoid(x), tri_t) for x in grow]

    def pick_col(a, j0, j1):
        return jnp.where(fwd, a[:, j0:j0 + 1], a[:, j1:j1 + 1])

    def pick_row(a, j0, j1):
        return jnp.where(fwd, a[j0:j0 + 1, :], a[j1:j1 + 1, :])

    cc = c + d * (nc - 1 - 2 * c)
    row0 = lax.broadcasted_iota(jnp.int32, (16, LANES), 0) == 0
    row0_l = lax.broadcasted_iota(jnp.int32, (16, L), 0) == 0

    chains = [(q, h) for q in seqs for h in heads]
    ids = range(len(chains))
    qb = [q_ref[q, :, hsl[h]] for q, h in chains]
    kb = [k_ref[q, :, hsl[h]] for q, h in chains]
    vt = [vt_ref[q, hsl[h], :] for q, h in chains]
    ra_c = [pick_col(gcol[q], h, 4 + h) - pick_col(bcol[q], 8 + h, 12 + h) for q, h in chains]
    i_r = [pick_row(grow[q], h, 4 + h) for q, h in chains]
    b_r = [pick_row(brow[q], 8 + h, 12 + h) for q, h in chains]
    b_tot = [jnp.where(fwd, b_r[i][:, L - 1:L], b_r[i][:, 0:1]) for i in ids]
    m_prev = [m_scr[q, h:h + 1, 0:1] for q, h in chains]
    n_prev = [n_scr[q, h:h + 1, :] for q, h in chains]
    ct_prev = [ct_scr[q, h] for q, h in chains]

    kq = [_dot_nt(kb[i], qb[i]) for i in ids]
    qct = [_dot_nt(ct_prev[i].astype(BF16), qb[i]) for i in ids]
    qn = [_dot_nt(jnp.where(row0, n_prev[i], 0.0).astype(BF16), qb[i])[0:1, :] for i in ids]
    dmt = [jnp.where(mask_st, ra_c[i], -jnp.inf) for i in ids]
    mx = [jnp.maximum(m_prev[i], jnp.max(dmt[i], axis=0, keepdims=True)) for i in ids]
    st = [kq[i] * jnp.exp(dmt[i] - mx[i]) for i in ids]
    sc = [jnp.exp(m_prev[i] - mx[i]) for i in ids]
    numt = [_dot(vt[i], st[i].astype(BF16)) + sc[i] * qct[i] for i in ids]
    den = [jnp.sum(st[i], axis=0, keepdims=True) + sc[i] * qn[i] for i in ids]
    ht = [numt[i] / jnp.maximum(jnp.abs(den[i]), jnp.exp(-(b_r[i] + mx[i]))) for i in ids]

    gs = [b_tot[i] - b_r[i] + i_r[i] for i in ids]
    m_new = [jnp.maximum(b_tot[i] + m_prev[i], jnp.max(gs[i], axis=1, keepdims=True)) for i in ids]
    wk = [jnp.exp(gs[i] - m_new[i]) for i in ids]
    decay = [jnp.exp(b_tot[i] + m_prev[i] - m_new[i]) for i in ids]
    ckv = [_dot((vt[i].astype(F32) * wk[i]).astype(BF16), kb[i]) for i in ids]
    nk = [_dot(jnp.where(row0_l, wk[i], 0.0).astype(BF16), kb[i])[0:1, :] for i in ids]
    for i, (q, h) in enumerate(chains):
        ct_scr[q, h] = decay[i] * ct_prev[i] + ckv[i]
        n_scr[q, h:h + 1, :] = decay[i] * n_prev[i] + nk[i]
        m_scr[q, h:h + 1, :] = jnp.broadcast_to(m_new[i], (1, LANES))

    @pl.when(fwd)
    def _():
        for q in seqs:
            acc_scr[cc, q] = jnp.concatenate(ht[q * M_HEADS:(q + 1) * M_HEADS], axis=0)

    @pl.when(jnp.logical_not(fwd))
    def _():
        for q in seqs:
            row = []
            for h in heads:
                i = q * M_HEADS + h
                t = jax.nn.sigmoid(ot_ref[q, hsl[h], :]) * (acc_scr[cc, q, hsl[h], :] + ht[i])
                t = t * lax.rsqrt(jnp.mean(t * t, axis=0, keepdims=True) + EPS) * hg_ref[hsl[h], :]
                row.append(t.T)
            out_ref[q] = jnp.concatenate(row, axis=1).astype(BF16)

    @pl.when(c == nc - 1)
    def _():
        for q in seqs:
            for h in heads:
                cs_ref[q, 0, h] = ct_scr[q, h].T
            ns_ref[q, 0] = n_scr[q, 0:M_HEADS, :]
            ms_ref[q, 0] = m_scr[q]


def _mlstm_call(q, k, vt, ot, gc, gr, gate_b_row, gate_b_col, head_g_col, batch, seq_len, init):
    n = q.shape[0]
    L = min(M_CHUNK, seq_len)
    assert seq_len % L == 0, (seq_len, L)
    nc = seq_len // L
    zero_init = init is None
    n_seq = 2 if batch % 2 == 0 else 1
    groups = batch // n_seq
    per_seq = lambda a: a.reshape(batch, seq_len, a.shape[-1])

    def chunk(d, c):
        return c + d * (nc - 1 - 2 * c)

    tok = lambda b, d, c: (b, chunk(d, c), 0)
    tok_t = lambda b, d, c: (b, 0, chunk(d, c))
    tok_out = lambda b, d, c: (b, nc - 1 - c * d, 0)

    args = [per_seq(q), per_seq(k), vt, ot, per_seq(gc), gr, gate_b_row, gate_b_col, head_g_col]
    specs = [pl.BlockSpec((n_seq, L, M_WIDTH), tok)] * 2 + [pl.BlockSpec((n_seq, M_WIDTH, L), tok_t)] * 2 + [
        pl.BlockSpec((n_seq, L, LANES), tok),
        pl.BlockSpec((n_seq, GATE_COLS, L), tok_t),
        _const_spec(gate_b_row.shape), _const_spec(gate_b_col.shape), _const_spec(head_g_col.shape),
    ]
    if not zero_init:
        c0, n0, m0 = init
        args += [c0, n0, m0]
        specs += [pl.BlockSpec((n_seq, 1) + c0.shape[2:], lambda b, d, c: (b, d, 0, 0, 0)),
                  pl.BlockSpec((n_seq, 1) + n0.shape[2:], lambda b, d, c: (b, d, 0, 0)),
                  pl.BlockSpec((n_seq, 1) + m0.shape[2:], lambda b, d, c: (b, d, 0, 0))]
    outs = [jax.ShapeDtypeStruct((batch, seq_len, M_WIDTH), BF16),
            jax.ShapeDtypeStruct((batch, 2, M_HEADS, M_DK, M_DV), F32),
            jax.ShapeDtypeStruct((batch, 2, M_HEADS, M_DK), F32),
            jax.ShapeDtypeStruct((batch, 2, 8, LANES), F32)]
    out_specs = [pl.BlockSpec((n_seq, L, M_WIDTH), tok_out),
                 pl.BlockSpec((n_seq, 1, M_HEADS, M_DK, M_DV), lambda b, d, c: (b, d, 0, 0, 0)),
                 pl.BlockSpec((n_seq, 1, M_HEADS, M_DK), lambda b, d, c: (b, d, 0, 0)),
                 pl.BlockSpec((n_seq, 1, 8, LANES), lambda b, d, c: (b, d, 0, 0))]
    body = functools.partial(_mlstm_body, zero_init, nc, n_seq)
    res = pl.pallas_call(
        body,
        grid=(groups, 2, nc),
        in_specs=specs,
        out_specs=out_specs,
        out_shape=outs,
        scratch_shapes=[pltpu.VMEM((n_seq, M_HEADS, M_DV, M_DK), F32),
                        pltpu.VMEM((n_seq, 8, LANES), F32),
                        pltpu.VMEM((n_seq, 8, LANES), F32),
                        pltpu.VMEM((nc, n_seq, M_WIDTH, L), F32)],
        compiler_params=_cparams(("parallel", "arbitrary", "arbitrary")),
        name="mlstm_zero" if zero_init else "mlstm_init",
    )(*args)
    return (res[0].reshape(n, M_WIDTH),) + tuple(res[1:])


def _rwkv_body(zero_init, nb, n_seq, *refs):
    refs = list(refs)
    (xr_ref, w0_ref, w2_ref, a0_ref, a2_ref, a0f_ref, a2f_ref, g2_ref, kk_ref, ka_ref, rk_ref,
     lng_ref, lnb_ref, hones_ref) = refs[:14]
    refs = refs[14:]
    if not zero_init:
        h0_ref = refs.pop(0)
    out_ref, hs_ref, h_scr = refs

    d = pl.program_id(1)
    c = pl.program_id(2)
    LB = xr_ref.shape[0]
    L = R_CHUNK
    n_chunks = LB // L
    P = 2 * L
    fwd = d == 0
    sgn = 1 - 2 * d
    pairs = range(R_HEADS // 2)
    cols = [slice(p * LANES, (p + 1) * LANES) for p in pairs]

    per_seq = n_chunks // n_seq

    def seq_of(j):
        return j // per_seq

    @pl.when(c == 0)
    def _():
        if zero_init:
            h_scr[...] = jnp.zeros_like(h_scr)
        else:
            h_scr[...] = h0_ref[:, 0]

    r = xr_ref[:, 0:R_WIDTH]
    k = xr_ref[:, R_WIDTH:2 * R_WIDTH]
    v = xr_ref[:, 2 * R_WIDTH:3 * R_WIDTH]
    wd = xr_ref[:, 3 * R_WIDTH:3 * R_WIDTH + R_LORA]
    ad = xr_ref[:, 3 * R_WIDTH + R_LORA:3 * R_WIDTH + 2 * R_LORA]
    gin = xr_ref[:, 3 * R_WIDTH + 2 * R_LORA:]
    hones = hones_ref[...]

    def head_sum(x):
        return jnp.concatenate([_dot_sel(x[:, cs], hones, 2) for cs in cols], axis=1)

    tw = jnp.tanh(wd).astype(BF16)
    adb = ad.astype(BF16)
    ww = w0_ref[0] + _dot(tw, w2_ref[0])
    lw = -math.exp(-0.5) * jax.nn.sigmoid(ww)
    a = jax.nn.sigmoid(a0_ref[0] + _dot(adb, a2_ref[0]))
    ka = ka_ref[...]
    kd = k * (1.0 + (a - 1.0) * ka)
    kk = k * kk_ref[...]
    kk = kk * lax.rsqrt(jnp.maximum(head_sum(kk * kk), 1e-24))
    bvec = kk * a

    ri = lax.broadcasted_iota(jnp.int32, (L, L), 0)
    ci = lax.broadcasted_iota(jnp.int32, (L, L), 1)
    tri = ((ci - ri) * sgn <= 0).astype(BF16)
    cl_j = [_sel_dot(tri, lw[j * L:(j + 1) * L, :], 2) for j in range(n_chunks)]
    tot_j = [jnp.where(fwd, x[L - 1:L, :], x[0:1, :]) for x in cl_j]
    cl = jnp.concatenate(cl_j, axis=0)
    w_end = [jnp.exp(x) for x in tot_j]
    e_in = jnp.exp(cl)
    e_out = jnp.exp(-cl)
    e_end = jnp.concatenate([e_out[j * L:(j + 1) * L, :] * w_end[j] for j in range(n_chunks)], axis=0)
    a_til = -kk * jnp.exp(cl - lw)
    r_til = r * e_in
    b_til = bvec * e_out
    k_til = kd * e_out
    b_hat = bvec * e_end
    k_hat = kd * e_end

    pr = lax.broadcasted_iota(jnp.int32, (P, P), 0)
    pc = lax.broadcasted_iota(jnp.int32, (P, P), 1)
    same = (pr // L) == (pc // L)
    before = (pc - pr) * sgn < 0
    m_strict = jnp.logical_and(same, before)
    m_incl = jnp.logical_and(same, jnp.logical_or(before, pc == pr))
    eye = pr == pc
    lane = lax.broadcasted_iota(jnp.int32, (L, LANES), 1)
    first = lane < R_HEAD

    def stack(x, j, p):
        x = x[j * L:(j + 1) * L, cols[p]].astype(BF16)
        zero = jnp.zeros_like(x)
        return jnp.concatenate([jnp.where(first, x, zero), jnp.where(first, zero, x)], axis=0)

    def twice(x, j, p):
        x = x[j * L:(j + 1) * L, cols[p]].astype(BF16)
        return jnp.concatenate([x, x], axis=0)

    chains = [(j, p) for j in range(n_chunks) for p in pairs]
    at = [stack(a_til, j, p) for j, p in chains]
    rt = [stack(r_til, j, p) for j, p in chains]
    bt = [twice(b_til, j, p) for j, p in chains]
    kt = [twice(k_til, j, p) for j, p in chains]
    bh = [stack(b_hat, j, p) for j, p in chains]
    kh = [stack(k_hat, j, p) for j, p in chains]
    vs = [stack(v, j, p) for j, p in chains]
    ids = range(len(chains))

    big = [_dot_nt(jnp.concatenate([at[i], rt[i]], axis=0), jnp.concatenate([bt[i], kt[i]], axis=0)) for i in ids]
    a_ab = [jnp.where(m_strict, big[i][:P, :P], 0.0) for i in ids]
    a_ak = [jnp.where(m_strict, big[i][:P, P:], 0.0).astype(BF16) for i in ids]
    a_rb = [jnp.where(m_incl, big[i][P:, :P], 0.0).astype(BF16) for i in ids]
    a_rk = [jnp.where(m_incl, big[i][P:, P:], 0.0).astype(BF16) for i in ids]

    def sib_mask(blk):
        return jnp.logical_and((pr // (2 * blk)) == (pc // (2 * blk)), (pr // blk) != (pc // blk))

    def to_wide(x, w):
        lane_blk = lax.broadcasted_iota(jnp.int32, (w, P), 1) // w
        return sum(jnp.where(lane_blk == k, x[k * w:(k + 1) * w, :], 0.0) for k in range(P // w))

    def to_diag(x, w):
        lane_blk = lax.broadcasted_iota(jnp.int32, (w, P), 1) // w
        return jnp.concatenate([jnp.where(lane_blk == k, x, 0.0) for k in range(P // w)], axis=0)

    pair2 = (pr // 2) == (pc // 2)
    tinv = [jnp.where(eye, 1.0, jnp.where(pair2, a_ab[i], 0.0)) for i in ids]
    blk = 2
    for wide in (L // 2, L):
        tw = [to_wide(tinv[i], wide) for i in ids]
        while 2 * blk <= wide:
            sib = sib_mask(blk)
            tb = [tw[i].astype(BF16) for i in ids]
            half = [_dot(tb[i], jnp.where(sib, a_ab[i], 0.0).astype(BF16)) for i in ids]
            tw = [tw[i] + _dot(half[i].astype(BF16), to_diag(tw[i], wide).astype(BF16)) for i in ids]
            blk *= 2
        tinv = [to_diag(tw[i], wide) for i in ids]

    av = [_dot(a_ak[i], vs[i]) for i in ids]
    pq = [_dot(tinv[i].astype(BF16), jnp.concatenate([at[i], av[i].astype(BF16)], axis=1)).astype(BF16) for i in ids]
    ry = [_dot(a_rb[i], pq[i]) for i in ids]
    rkv = [_dot(a_rk[i], vs[i]) for i in ids]
    mg = [_dot_tn(bh[i], pq[i]) for i in ids]
    kv = [_dot_tn(kh[i], vs[i]) for i in ids]
    r_hat = [(rt[i].astype(F32) + ry[i][:, :LANES]).astype(BF16) for i in ids]
    y0 = [ry[i][:, LANES:] + rkv[i] for i in ids]
    m_corr = [mg[i][:, :LANES].astype(BF16) for i in ids]
    g_add = [mg[i][:, LANES:] + kv[i] for i in ids]
    w_col = [jnp.sum(jnp.where(eye, jnp.broadcast_to(w_end[j][:, cols[p]], (P, P)), 0.0),
                     axis=1, keepdims=True) for j, p in chains]

    def scan_chunks(order):
        y = [None] * n_chunks
        h = {(q, p): h_scr[q, p] for q in range(n_seq) for p in pairs}
        for j in order:
            ys = []
            for p in pairs:
                i = j * len(pairs) + p
                key = (seq_of(j), p)
                hb = h[key].astype(BF16)
                yst = _dot(r_hat[i], hb) + y0[i]
                ys.append(yst[:L, :] + yst[L:, :])
                h[key] = w_col[i] * h[key] + _dot(m_corr[i], hb) + g_add[i]
            y[j] = jnp.concatenate(ys, axis=1)
        for (q, p), val in h.items():
            h_scr[q, p] = val
        return jnp.concatenate(y, axis=0)

    cc = c + d * (nb - 1 - 2 * c)
    rows = pl.ds(pl.multiple_of(cc * LB, LB), LB)

    @pl.when(fwd)
    def _():
        out_ref[rows, :] = scan_chunks(range(n_chunks))

    @pl.when(jnp.logical_not(fwd))
    def _():
        inv_n = 1.0 / R_HEAD
        ysum = out_ref[rows, :] + scan_chunks(reversed(range(n_chunks)))
        mean = head_sum(ysum) * inv_n
        yc = ysum - mean
        var = head_sum(yc * yc) * inv_n
        yn = yc * lax.rsqrt(var + R_LN_EPS) * lng_ref[...] + lnb_ref[...]
        a_f = jax.nn.sigmoid(a0f_ref[...] + _dot(adb, a2f_ref[...]))
        k_bar = k * (1.0 + (0.5 * (a_f + a) - 1.0) * ka)
        bonus = head_sum(r * k_bar * rk_ref[...]) * v
        gate = _dot(jax.nn.sigmoid(gin).astype(BF16), g2_ref[...])
        out_ref[rows, :] = (yn + bonus) * gate

    @pl.when(c == nb - 1)
    def _():
        for q in range(n_seq):
            for p in pairs:
                st = h_scr[q, p].T
                hs_ref[q, 0, 2 * p] = st[:R_HEAD, :R_HEAD]
                hs_ref[q, 0, 2 * p + 1] = st[R_HEAD:, R_HEAD:]


def _rwkv_call(xr, w0, w2p, a0, a2p, g2, k_k, k_a, r_k, ln_g, ln_b, hones, batch, seq_len, h0):
    n = xr.shape[0]
    LB = R_CHUNK * R_CHUNKS_PER_STEP
    n_seq = max(LB // seq_len, 1)
    assert (seq_len % LB == 0 or LB % seq_len == 0) and seq_len % R_CHUNK == 0 and batch % n_seq == 0
    nb = max(seq_len // LB, 1)
    groups = batch // n_seq
    zero_init = h0 is None
    n_pairs = R_HEADS // 2

    def tok(b, d, c):
        return (b * nb + c + d * (nb - 1 - 2 * c), 0)

    by_dir = [w0, w2p, a0, a2p]
    consts = [a0[0], a2p[0], g2, k_k, k_a, r_k, ln_g, ln_b, hones]
    args = [xr] + by_dir + consts
    specs = [pl.BlockSpec((LB, R_COLS), tok)]
    specs += [pl.BlockSpec((1,) + a.shape[1:], lambda b, d, c: (d, 0, 0)) for a in by_dir]
    specs += [_const_spec(a.shape) for a in consts]
    if not zero_init:
        args.append(h0)
        specs.append(pl.BlockSpec((n_seq, 1, n_pairs, LANES, LANES), lambda b, d, c: (b, d, 0, 0, 0)))
    outs = [jax.ShapeDtypeStruct((n, R_WIDTH), F32),
            jax.ShapeDtypeStruct((batch, 2, R_HEADS, R_HEAD, R_HEAD), F32)]
    out_specs = [pl.BlockSpec((n_seq * seq_len, R_WIDTH), lambda b, d, c: (b, 0)),
                 pl.BlockSpec((n_seq, 1, R_HEADS, R_HEAD, R_HEAD), lambda b, d, c: (b, d, 0, 0, 0))]
    body = functools.partial(_rwkv_body, zero_init, nb, n_seq)
    return pl.pallas_call(
        body,
        grid=(groups, 2, nb),
        in_specs=specs,
        out_specs=out_specs,
        out_shape=outs,
        scratch_shapes=[pltpu.VMEM((n_seq, n_pairs, LANES, LANES), F32)],
        compiler_params=_cparams(("parallel", "arbitrary", "arbitrary")),
        name="rwkv_zero" if zero_init else "rwkv_init",
    )(*args)


def _pairs_from_heads(s):
    ht = jnp.swapaxes(s, -1, -2)
    lead = ht.shape[:-3]
    ht = ht.reshape(lead + (R_HEADS // 2, 2, R_HEAD, R_HEAD))
    z = jnp.zeros_like(ht[..., 0, :, :])
    top = jnp.concatenate([ht[..., 0, :, :], z], axis=-1)
    bot = jnp.concatenate([z, ht[..., 1, :, :]], axis=-1)
    return jnp.concatenate([top, bot], axis=-2)


def _trunk(x, mod3, mod_row0, per_seq_rows, grid_mode, init, w):
    batch, seq_len, d = x.shape
    x2 = x.reshape(batch * seq_len, d)

    def rows_for(tile):
        per = seq_len // tile
        if per_seq_rows:
            return lambda i: mod_row0 + i // per
        return lambda i: mod_row0

    x1 = _ffn_call(x2, mod3, rows_for(FFN_TILE), 0, w["norm_ffn1"], w["ffn1_in"], w["ffn1_out"])
    mix_tile = MIX_TILE if grid_mode else seq_len
    q, k, vt, ot, gc, gr, xr = _mixin_call(
        x1, seq_len, grid_mode, mod3, rows_for(mix_tile), w["norm_mix"], w["wqk"], w["wvot"], w["wg"], w["wgt"],
        w["wr"], w["conv_w"], w["conv_b"], w["mu"])
    if init is None:
        m_init = None
        r_init = None
    else:
        c0, n0, m0, s0 = init
        m0p = jnp.broadcast_to(jnp.pad(m0, ((0, 0), (0, 0), (0, 8 - M_HEADS)))[..., None], m0.shape[:2] + (8, LANES))
        m_init = (c0, n0, m0p)
        r_init = _pairs_from_heads(s0)
    mo, cs, ns, ms = _mlstm_call(q, k, vt, ot, gc, gr, w["gate_b_row"], w["gate_b_col"], w["head_g_col"],
                                 batch, seq_len, m_init)
    ro, hs = _rwkv_call(xr, w["r_w0"], w["r_w2p"], w["r_a0"], w["r_a2p"], w["r_g2"], w["r_k_k"], w["r_k_a"],
                        w["r_r_k"], w["r_ln_g"], w["r_ln_b"], w["hones"], batch, seq_len, r_init)
    y = _ffn_call(x1, mod3, rows_for(FFN_TILE), 6, w["norm_ffn2"], w["ffn2_in"], w["ffn2_out"],
                  mix=(mo, ro, w["wo_m"], w["wo_r"]), g_final=w["norm_final"])
    states = (cs, ns, ms[:, :, :M_HEADS, 0], hs)
    return y.reshape(batch, seq_len, d), states


def _prepare_weights(ada_w, ada_b, norm_ffn1, ffn1_w_in, ffn1_w_out, norm_mix, mix_w_in, mix_w_out,
                     m_conv_w, m_conv_b, m_gate_b, m_head_g, r_mu, r_w0, r_w2, r_a0, r_a2, r_g2,
                     r_k_k, r_k_a, r_r_k, r_ln_g, r_ln_b, norm_ffn2, ffn2_w_in, ffn2_w_out, norm_final):
    assert ada_w.shape[0] == 1, "single trunk layer"
    lora = r_w2.shape[2]
    nm = 4 * M_WIDTH

    w_in = mix_w_in[0]
    gate_w = w_in[:, nm:nm + GATE_COLS]
    zpad = jnp.zeros((lora, R_WIDTH), F32)

    def dir_pad(w2):
        return jnp.stack([jnp.concatenate([w2[0], zpad], axis=0), jnp.concatenate([zpad, w2[1]], axis=0)])

    head_id = jnp.arange(LANES) // R_HEAD
    return dict(
        norm_ffn1=norm_ffn1, norm_mix=norm_mix, norm_ffn2=norm_ffn2, norm_final=norm_final[None],
        ffn1_in=ffn1_w_in[0].astype(BF16), ffn1_out=ffn1_w_out[0].astype(BF16),
        ffn2_in=ffn2_w_in[0].astype(BF16), ffn2_out=ffn2_w_out[0].astype(BF16),
        wqk=w_in[:, :2 * M_WIDTH].astype(BF16), wvot=w_in[:, 2 * M_WIDTH:nm].T.astype(BF16),
        wg=jnp.pad(gate_w, ((0, 0), (0, LANES - GATE_COLS))).astype(BF16),
        wgt=gate_w.T.astype(BF16),
        wr=w_in[:, nm + GATE_COLS:].astype(BF16),
        wo_m=mix_w_out[0, :M_WIDTH].astype(BF16), wo_r=mix_w_out[0, M_WIDTH:].astype(BF16),
        conv_w=m_conv_w[0], conv_b=m_conv_b, mu=r_mu[0],
        gate_b_row=jnp.pad(m_gate_b[0].reshape(1, GATE_COLS), ((0, 0), (0, LANES - GATE_COLS))),
        gate_b_col=m_gate_b[0].reshape(GATE_COLS, 1),
        head_g_col=m_head_g.reshape(M_WIDTH, 1),
        r_w0=r_w0[0][:, None, :], r_w2p=dir_pad(r_w2[0]).astype(BF16),
        r_a0=r_a0[0][:, None, :], r_a2p=dir_pad(r_a2[0]).astype(BF16),
        r_g2=r_g2[0].astype(BF16), r_k_k=r_k_k, r_k_a=r_k_a, r_r_k=r_r_k[0].reshape(1, R_WIDTH),
        r_ln_g=r_ln_g, r_ln_b=r_ln_b,
        hones=(head_id[:, None] == head_id[None, :]).astype(BF16),
    )


def kernel(x_prompt, x_sample, c, state_mlstm_C, state_mlstm_n, state_mlstm_m, state_rwkv_S, c_ctx,
           ada_w, ada_b, norm_ffn1, ffn1_w_in, ffn1_w_out, norm_mix, mix_w_in, mix_w_out,
           m_conv_w, m_conv_b, m_gate_b, m_head_g, r_mu, r_w0, r_w2, r_a0, r_a2, r_g2,
           r_k_k, r_k_a, r_r_k, r_ln_g, r_ln_b, norm_ffn2, ffn2_w_in, ffn2_w_out, norm_final):
    w = _prepare_weights(ada_w, ada_b, norm_ffn1, ffn1_w_in, ffn1_w_out, norm_mix, mix_w_in, mix_w_out,
                         m_conv_w, m_conv_b, m_gate_b, m_head_g, r_mu, r_w0, r_w2, r_a0, r_a2, r_g2,
                         r_k_k, r_k_a, r_r_k, r_ln_g, r_ln_b, norm_ffn2, ffn2_w_in, ffn2_w_out, norm_final)
    d = x_prompt.shape[-1]
    dec_batch = x_sample.shape[0]
    cond = jnp.concatenate([c_ctx[None], c, jnp.zeros((16 - 1 - dec_batch, d), F32)], axis=0)
    mod3 = _ada_call(cond, ada_w[0], ada_b).reshape(16, 9, d)

    y_prompt, (cs, ns, ms, ss) = _trunk(x_prompt, mod3, 0, False, False, None, w)
    init = (state_mlstm_C[:, 0], state_mlstm_n[:, 0], state_mlstm_m[:, 0], state_rwkv_S[:, 0])
    y_sample, _ = _trunk(x_sample, mod3, 1, True, True, init, w)
    return (y_prompt, y_sample, cs[:, None], ns[:, None], ms[:, None], ss[:, None])
```
